```python
import math
import jax, jax.numpy as jnp
from jax import lax
import numpy as np

D_MODEL = 4096
BATCH = 8
SEQ = 2048
DEPTH = 2
DEC_BATCH = 16
DEC_SEQ = 16
PAST_LEN = 4096

CHUNK = 64
WINDOW = 128
N_SWA_LAYERS = (DEPTH + 1) // 2
N_GDN_LAYERS = DEPTH // 2
ATTN_WIDTH = D_MODEL // 2
HEAD_DIM_A = 64
N_Q_HEADS = ATTN_WIDTH // HEAD_DIM_A
N_KV_HEADS = N_Q_HEADS // 8
GROUP = N_Q_HEADS // N_KV_HEADS
ROT_DIM = HEAD_DIM_A // 4
ROPE_THETA = 500000.0
SCONV_CH = D_MODEL - ATTN_WIDTH
SCONV_WIDTH = 3
KV_WIDTH = N_KV_HEADS * HEAD_DIM_A
MIX0_WIDTH = ATTN_WIDTH + SCONV_CH
IN0_WIDTH = ATTN_WIDTH + 2 * KV_WIDTH + 3 * SCONV_CH
GDN_HEAD_DIM = 128
N_GDN_HEADS = D_MODEL // GDN_HEAD_DIM
GDN_WIDTH = N_GDN_HEADS * GDN_HEAD_DIM
GDN_CONV_WIDTH = 4
QKV_CH = 3 * GDN_WIDTH
IN1_WIDTH = QKV_CH + GDN_WIDTH + 2 * N_GDN_HEADS
D_FF = 4 * D_MODEL
EPS = 1e-6
NEG_INF = -1e30

kernel_name = "hybrid_swa_sconv_gdn_stream_step"


def rmsnorm(x, g):
    xf = x.astype(jnp.float32)
    y = xf * lax.rsqrt(jnp.mean(xf * xf, axis=-1, keepdims=True) + EPS)
    return (y * g.astype(jnp.float32)).astype(x.dtype)


def rope_partial(x, pos):
    half = ROT_DIM // 2
    inv = ROPE_THETA ** (-jnp.arange(half, dtype=jnp.float32) / half)
    ang = pos.astype(jnp.float32)[:, None] * inv[None, :]
    cos = jnp.cos(ang)[None, :, None, :]
    sin = jnp.sin(ang)[None, :, None, :]
    xr = x[..., :ROT_DIM].astype(jnp.float32)
    x1, x2 = xr[..., :half], xr[..., half:]
    rot = jnp.concatenate([x1 * cos - x2 * sin, x2 * cos + x1 * sin], axis=-1)
    return jnp.concatenate([rot.astype(x.dtype), x[..., ROT_DIM:]], axis=-1)


def causal_dwconv(u, buf, w):
    width = w.shape[0]
    T = u.shape[1]
    uf = jnp.concatenate([buf.astype(u.dtype), u], axis=1)
    out = sum(uf[:, i:i + T] * w[i] for i in range(width))
    return out, uf[:, -(width - 1):]


def sink_attention(q, k, v, sinks, valid):
    s = jnp.einsum('nqhgd,nkhd->nhgqk', q.astype(jnp.float32), k.astype(jnp.float32)) * (HEAD_DIM_A ** -0.5)
    if valid is not None:
        s = jnp.where(valid[:, None, None, None, :], s, NEG_INF)
    sink = jnp.broadcast_to(sinks.astype(jnp.float32).reshape(1, N_KV_HEADS, GROUP, 1, 1), s.shape[:-1] + (1,))
    p = jax.nn.softmax(jnp.concatenate([s, sink], axis=-1), axis=-1)[..., :-1]
    o = jnp.einsum('nhgqk,nkhd->nqhgd', p, v.astype(jnp.float32))
    return o.astype(v.dtype)


def banded_swa(q, k, v, sinks):
    N, S = q.shape[:2]
    nc = S // CHUNK
    nb = WINDOW // CHUNK
    band = (nb + 1) * CHUNK
    qb = q.reshape(N * nc, CHUNK, N_KV_HEADS, GROUP, HEAD_DIM_A)
    pad = ((0, 0), (WINDOW, 0), (0, 0), (0, 0))
    kc = jnp.pad(k, pad).reshape(N, nc + nb, CHUNK, N_KV_HEADS, HEAD_DIM_A)
    vc = jnp.pad(v, pad).reshape(N, nc + nb, CHUNK, N_KV_HEADS, HEAD_DIM_A)
    kband = jnp.concatenate([kc[:, i:i + nc] for i in range(nb + 1)], axis=2).reshape(N * nc, band, N_KV_HEADS, HEAD_DIM_A)
    vband = jnp.concatenate([vc[:, i:i + nc] for i in range(nb + 1)], axis=2).reshape(N * nc, band, N_KV_HEADS, HEAD_DIM_A)
    key_pos = jnp.arange(nc)[:, None] * CHUNK - WINDOW + jnp.arange(band)[None, :]
    valid = jnp.broadcast_to((key_pos >= 0)[None], (N, nc, band)).reshape(N * nc, band)
    o = sink_attention(qb, kband, vband, sinks, valid)
    return o.reshape(N, S, ATTN_WIDTH)


def swa_sconv_mixer(x, pos, k_cache, v_cache, conv_buf, norm_g, w_in, sinks, conv_w, w_out):
    N, T, _ = x.shape
    h = rmsnorm(x, norm_g)
    proj = h @ w_in
    splits = np.cumsum([ATTN_WIDTH, KV_WIDTH, KV_WIDTH, SCONV_CH, SCONV_CH]).tolist()
    q, k, v, b_gate, c_gate, xc = jnp.split(proj, splits, axis=-1)
    q = rope_partial(q.reshape(N, T, N_Q_HEADS, HEAD_DIM_A), pos)
    k = rope_partial(k.reshape(N, T, N_KV_HEADS, HEAD_DIM_A), pos)
    v = v.reshape(N, T, N_KV_HEADS, HEAD_DIM_A)
    if k_cache is None:
        attn = banded_swa(q, k, v, sinks)
        k_all, v_all = k, v
        conv_buf = jnp.zeros((N, SCONV_WIDTH - 1, SCONV_CH), x.dtype)
    else:
        k_all = jnp.concatenate([k_cache.astype(k.dtype), k], axis=1)
        v_all = jnp.concatenate([v_cache.astype(v.dtype), v], axis=1)
        o = sink_attention(q.reshape(N, T, N_KV_HEADS, GROUP, HEAD_DIM_A), k_all, v_all, sinks, None)
        attn = o.reshape(N, T, ATTN_WIDTH)
    conv, new_buf = causal_dwconv(c_gate * xc, conv_buf, conv_w)
    y = jnp.concatenate([attn.astype(x.dtype), b_gate * conv], axis=-1) @ w_out
    return x + y, k_all[:, -WINDOW:], v_all[:, -WINDOW:], new_buf


def gated_delta_rule(q, k, v, g, beta, S0, chunk):
    N, T, H, _ = q.shape
    nc = T // chunk

    def blk(a):
        return a.reshape(N, nc, chunk, H, *a.shape[3:]).swapaxes(2, 3)

    q, k, v, g, beta = blk(q), blk(k), blk(v), blk(g), blk(beta)
    G = jnp.cumsum(g, axis=-1)
    idx = jnp.arange(chunk)
    lower_incl = idx[:, None] >= idx[None, :]
    strict = idx[:, None] > idx[None, :]
    decay_mat = jnp.exp(jnp.where(lower_incl, G[..., :, None] - G[..., None, :], -jnp.inf))
    kb = k * beta[..., None]
    A = jnp.where(strict, jnp.einsum('...id,...jd->...ij', kb, k) * decay_mat, 0.0)
    eye = jnp.eye(chunk, dtype=jnp.float32)
    rhs = jnp.concatenate([v * beta[..., None], kb * jnp.exp(G)[..., None]], axis=-1)
    sol = lax.linalg.triangular_solve(eye + A, rhs, left_side=True, lower=True)
    dv = v.shape[-1]
    u_base, w = sol[..., :dv], sol[..., dv:]
    aqk = jnp.einsum('...id,...jd->...ij', q, k) * decay_mat
    q_dec = q * jnp.exp(G)[..., None]
    k_dec = k * jnp.exp(G[..., -1:] - G)[..., None]
    blk_decay = jnp.exp(G[..., -1])

    def step(S, xs):
        u_b, w_c, a_c, qd, kd, cd = xs
        u = u_b - jnp.einsum('nhld,nhdv->nhlv', w_c, S)
        o = jnp.einsum('nhld,nhdv->nhlv', qd, S) + jnp.einsum('nhij,nhjv->nhiv', a_c, u)
        S = S * cd[..., None, None] + jnp.einsum('nhld,nhlv->nhdv', kd, u)
        return S, o

    xs = tuple(jnp.moveaxis(a, 1, 0) for a in (u_base, w, aqk, q_dec, k_dec, blk_decay))
    S, o = lax.scan(step, S0, xs)
    o = o.transpose(1, 0, 3, 2, 4).reshape(N, T, H, dv)
    return o, S


def gdn_mixer(x, conv_buf, S0, norm_g, w_in, conv_w, A_log, dt_bias, o_norm_w, w_out, chunk):
    N, T, _ = x.shape
    h = rmsnorm(x, norm_g)
    proj = h @ w_in
    splits = np.cumsum([QKV_CH, GDN_WIDTH, N_GDN_HEADS]).tolist()
    qkv, gate, a, b = jnp.split(proj, splits, axis=-1)
    if conv_buf is None:
        conv_buf = jnp.zeros((N, GDN_CONV_WIDTH - 1, QKV_CH), x.dtype)
    if S0 is None:
        S0 = jnp.zeros((N, N_GDN_HEADS, GDN_HEAD_DIM, GDN_HEAD_DIM), jnp.float32)
    qkv_c, new_buf = causal_dwconv(qkv, conv_buf, conv_w)
    qkv_c = jax.nn.silu(qkv_c.astype(jnp.float32))
    q, k, v = jnp.split(qkv_c, [GDN_WIDTH, 2 * GDN_WIDTH], axis=-1)
    shp = (N, T, N_GDN_HEADS, GDN_HEAD_DIM)
    q, k, v = q.reshape(shp), k.reshape(shp), v.reshape(shp)
    q = q * lax.rsqrt(jnp.sum(q * q, axis=-1, keepdims=True) + EPS) * (GDN_HEAD_DIM ** -0.5)
    k = k * lax.rsqrt(jnp.sum(k * k, axis=-1, keepdims=True) + EPS)
    beta = jax.nn.sigmoid(b.astype(jnp.float32))
    g = -jnp.exp(A_log.astype(jnp.float32)) * jax.nn.softplus(a.astype(jnp.float32) + dt_bias.astype(jnp.float32))
    o, S = gated_delta_rule(q, k, v, g, beta, S0.astype(jnp.float32), chunk)
    o = o * lax.rsqrt(jnp.mean(o * o, axis=-1, keepdims=True) + EPS) * o_norm_w.astype(jnp.float32)
    o = o * jax.nn.silu(gate.astype(jnp.float32).reshape(shp))
    y = o.reshape(N, T, GDN_WIDTH).astype(x.dtype) @ w_out
    return x + y, new_buf, S.astype(x.dtype)


def sqrelu_mlp(x, norm_g, w_up, w_down):
    h = rmsnorm(x, norm_g)
    return x + jnp.square(jax.nn.relu(h @ w_up)) @ w_down


def setup_inputs(seed: int = 0) -> dict:
    key = jax.random.key(seed)
    ks = iter(jax.random.split(key, 32))
    f32 = jnp.float32

    def nrm(shape, scale):
        return scale * jax.random.normal(next(ks), shape, f32)

    def gain(shape):
        return 1.0 + nrm(shape, 0.02)

    NS, NG = N_SWA_LAYERS, N_GDN_LAYERS
    n_win = min(WINDOW, PAST_LEN)
    x_prompt = nrm((BATCH, SEQ, D_MODEL), 1.0)
    x_sample = nrm((DEC_BATCH, DEC_SEQ, D_MODEL), 1.0)
    cache_swa_k = nrm((NS, DEC_BATCH, n_win, N_KV_HEADS, HEAD_DIM_A), 1.0)
    cache_swa_v = nrm((NS, DEC_BATCH, n_win, N_KV_HEADS, HEAD_DIM_A), 1.0)
    state_sconv = nrm((NS, DEC_BATCH, SCONV_WIDTH - 1, SCONV_CH), 1.0)
    state_dn_conv = nrm((NG, DEC_BATCH, GDN_CONV_WIDTH - 1, QKV_CH), 1.0)
    state_dn = nrm((NG, DEC_BATCH, N_GDN_HEADS, GDN_HEAD_DIM, GDN_HEAD_DIM), 0.5)
    attn_norm = gain((NS, D_MODEL))
    w_in_a = nrm((NS, D_MODEL, IN0_WIDTH), D_MODEL ** -0.5)
    sinks = nrm((NS, N_Q_HEADS), 0.5)
    sconv_w = nrm((NS, SCONV_WIDTH, SCONV_CH), SCONV_WIDTH ** -0.5)
    w_out_a = nrm((NS, MIX0_WIDTH, D_MODEL), MIX0_WIDTH ** -0.5)
    dn_norm = gain((NG, D_MODEL))
    w_in_g = nrm((NG, D_MODEL, IN1_WIDTH), D_MODEL ** -0.5)
    dn_conv_w = nrm((NG, GDN_CONV_WIDTH, QKV_CH), GDN_CONV_WIDTH ** -0.5)
    A_log = jnp.log(jax.random.uniform(next(ks), (NG, N_GDN_HEADS), f32, 1.0, 16.0))
    dt = jnp.exp(jax.random.uniform(next(ks), (NG, N_GDN_HEADS), f32, math.log(1e-3), math.log(1e-1)))
    dt_bias = dt + jnp.log(-jnp.expm1(-dt))
    o_norm_w = gain((NG, GDN_HEAD_DIM))
    w_out_g = nrm((NG, GDN_WIDTH, D_MODEL), GDN_WIDTH ** -0.5)
    mlp_norm = gain((DEPTH, D_MODEL))
    w_up = nrm((DEPTH, D_MODEL, D_FF), D_MODEL ** -0.5)
    w_down = nrm((DEPTH, D_FF, D_MODEL), D_FF ** -0.5)
    final_norm = gain((D_MODEL,))
    return {
        "x_prompt": x_prompt, "x_sample": x_sample,
        "cache_swa_k": cache_swa_k, "cache_swa_v": cache_swa_v, "state_sconv": state_sconv,
        "state_dn_conv": state_dn_conv, "state_dn": state_dn,
        "attn_norm": attn_norm, "w_in_a": w_in_a, "sinks": sinks, "sconv_w": sconv_w, "w_out_a": w_out_a,
        "dn_norm": dn_norm, "w_in_g": w_in_g, "dn_conv_w": dn_conv_w, "A_log": A_log, "dt_bias": dt_bias,
        "o_norm_w": o_norm_w, "w_out_g": w_out_g,
        "mlp_norm": mlp_norm, "w_up": w_up, "w_down": w_down, "final_norm": final_norm,
    }


def reference(x_prompt, x_sample, cache_swa_k, cache_swa_v, state_sconv, state_dn_conv, state_dn,
              attn_norm, w_in_a, sinks, sconv_w, w_out_a,
              dn_norm, w_in_g, dn_conv_w, A_log, dt_bias, o_norm_w, w_out_g,
              mlp_norm, w_up, w_down, final_norm):
    xp, xs = x_prompt, x_sample
    pos_p = jnp.arange(xp.shape[1])
    pos_s = PAST_LEN + jnp.arange(xs.shape[1])
    kp_l, vp_l, ks_l, vs_l, cp_l, cs_l = [], [], [], [], [], []
    dcp_l, dcs_l, dsp_l, dss_l = [], [], [], []
    for li in range(DEPTH):
        j = li // 2
        if li % 2 == 0:
            xp, kp, vp, cp = swa_sconv_mixer(xp, pos_p, None, None, None,
                                             attn_norm[j], w_in_a[j], sinks[j], sconv_w[j], w_out_a[j])
            xs, ks_, vs_, cs = swa_sconv_mixer(xs, pos_s, cache_swa_k[j], cache_swa_v[j], state_sconv[j],
                                               attn_norm[j], w_in_a[j], sinks[j], sconv_w[j], w_out_a[j])
            kp_l.append(kp); vp_l.append(vp); ks_l.append(ks_); vs_l.append(vs_)
            cp_l.append(cp); cs_l.append(cs)
        else:
            xp, dcp, dsp = gdn_mixer(xp, None, None, dn_norm[j], w_in_g[j], dn_conv_w[j], A_log[j],
                                     dt_bias[j], o_norm_w[j], w_out_g[j], CHUNK)
            xs, dcs, dss = gdn_mixer(xs, state_dn_conv[j], state_dn[j], dn_norm[j], w_in_g[j], dn_conv_w[j],
                                     A_log[j], dt_bias[j], o_norm_w[j], w_out_g[j], xs.shape[1])
            dcp_l.append(dcp); dcs_l.append(dcs); dsp_l.append(dsp); dss_l.append(dss)
        xp = sqrelu_mlp(xp, mlp_norm[li], w_up[li], w_down[li])
        xs = sqrelu_mlp(xs, mlp_norm[li], w_up[li], w_down[li])
    y_prompt = rmsnorm(xp, final_norm)
    y_sample = rmsnorm(xs, final_norm)
    swa_k_prompt = jnp.stack(kp_l, 0)
    swa_v_prompt = jnp.stack(vp_l, 0)
    swa_k_sample = jnp.stack(ks_l, 0)
    swa_v_sample = jnp.stack(vs_l, 0)
    sconv_prompt = jnp.stack(cp_l, 0)
    sconv_sample = jnp.stack(cs_l, 0)
    dn_conv_prompt = jnp.stack(dcp_l, 0)
    dn_conv_sample = jnp.stack(dcs_l, 0)
    dn_state_prompt = jnp.stack(dsp_l, 0)
    dn_state_sample = jnp.stack(dss_l, 0)
    return (y_prompt, y_sample, swa_k_prompt, swa_v_prompt, swa_k_sample, swa_v_sample,
            sconv_prompt, sconv_sample, dn_conv_prompt, dn_conv_sample, dn_state_prompt, dn_state_sample)
```

```python
import functools

import jax
import jax.numpy as jnp
from jax import lax
from jax.experimental import pallas as pl
from jax.experimental.pallas import tpu as pltpu

EPS = 1e-6
CHUNK = 64
WINDOW = 128
PAST_LEN = 4096
ROPE_THETA = 500000.0
NEG_INF = -1e30
LANES = 128
SUBLANES = 8
VMEM_LIMIT = 56 * 1024 * 1024
BF16 = jnp.bfloat16
F32 = jnp.float32


def _params(*sem):
    return pltpu.CompilerParams(dimension_semantics=sem, vmem_limit_bytes=VMEM_LIMIT)


def _tile(n, pref, align=SUBLANES):
    if n <= pref:
        return n
    t = pref - pref % align
    while t >= align:
        if n % t == 0:
            return t
        t -= align
    return n


def _rmsnorm_kernel(x_ref, g_ref, o_ref):
    x = x_ref[...]
    ms = jnp.mean(x * x, axis=-1, keepdims=True)
    o_ref[...] = (x * lax.rsqrt(ms + EPS) * g_ref[...]).astype(o_ref.dtype)


def _rmsnorm(x, g, out_dtype):
    m, d = x.shape
    tm = _tile(m, 256)
    return pl.pallas_call(
        _rmsnorm_kernel,
        grid=(m // tm,),
        in_specs=[pl.BlockSpec((tm, d), lambda i: (i, 0)),
                  pl.BlockSpec((1, d), lambda i: (0, 0))],
        out_specs=pl.BlockSpec((tm, d), lambda i: (i, 0)),
        out_shape=jax.ShapeDtypeStruct((m, d), out_dtype),
        compiler_params=_params("parallel"),
        name="rmsnorm",
    )(x, g.reshape(1, d))


def _mm_kernel(*refs, nk, act, has_res):
    a_ref, w_ref = refs[0], refs[1]
    r_ref = refs[2] if has_res else None
    o_ref = refs[2 + has_res]

    def finish(acc):
        if act == "relu2":
            acc = jnp.square(jnp.maximum(acc, 0.0))
        if has_res:
            acc = r_ref[...] + acc
        o_ref[...] = acc.astype(o_ref.dtype)

    part = jnp.dot(a_ref[...], w_ref[...], preferred_element_type=F32)
    if nk == 1:
        finish(part)
    else:
        acc_ref = refs[3 + has_res]
        k = pl.program_id(2)

        @pl.when(k == 0)
        def _():
            acc_ref[...] = part

        @pl.when(k > 0)
        def _():
            acc_ref[...] += part

        @pl.when(k == nk - 1)
        def _():
            finish(acc_ref[...])


def _matmul(a, w, *, res=None, act=None, out_dtype=F32, name="matmul"):
    m, kdim = a.shape
    n = w.shape[1]
    tm = _tile(m, 1024)
    tn = n if n < LANES else _tile(n, 1024, LANES)
    tk = _tile(kdim, 4096 if kdim <= 4096 else 2048, LANES)
    nk = kdim // tk
    in_specs = [pl.BlockSpec((tm, tk), lambda i, j, k: (i, k)),
                pl.BlockSpec((tk, tn), lambda i, j, k: (k, j))]
    args = [a, w]
    if res is not None:
        in_specs.append(pl.BlockSpec((tm, tn), lambda i, j, k: (i, j)))
        args.append(res)
    return pl.pallas_call(
        functools.partial(_mm_kernel, nk=nk, act=act, has_res=res is not None),
        grid=(m // tm, n // tn, nk),
        in_specs=in_specs,
        out_specs=pl.BlockSpec((tm, tn), lambda i, j, k: (i, j)),
        out_shape=jax.ShapeDtypeStruct((m, n), out_dtype),
        scratch_shapes=[pltpu.VMEM((tm, tn), F32)] if nk > 1 else [],
        compiler_params=_params("parallel", "parallel", "arbitrary"),
        name=name,
    )(*args)


def _rope_table_kernel(inv_ref, c_ref, s_ref, *, pos0, hd):
    shape = c_ref.shape
    pos = (pos0 + lax.broadcasted_iota(jnp.int32, shape, 0)).astype(F32)
    ang = pos * inv_ref[...]
    d = jnp.bitwise_and(lax.broadcasted_iota(jnp.int32, shape, 1), hd - 1)
    rot = hd // 4
    cos, sin = jnp.cos(ang), jnp.sin(ang)
    c_ref[...] = jnp.where(d < rot, cos, 1.0)
    s_ref[...] = jnp.where(d < rot // 2, -sin, jnp.where(d < rot, sin, 0.0))


def _rope_tables(n_pos, pos0, hd):
    half = hd // 8
    inv = ROPE_THETA ** (-jnp.arange(half, dtype=F32) / half)
    inv_lane = jnp.tile(jnp.concatenate([inv, inv, jnp.zeros((hd - 2 * half,), F32)]), LANES // hd)
    shp = jax.ShapeDtypeStruct((n_pos, LANES), F32)
    return pl.pallas_call(
        functools.partial(_rope_table_kernel, pos0=pos0, hd=hd),
        out_shape=(shp, shp),
        name="rope_tables",
    )(inv_lane.reshape(1, LANES))


def _rope(x, c, s, hd):
    w = x.shape[1]
    reps = w // LANES
    cf = jnp.concatenate([c] * reps, axis=1) if reps > 1 else c
    sf = jnp.concatenate([s] * reps, axis=1) if reps > 1 else s
    d = jnp.bitwise_and(lax.broadcasted_iota(jnp.int32, x.shape, 1), hd - 1)
    half = hd // 8
    fwd = pltpu.roll(x, w - half, 1)
    bwd = pltpu.roll(x, half, 1)
    sw = jnp.where(d < half, fwd, jnp.where(d < 2 * half, bwd, 0.0))
    return x * cf + sw * sf


def _rope_k_kernel(k_ref, c_ref, s_ref, o_ref, *, hd):
    o_ref[0] = _rope(k_ref[0], c_ref[...], s_ref[...], hd)


def _rope_k(proj, cos_t, sin_t, col0, width, hd):
    n, t, _ = proj.shape
    tb = _tile(t, 512)
    return pl.pallas_call(
        functools.partial(_rope_k_kernel, hd=hd),
        grid=(n, t // tb),
        in_specs=[pl.BlockSpec((1, tb, width), lambda i, j: (i, j, col0 // width)),
                  pl.BlockSpec((tb, LANES), lambda i, j: (j, 0)),
                  pl.BlockSpec((tb, LANES), lambda i, j: (j, 0))],
        out_specs=pl.BlockSpec((1, tb, width), lambda i, j: (i, j, 0)),
        out_shape=jax.ShapeDtypeStruct((n, t, width), F32),
        compiler_params=_params("parallel", "parallel"),
        name="rope_k",
    )(proj, cos_t, sin_t)


def _attn_kernel(sink_ref, q_ref, k_ref, v_ref, c_ref, s_ref, o_ref, *,
                 rows, band, window, n_kv, group, hd):
    tb = q_ref.shape[1]
    gw = group * hd
    for ci in range(tb // rows):
        r0 = ci * rows
        if window:
            c_abs = pl.program_id(1) * (tb // rows) + ci
            lo = c_abs * rows - window
            start = pl.multiple_of(jnp.maximum(lo, 0), rows)
            kpos = start + lax.broadcasted_iota(jnp.int32, (1, band), 1)
            valid = jnp.logical_and(kpos >= lo, kpos < lo + band)
            ksl = pl.ds(start, band)
        else:
            ksl = slice(0, band)
        c = c_ref[r0:r0 + rows, :]
        s = s_ref[r0:r0 + rows, :]
        for h in range(n_kv):
            qh = _rope(q_ref[0, r0:r0 + rows, h * gw:(h + 1) * gw], c, s, hd)
            qs = jnp.concatenate([qh[:, g * hd:(g + 1) * hd] for g in range(group)], axis=0)
            kh = k_ref[0, ksl, h * hd:(h + 1) * hd].astype(BF16)
            vh = v_ref[0, ksl, h * hd:(h + 1) * hd].astype(BF16)
            sc = lax.dot_general(qs.astype(BF16), kh, (((1,), (1,)), ((), ())),
                                 preferred_element_type=F32) * (hd ** -0.5)
            if window:
                sc = jnp.where(valid, sc, NEG_INF)
            sink = jnp.concatenate(
                [jnp.full((rows, 1), sink_ref[h * group + g], F32) for g in range(group)], axis=0)
            m = jnp.maximum(jnp.max(sc, axis=-1, keepdims=True), sink)
            p = jnp.exp(sc - m)
            denom = jnp.sum(p, axis=-1, keepdims=True) + jnp.exp(sink - m)
            o = jnp.dot(p.astype(BF16), vh, preferred_element_type=F32) / denom
            o_ref[0, r0:r0 + rows, h * gw:(h + 1) * gw] = jnp.concatenate(
                [o[g * rows:(g + 1) * rows, :] for g in range(group)], axis=1).astype(o_ref.dtype)


def _attention(proj, k_all, v_all, cos_t, sin_t, sinks, *, aw, hd, window):
    n, t, _ = proj.shape
    tk, kvw = k_all.shape[1], k_all.shape[2]
    n_kv = kvw // hd
    group = aw // kvw
    if window:
        rows, band = CHUNK, window + CHUNK
        tb = _tile(t, 4 * CHUNK, CHUNK)
    else:
        rows, band, tb = t, tk, t
    return pl.pallas_call(
        functools.partial(_attn_kernel, rows=rows, band=band, window=window,
                          n_kv=n_kv, group=group, hd=hd),
        grid=(n, t // tb),
        in_specs=[pl.BlockSpec(memory_space=pltpu.SMEM),
                  pl.BlockSpec((1, tb, aw), lambda i, j: (i, j, 0)),
                  pl.BlockSpec((1, tk, kvw), lambda i, j: (i, 0, 0)),
                  pl.BlockSpec((1, tk, kvw), lambda i, j: (i, 0, 0)),
                  pl.BlockSpec((tb, LANES), lambda i, j: (j, 0)),
                  pl.BlockSpec((tb, LANES), lambda i, j: (j, 0))],
        out_specs=pl.BlockSpec((1, tb, aw), lambda i, j: (i, j, 0)),
        out_shape=jax.ShapeDtypeStruct((n, t, aw), BF16),
        compiler_params=_params("parallel", "parallel"),
        name="swa_attention",
    )(sinks, proj, k_all, v_all, cos_t, sin_t)


def _sconv_kernel(b_ref, c_ref, x_ref, st_ref, w_ref, z_ref, ns_ref, buf):
    t = x_ref.shape[1]
    width = w_ref.shape[0]
    buf[SUBLANES:SUBLANES + t, :] = c_ref[0] * x_ref[0]
    buf[SUBLANES - (width - 1):SUBLANES, :] = st_ref[0]
    lo = SUBLANES - (width - 1)
    acc = buf[lo:lo + t, :] * w_ref[0:1, :]
    for i in range(1, width):
        acc = acc + buf[lo + i:lo + i + t, :] * w_ref[i:i + 1, :]
    z_ref[0] = (b_ref[0] * acc).astype(z_ref.dtype)
    ns_ref[0] = buf[SUBLANES + t - (width - 1):SUBLANES + t, :]


def _sconv(proj, state, w, col_b, col_c, col_x):
    n, t, _ = proj.shape
    width, ch = w.shape
    cb = next(c for c in (256, LANES) if all(v % c == 0 for v in (ch, col_b, col_c, col_x)))
    blk = lambda off: pl.BlockSpec((1, t, cb), lambda i, j: (i, 0, off // cb + j))
    return pl.pallas_call(
        _sconv_kernel,
        grid=(n, ch // cb),
        in_specs=[blk(col_b), blk(col_c), blk(col_x),
                  pl.BlockSpec((1, width - 1, cb), lambda i, j: (i, 0, j)),
                  pl.BlockSpec((width, cb), lambda i, j: (0, j))],
        out_specs=[pl.BlockSpec((1, t, cb), lambda i, j: (i, 0, j)),
                   pl.BlockSpec((1, width - 1, cb), lambda i, j: (i, 0, j))],
        out_shape=[jax.ShapeDtypeStruct((n, t, ch), BF16),
                   jax.ShapeDtypeStruct((n, width - 1, ch), F32)],
        scratch_shapes=[pltpu.VMEM((t + SUBLANES, cb), F32)],
        compiler_params=_params("parallel", "parallel"),
        name="sconv",
    )(proj, proj, proj, state, w)


def _gdn_prep_kernel(x_ref, st_ref, w_ref, o_ref, buf, *, mode, hd):
    t = x_ref.shape[1]
    width = w_ref.shape[0]
    buf[SUBLANES:SUBLANES + t, :] = x_ref[0]
    buf[SUBLANES - (width - 1):SUBLANES, :] = st_ref[0]
    lo = SUBLANES - (width - 1)
    acc = buf[lo:lo + t, :] * w_ref[0:1, :]
    for i in range(1, width):
        acc = acc + buf[lo + i:lo + i + t, :] * w_ref[i:i + 1, :]
    y = acc * (1.0 / (1.0 + jnp.exp(-acc)))
    for h in range(o_ref.shape[1]):
        yh = y[:, h * hd:(h + 1) * hd]
        if mode != "v":
            yh = yh * lax.rsqrt(jnp.sum(yh * yh, axis=-1, keepdims=True) + EPS)
        if mode == "q":
            yh = yh * (hd ** -0.5)
        o_ref[0, h] = yh


def _gdn_prep(proj, state, w, col0, n_heads, hd, mode):
    n, t, _ = proj.shape
    width = w.shape[0]
    hb = 2
    cb = hb * hd
    return pl.pallas_call(
        functools.partial(_gdn_prep_kernel, mode=mode, hd=hd),
        grid=(n, n_heads // hb),
        in_specs=[pl.BlockSpec((1, t, cb), lambda i, j: (i, 0, col0 // cb + j)),
                  pl.BlockSpec((1, width - 1, cb), lambda i, j: (i, 0, col0 // cb + j)),
                  pl.BlockSpec((width, cb), lambda i, j: (0, col0 // cb + j))],
        out_specs=pl.BlockSpec((1, hb, t, hd), lambda i, j: (i, j, 0, 0)),
        out_shape=jax.ShapeDtypeStruct((n, n_heads, t, hd), F32),
        scratch_shapes=[pltpu.VMEM((t + SUBLANES, cb), F32)],
        compiler_params=_params("parallel", "parallel"),
        name="gdn_prep_" + mode,
    )(proj, state, w)


def _softplus(x):
    return jnp.maximum(x, 0.0) + jnp.log1p(jnp.exp(-jnp.abs(x)))


def _gates_kernel(h_ref, w_ref, wt_ref, al_ref, alt_ref, dt_ref, dtt_ref,
                  gr_ref, br_ref, gc_ref, bc_ref):
    hx = h_ref[0]
    nh = al_ref.shape[1]
    ell = hx.shape[0]
    ab = jnp.dot(hx, w_ref[...], preferred_element_type=F32)
    abt = lax.dot_general(wt_ref[...], hx, (((1,), (1,)), ((), ())),
                          preferred_element_type=F32)
    g_col = -jnp.exp(al_ref[...]) * _softplus(ab[:, :nh] + dt_ref[...])
    g_row = -jnp.exp(alt_ref[...]) * _softplus(abt[:nh, :] + dtt_ref[...])
    i = lax.broadcasted_iota(jnp.int32, (ell, ell), 0)
    j = lax.broadcasted_iota(jnp.int32, (ell, ell), 1)
    lower = (i >= j).astype(F32)
    upper = (i <= j).astype(F32)
    gc_ref[0, 0] = jnp.dot(lower, g_col, preferred_element_type=F32, precision=lax.Precision.HIGHEST)
    gr_ref[0, 0] = jnp.dot(g_row, upper, preferred_element_type=F32, precision=lax.Precision.HIGHEST)
    bc_ref[0, 0] = 1.0 / (1.0 + jnp.exp(-ab[:, nh:]))
    br_ref[0, 0] = 1.0 / (1.0 + jnp.exp(-abt[nh:, :]))


def _gates(hn, w_ab, a_log, dt_bias, ell):
    n, t, d = hn.shape
    nh = a_log.shape[0]
    nc = t // ell
    row = jax.ShapeDtypeStruct((n, nc, nh, ell), F32)
    col = jax.ShapeDtypeStruct((n, nc, ell, nh), F32)
    full = lambda shape: pl.BlockSpec(shape, lambda i, j: (0,) * len(shape))
    return pl.pallas_call(
        _gates_kernel,
        grid=(n, nc),
        in_specs=[pl.BlockSpec((1, ell, d), lambda i, j: (i, j, 0)),
                  full((d, 2 * nh)), full((2 * nh, d)),
                  full((1, nh)), full((nh, 1)), full((1, nh)), full((nh, 1))],
        out_specs=[pl.BlockSpec((1, 1, nh, ell), lambda i, j: (i, j, 0, 0)),
                   pl.BlockSpec((1, 1, nh, ell), lambda i, j: (i, j, 0, 0)),
                   pl.BlockSpec((1, 1, ell, nh), lambda i, j: (i, j, 0, 0)),
                   pl.BlockSpec((1, 1, ell, nh), lambda i, j: (i, j, 0, 0))],
        out_shape=[row, row, col, col],
        compiler_params=_params("parallel", "parallel"),
        name="gdn_gates",
    )(hn, w_ab, w_ab.T, a_log.reshape(1, nh), a_log.reshape(nh, 1),
      dt_bias.reshape(1, nh), dt_bias.reshape(nh, 1))


def _bdot(a, b):
    return jnp.dot(a.astype(BF16), b.astype(BF16), preferred_element_type=F32)


def _tdot(a, b):
    return jnp.dot(a.astype(BF16), b.astype(BF16), preferred_element_type=F32)


def _scan_kernel(q_ref, k_ref, v_ref, gate_ref, gr_ref, br_ref, gc_ref, bc_ref, s0_ref, ow_ref,
                 o_ref, sout_ref, s_scr, cg_scr, cb_scr, o_scr, *, unroll):
    nh, ell, hd = q_ref.shape[1], q_ref.shape[2], q_ref.shape[3]
    cidx = pl.program_id(1)

    @pl.when(cidx == 0)
    def _():
        s_scr[...] = s0_ref[0]

    gcol = gc_ref[0, 0]
    bcol = bc_ref[0, 0]
    for h in range(nh):
        cg_scr[h] = jnp.broadcast_to(gcol[:, h:h + 1], (ell, hd))
        cb_scr[h] = jnp.broadcast_to(bcol[:, h:h + 1], (ell, hd))

    i = lax.broadcasted_iota(jnp.int32, (ell, ell), 0)
    j = lax.broadcasted_iota(jnp.int32, (ell, ell), 1)
    eye = (i == j).astype(F32)
    shifts = range(3, ell.bit_length())
    blk = [jnp.right_shift(i, sh) == jnp.right_shift(j, sh) for sh in shifts]

    def head(h, carry):
        q, k, v = q_ref[0, h], k_ref[0, h], v_ref[0, h]
        g_row = gr_ref[0, 0, pl.ds(h, 1), :]
        b_row = br_ref[0, 0, pl.ds(h, 1), :]
        g_col = cg_scr[h]
        b_col = cb_scr[h]
        k16 = k.astype(BF16)
        nt = (((1,), (1,)), ((), ()))
        kk = lax.dot_general(k16, k16, nt, preferred_element_type=F32)
        qk = lax.dot_general(q.astype(BF16), k16, nt, preferred_element_type=F32)
        decay = jnp.exp(jnp.where(i >= j, g_col[:, :ell] - g_row, -jnp.inf))
        a = jnp.where(i > j, b_col[:, :ell] * kk * decay, 0.0)
        p = -jnp.where(blk[0], a, 0.0)
        tm = eye + p
        for _ in range(2):
            p = _tdot(p, p)
            tm = tm + _tdot(tm, p)
        for lvl in range(1, len(blk)):
            low = jnp.where(jnp.logical_and(blk[lvl], jnp.logical_not(blk[lvl - 1])), a, 0.0)
            tm = tm - _tdot(_tdot(tm, low), tm)
        u_base = _bdot(tm * b_row, v)
        w = _bdot(tm * (b_row * jnp.exp(g_row)), k16)
        s = s_scr[h]
        u = u_base - _bdot(w, s)
        o = _bdot(q, s) * jnp.exp(g_col) + _bdot(qk * decay, u)
        g_last = g_row[:, ell - 1:ell]
        k_dec = k * jnp.exp(g_last - g_col)
        s_scr[h] = s * jnp.exp(g_last) + lax.dot_general(
            k_dec.astype(BF16), u.astype(BF16), (((0,), (0,)), ((), ())), preferred_element_type=F32)
        o_scr[h] = o
        return carry

    lax.fori_loop(0, nh, head, 0, unroll=unroll)

    for h in range(nh):
        o = o_scr[h]
        o = o * lax.rsqrt(jnp.mean(o * o, axis=-1, keepdims=True) + EPS) * ow_ref[...]
        gt = gate_ref[0, :, h * hd:(h + 1) * hd]
        o_ref[0, :, h * hd:(h + 1) * hd] = (o * (gt * (1.0 / (1.0 + jnp.exp(-gt))))).astype(o_ref.dtype)

    @pl.when(cidx == pl.num_programs(1) - 1)
    def _():
        sout_ref[0] = s_scr[...]


def _gdn_scan(q, k, v, proj, gate_col, gates, s0, o_norm_w, ell):
    n, nh, t, hd = q.shape
    nc = t // ell
    hw = nh * hd
    g_row, b_row, g_col, b_col = gates
    qkv_spec = pl.BlockSpec((1, nh, ell, hd), lambda i, c: (i, 0, c, 0))
    row_spec = pl.BlockSpec((1, 1, nh, ell), lambda i, c: (i, c, 0, 0))
    col_spec = pl.BlockSpec((1, 1, ell, nh), lambda i, c: (i, c, 0, 0))
    st_spec = pl.BlockSpec((1, nh, hd, hd), lambda i, c: (i, 0, 0, 0))
    return pl.pallas_call(
        functools.partial(_scan_kernel, unroll=2),
        grid=(n, nc),
        in_specs=[qkv_spec, qkv_spec, qkv_spec,
                  pl.BlockSpec((1, ell, hw), lambda i, c: (i, c, gate_col // hw)),
                  row_spec, row_spec, col_spec, col_spec, st_spec,
                  pl.BlockSpec((1, hd), lambda i, c: (0, 0))],
        out_specs=[pl.BlockSpec((1, ell, hw), lambda i, c: (i, c, 0)), st_spec],
        out_shape=[jax.ShapeDtypeStruct((n, t, hw), BF16),
                   jax.ShapeDtypeStruct((n, nh, hd, hd), F32)],
        scratch_shapes=[pltpu.VMEM((nh, hd, hd), F32), pltpu.VMEM((nh, ell, hd), F32),
                        pltpu.VMEM((nh, ell, hd), F32), pltpu.VMEM((nh, ell, hd), F32)],
        compiler_params=_params("parallel", "arbitrary"),
        name="gdn_scan",
    )(q, k, v, proj, g_row, b_row, g_col, b_col, s0, o_norm_w.reshape(1, hd))


def _swa_sconv_layer(x, pos0, k_cache, v_cache, conv_state, norm_g, w_in, sinks, conv_w, w_out):
    n, t, d = x.shape
    ch = conv_w.shape[1]
    aw = w_out.shape[0] - ch
    kvw = (w_in.shape[1] - aw - 3 * ch) // 2
    hd = aw // sinks.shape[0]
    hn = _rmsnorm(x.reshape(n * t, d), norm_g, BF16)
    proj = _matmul(hn, w_in, name="in_proj_a").reshape(n, t, -1)
    cos_t, sin_t = _rope_tables(t, pos0, hd)
    k_new = _rope_k(proj, cos_t, sin_t, aw, kvw, hd)
    v_new = proj[:, :, aw + kvw:aw + 2 * kvw]
    if k_cache is None:
        k_all, v_all = k_new, v_new
        attn = _attention(proj, k_all, v_all, cos_t, sin_t, sinks, aw=aw, hd=hd, window=WINDOW)
    else:
        k_all = jnp.concatenate([k_cache, k_new], axis=1)
        v_all = jnp.concatenate([v_cache, v_new], axis=1)
        attn = _attention(proj, k_all, v_all, cos_t, sin_t, sinks, aw=aw, hd=hd, window=0)
    col_b = aw + 2 * kvw
    z, new_state = _sconv(proj, conv_state, conv_w, col_b, col_b + ch, col_b + 2 * ch)
    mix = jnp.concatenate([attn, z], axis=-1).reshape(n * t, aw + ch)
    x = _matmul(mix, w_out, res=x.reshape(n * t, d), name="out_proj_a").reshape(n, t, d)
    return x, k_all[:, -WINDOW:], v_all[:, -WINDOW:], new_state


def _gdn_layer(x, conv_state, s0, norm_g, w_in, conv_w, a_log, dt_bias, o_norm_w, w_out, ell):
    n, t, d = x.shape
    nh = a_log.shape[0]
    hd = o_norm_w.shape[0]
    hw = nh * hd
    hn = _rmsnorm(x.reshape(n * t, d), norm_g, BF16)
    proj = _matmul(hn, w_in[:, :4 * hw], name="in_proj_g").reshape(n, t, 4 * hw)
    gates = _gates(hn.reshape(n, t, d), w_in[:, 4 * hw:], a_log, dt_bias, ell)
    q = _gdn_prep(proj, conv_state, conv_w, 0, nh, hd, "q")
    k = _gdn_prep(proj, conv_state, conv_w, hw, nh, hd, "k")
    v = _gdn_prep(proj, conv_state, conv_w, 2 * hw, nh, hd, "v")
    o, s_new = _gdn_scan(q, k, v, proj, 3 * hw, gates, s0, o_norm_w, ell)
    x = _matmul(o.reshape(n * t, hw), w_out, res=x.reshape(n * t, d), name="out_proj_g").reshape(n, t, d)
    width = conv_w.shape[0]
    return x, proj[:, t - (width - 1):, :3 * hw], s_new


def _mlp(x, norm_g, w_up, w_down):
    n, t, d = x.shape
    hn = _rmsnorm(x.reshape(n * t, d), norm_g, BF16)
    hid = _matmul(hn, w_up, act="relu2", out_dtype=BF16, name="mlp_up")
    return _matmul(hid, w_down, res=x.reshape(n * t, d), name="mlp_down").reshape(n, t, d)


def kernel(x_prompt, x_sample, cache_swa_k, cache_swa_v, state_sconv, state_dn_conv, state_dn,
           attn_norm, w_in_a, sinks, sconv_w, w_out_a,
           dn_norm, w_in_g, dn_conv_w, A_log, dt_bias, o_norm_w, w_out_g,
           mlp_norm, w_up, w_down, final_norm):
    xp, xs = x_prompt, x_sample
    nb, nd = xp.shape[0], xs.shape[0]
    depth = mlp_norm.shape[0]
    n_kv, hd_a = cache_swa_k.shape[3], cache_swa_k.shape[4]
    nh_g, hd_g = state_dn.shape[2], state_dn.shape[3]
    outs = [[] for _ in range(10)]
    for li in range(depth):
        j = li // 2
        if li % 2 == 0:
            w_in, w_out = w_in_a[j].astype(BF16), w_out_a[j].astype(BF16)
            zero_state = jnp.zeros((nb,) + state_sconv.shape[2:], F32)
            xp, kp, vp, cp = _swa_sconv_layer(xp, 0, None, None, zero_state, attn_norm[j], w_in,
                                              sinks[j], sconv_w[j], w_out)
            kc = cache_swa_k[j].reshape(nd, -1, n_kv * hd_a)
            vc = cache_swa_v[j].reshape(nd, -1, n_kv * hd_a)
            xs, ks, vs, cs = _swa_sconv_layer(xs, PAST_LEN, kc, vc, state_sconv[j], attn_norm[j], w_in,
                                              sinks[j], sconv_w[j], w_out)
            shape5 = lambda a: a.reshape(a.shape[0], a.shape[1], n_kv, hd_a)
            for lst, val in zip(outs[:6], (shape5(kp), shape5(vp), shape5(ks), shape5(vs), cp, cs)):
                lst.append(val)
        else:
            w_in, w_out = w_in_g[j].astype(BF16), w_out_g[j].astype(BF16)
            zero_conv = jnp.zeros((nb,) + state_dn_conv.shape[2:], F32)
            zero_s = jnp.zeros((nb, nh_g, hd_g, hd_g), F32)
            xp, dcp, dsp = _gdn_layer(xp, zero_conv, zero_s, dn_norm[j], w_in, dn_conv_w[j], A_log[j],
                                      dt_bias[j], o_norm_w[j], w_out, CHUNK)
            xs, dcs, dss = _gdn_layer(xs, state_dn_conv[j], state_dn[j], dn_norm[j], w_in, dn_conv_w[j],
                                      A_log[j], dt_bias[j], o_norm_w[j], w_out, xs.shape[1])
            for lst, val in zip(outs[6:], (dcp, dcs, dsp, dss)):
                lst.append(val)
        wu, wd = w_up[li].astype(BF16), w_down[li].astype(BF16)
        xp = _mlp(xp, mlp_norm[li], wu, wd)
        xs = _mlp(xs, mlp_norm[li], wu, wd)
    y_prompt = _rmsnorm(xp.reshape(-1, xp.shape[-1]), final_norm, F32).reshape(xp.shape)
    y_sample = _rmsnorm(xs.reshape(-1, xs.shape[-1]), final_norm, F32).reshape(xs.shape)
    return (y_prompt, y_sample) + tuple(jnp.stack(o, 0) for o in outs)
```

```python
import functools

import jax
import jax.numpy as jnp
from jax import lax
from jax.experimental import pallas as pl
from jax.experimental.pallas import tpu as pltpu

EPS = 1e-6
CHUNK = 64
WINDOW = 128
PAST_LEN = 4096
ROPE_THETA = 500000.0
NEG_INF = -1e30
LANES = 128
SUBLANES = 8
VMEM_LIMIT = 56 * 1024 * 1024
BF16 = jnp.bfloat16
F32 = jnp.float32


def _params(*sem):
    return pltpu.CompilerParams(dimension_semantics=sem, vmem_limit_bytes=VMEM_LIMIT)


def _tile(n, pref, align=SUBLANES):
    if n <= pref:
        return n
    t = pref - pref % align
    while t >= align:
        if n % t == 0:
            return t
        t -= align
    return n


def _rmsnorm_kernel(x_ref, g_ref, o_ref):
    x = x_ref[...]
    ms = jnp.mean(x * x, axis=-1, keepdims=True)
    o_ref[...] = (x * lax.rsqrt(ms + EPS) * g_ref[...]).astype(o_ref.dtype)


def _rmsnorm(x, g, out_dtype):
    m, d = x.shape
    tm = _tile(m, 256)
    return pl.pallas_call(
        _rmsnorm_kernel,
        grid=(m // tm,),
        in_specs=[pl.BlockSpec((tm, d), lambda i: (i, 0)),
                  pl.BlockSpec((1, d), lambda i: (0, 0))],
        out_specs=pl.BlockSpec((tm, d), lambda i: (i, 0)),
        out_shape=jax.ShapeDtypeStruct((m, d), out_dtype),
        compiler_params=_params("parallel"),
        name="rmsnorm",
    )(x, g.reshape(1, d))


def _mm_kernel(*refs, nk, act, has_res):
    a_ref, w_ref = refs[0], refs[1]
    r_ref = refs[2] if has_res else None
    o_ref = refs[2 + has_res]

    def finish(acc):
        if act == "relu2":
            acc = jnp.square(jnp.maximum(acc, 0.0))
        if has_res:
            acc = r_ref[...] + acc
        o_ref[...] = acc.astype(o_ref.dtype)

    part = jnp.dot(a_ref[...], w_ref[...], preferred_element_type=F32)
    if nk == 1:
        finish(part)
    else:
        acc_ref = refs[3 + has_res]
        k = pl.program_id(2)

        @pl.when(k == 0)
        def _():
            acc_ref[...] = part

        @pl.when(k > 0)
        def _():
            acc_ref[...] += part

        @pl.when(k == nk - 1)
        def _():
            finish(acc_ref[...])


def _matmul(a, w, *, res=None, act=None, out_dtype=F32, name="matmul"):
    m, kdim = a.shape
    n = w.shape[1]
    tm = _tile(m, 1024)
    tn = n if n < LANES else _tile(n, 1024, LANES)
    tk = _tile(kdim, 4096 if kdim <= 4096 else 2048, LANES)
    nk = kdim // tk
    in_specs = [pl.BlockSpec((tm, tk), lambda i, j, k: (i, k)),
                pl.BlockSpec((tk, tn), lambda i, j, k: (k, j))]
    args = [a, w]
    if res is not None:
        in_specs.append(pl.BlockSpec((tm, tn), lambda i, j, k: (i, j)))
        args.append(res)
    return pl.pallas_call(
        functools.partial(_mm_kernel, nk=nk, act=act, has_res=res is not None),
        grid=(m // tm, n // tn, nk),
        in_specs=in_specs,
        out_specs=pl.BlockSpec((tm, tn), lambda i, j, k: (i, j)),
        out_shape=jax.ShapeDtypeStruct((m, n), out_dtype),
        scratch_shapes=[pltpu.VMEM((tm, tn), F32)] if nk > 1 else [],
        compiler_params=_params("parallel", "parallel", "arbitrary"),
        name=name,
    )(*args)


def _rope_table_kernel(inv_ref, c_ref, s_ref, *, pos0, hd):
    shape = c_ref.shape
    pos = (pos0 + lax.broadcasted_iota(jnp.int32, shape, 0)).astype(F32)
    ang = pos * inv_ref[...]
    d = jnp.bitwise_and(lax.broadcasted_iota(jnp.int32, shape, 1), hd - 1)
    rot = hd // 4
    cos, sin = jnp.cos(ang), jnp.sin(ang)
    c_ref[...] = jnp.where(d < rot, cos, 1.0)
    s_ref[...] = jnp.where(d < rot // 2, -sin, jnp.where(d < rot, sin, 0.0))


def _rope_tables(n_pos, pos0, hd):
    half = hd // 8
    inv = ROPE_THETA ** (-jnp.arange(half, dtype=F32) / half)
    inv_lane = jnp.tile(jnp.concatenate([inv, inv, jnp.zeros((hd - 2 * half,), F32)]), LANES // hd)
    shp = jax.ShapeDtypeStruct((n_pos, LANES), F32)
    return pl.pallas_call(
        functools.partial(_rope_table_kernel, pos0=pos0, hd=hd),
        out_shape=(shp, shp),
        name="rope_tables",
    )(inv_lane.reshape(1, LANES))


def _rope(x, c, s, hd):
    w = x.shape[1]
    reps = w // LANES
    cf = jnp.concatenate([c] * reps, axis=1) if reps > 1 else c
    sf = jnp.concatenate([s] * reps, axis=1) if reps > 1 else s
    d = jnp.bitwise_and(lax.broadcasted_iota(jnp.int32, x.shape, 1), hd - 1)
    half = hd // 8
    fwd = pltpu.roll(x, w - half, 1)
    bwd = pltpu.roll(x, half, 1)
    sw = jnp.where(d < half, fwd, jnp.where(d < 2 * half, bwd, 0.0))
    return x * cf + sw * sf


def _rope_k_kernel(k_ref, c_ref, s_ref, o_ref, *, hd):
    o_ref[0] = _rope(k_ref[0], c_ref[...], s_ref[...], hd)


def _rope_k(proj, cos_t, sin_t, col0, width, hd):
    n, t, _ = proj.shape
    tb = _tile(t, 512)
    return pl.pallas_call(
        functools.partial(_rope_k_kernel, hd=hd),
        grid=(n, t // tb),
        in_specs=[pl.BlockSpec((1, tb, width), lambda i, j: (i, j, col0 // width)),
                  pl.BlockSpec((tb, LANES), lambda i, j: (j, 0)),
                  pl.BlockSpec((tb, LANES), lambda i, j: (j, 0))],
        out_specs=pl.BlockSpec((1, tb, width), lambda i, j: (i, j, 0)),
        out_shape=jax.ShapeDtypeStruct((n, t, width), F32),
        compiler_params=_params("parallel", "parallel"),
        name="rope_k",
    )(proj, cos_t, sin_t)


def _attn_kernel(sink_ref, q_ref, k_ref, v_ref, c_ref, s_ref, o_ref, *,
                 rows, band, window, n_kv, group, hd):
    tb = q_ref.shape[1]
    gw = group * hd
    for ci in range(tb // rows):
        r0 = ci * rows
        if window:
            c_abs = pl.program_id(1) * (tb // rows) + ci
            lo = c_abs * rows - window
            start = pl.multiple_of(jnp.maximum(lo, 0), rows)
            kpos = start + lax.broadcasted_iota(jnp.int32, (1, band), 1)
            valid = jnp.logical_and(kpos >= lo, kpos < lo + band)
            ksl = pl.ds(start, band)
        else:
            ksl = slice(0, band)
        c = c_ref[r0:r0 + rows, :]
        s = s_ref[r0:r0 + rows, :]
        for h in range(n_kv):
            qh = _rope(q_ref[0, r0:r0 + rows, h * gw:(h + 1) * gw], c, s, hd)
            qs = jnp.concatenate([qh[:, g * hd:(g + 1) * hd] for g in range(group)], axis=0)
            kh = k_ref[0, ksl, h * hd:(h + 1) * hd].astype(BF16)
            vh = v_ref[0, ksl, h * hd:(h + 1) * hd].astype(BF16)
            sc = lax.dot_general(qs.astype(BF16), kh, (((1,), (1,)), ((), ())),
                                 preferred_element_type=F32) * (hd ** -0.5)
            if window:
                sc = jnp.where(valid, sc, NEG_INF)
            sink = jnp.concatenate(
                [jnp.full((rows, 1), sink_ref[h * group + g], F32) for g in range(group)], axis=0)
            m = jnp.maximum(jnp.max(sc, axis=-1, keepdims=True), sink)
            p = jnp.exp(sc - m)
            denom = jnp.sum(p, axis=-1, keepdims=True) + jnp.exp(sink - m)
            o = jnp.dot(p.astype(BF16), vh, preferred_element_type=F32) / denom
            o_ref[0, r0:r0 + rows, h * gw:(h + 1) * gw] = jnp.concatenate(
                [o[g * rows:(g + 1) * rows, :] for g in range(group)], axis=1).astype(o_ref.dtype)


def _attention(proj, k_all, v_all, cos_t, sin_t, sinks, *, aw, hd, window):
    n, t, _ = proj.shape
    tk, kvw = k_all.shape[1], k_all.shape[2]
    n_kv = kvw // hd
    group = aw // kvw
    if window:
        rows, band = CHUNK, window + CHUNK
        tb = _tile(t, 4 * CHUNK, CHUNK)
    else:
        rows, band, tb = t, tk, t
    return pl.pallas_call(
        functools.partial(_attn_kernel, rows=rows, band=band, window=window,
                          n_kv=n_kv, group=group, hd=hd),
        grid=(n, t // tb),
        in_specs=[pl.BlockSpec(memory_space=pltpu.SMEM),
                  pl.BlockSpec((1, tb, aw), lambda i, j: (i, j, 0)),
                  pl.BlockSpec((1, tk, kvw), lambda i, j: (i, 0, 0)),
                  pl.BlockSpec((1, tk, kvw), lambda i, j: (i, 0, 0)),
                  pl.BlockSpec((tb, LANES), lambda i, j: (j, 0)),
                  pl.BlockSpec((tb, LANES), lambda i, j: (j, 0))],
        out_specs=pl.BlockSpec((1, tb, aw), lambda i, j: (i, j, 0)),
        out_shape=jax.ShapeDtypeStruct((n, t, aw), BF16),
        compiler_params=_params("parallel", "parallel"),
        name="swa_attention",
    )(sinks, proj, k_all, v_all, cos_t, sin_t)


def _sconv_kernel(b_ref, c_ref, x_ref, st_ref, w_ref, z_ref, ns_ref, buf):
    t = x_ref.shape[1]
    width = w_ref.shape[0]
    buf[SUBLANES:SUBLANES + t, :] = c_ref[0] * x_ref[0]
    buf[SUBLANES - (width - 1):SUBLANES, :] = st_ref[0]
    lo = SUBLANES - (width - 1)
    acc = buf[lo:lo + t, :] * w_ref[0:1, :]
    for i in range(1, width):
        acc = acc + buf[lo + i:lo + i + t, :] * w_ref[i:i + 1, :]
    z_ref[0] = (b_ref[0] * acc).astype(z_ref.dtype)
    ns_ref[0] = buf[SUBLANES + t - (width - 1):SUBLANES + t, :]


def _sconv(proj, state, w, col_b, col_c, col_x):
    n, t, _ = proj.shape
    width, ch = w.shape
    cb = next(c for c in (256, LANES) if all(v % c == 0 for v in (ch, col_b, col_c, col_x)))
    blk = lambda off: pl.BlockSpec((1, t, cb), lambda i, j: (i, 0, off // cb + j))
    return pl.pallas_call(
        _sconv_kernel,
        grid=(n, ch // cb),
        in_specs=[blk(col_b), blk(col_c), blk(col_x),
                  pl.BlockSpec((1, width - 1, cb), lambda i, j: (i, 0, j)),
                  pl.BlockSpec((width, cb), lambda i, j: (0, j))],
        out_specs=[pl.BlockSpec((1, t, cb), lambda i, j: (i, 0, j)),
                   pl.BlockSpec((1, width - 1, cb), lambda i, j: (i, 0, j))],
        out_shape=[jax.ShapeDtypeStruct((n, t, ch), BF16),
                   jax.ShapeDtypeStruct((n, width - 1, ch), F32)],
        scratch_shapes=[pltpu.VMEM((t + SUBLANES, cb), F32)],
        compiler_params=_params("parallel", "parallel"),
        name="sconv",
    )(proj, proj, proj, state, w)


def _gdn_prep_kernel(x_ref, st_ref, w_ref, o_ref, buf, *, mode, hd):
    t = x_ref.shape[1]
    width = w_ref.shape[0]
    buf[SUBLANES:SUBLANES + t, :] = x_ref[0]
    buf[SUBLANES - (width - 1):SUBLANES, :] = st_ref[0]
    lo = SUBLANES - (width - 1)
    acc = buf[lo:lo + t, :] * w_ref[0:1, :]
    for i in range(1, width):
        acc = acc + buf[lo + i:lo + i + t, :] * w_ref[i:i + 1, :]
    y = acc * (1.0 / (1.0 + jnp.exp(-acc)))
    for h in range(o_ref.shape[1]):
        yh = y[:, h * hd:(h + 1) * hd]
        if mode != "v":
            yh = yh * lax.rsqrt(jnp.sum(yh * yh, axis=-1, keepdims=True) + EPS)
        if mode == "q":
            yh = yh * (hd ** -0.5)
        o_ref[0, h] = yh


def _gdn_prep(proj, state, w, col0, n_heads, hd, mode):
    n, t, _ = proj.shape
    width = w.shape[0]
    hb = 2
    cb = hb * hd
    return pl.pallas_call(
        functools.partial(_gdn_prep_kernel, mode=mode, hd=hd),
        grid=(n, n_heads // hb),
        in_specs=[pl.BlockSpec((1, t, cb), lambda i, j: (i, 0, col0 // cb + j)),
                  pl.BlockSpec((1, width - 1, cb), lambda i, j: (i, 0, col0 // cb + j)),
                  pl.BlockSpec((width, cb), lambda i, j: (0, col0 // cb + j))],
        out_specs=pl.BlockSpec((1, hb, t, hd), lambda i, j: (i, j, 0, 0)),
        out_shape=jax.ShapeDtypeStruct((n, n_heads, t, hd), F32),
        scratch_shapes=[pltpu.VMEM((t + SUBLANES, cb), F32)],
        compiler_params=_params("parallel", "parallel"),
        name="gdn_prep_" + mode,
    )(proj, state, w)


def _softplus(x):
    return jnp.maximum(x, 0.0) + jnp.log1p(jnp.exp(-jnp.abs(x)))


def _gates_kernel(h_ref, w_ref, wt_ref, al_ref, alt_ref, dt_ref, dtt_ref,
                  gr_ref, br_ref, gc_ref, bc_ref):
    hx = h_ref[0]
    nh = al_ref.shape[1]
    ell = hx.shape[0]
    ab = jnp.dot(hx, w_ref[...], preferred_element_type=F32)
    abt = lax.dot_general(wt_ref[...], hx, (((1,), (1,)), ((), ())),
                          preferred_element_type=F32)
    g_col = -jnp.exp(al_ref[...]) * _softplus(ab[:, :nh] + dt_ref[...])
    g_row = -jnp.exp(alt_ref[...]) * _softplus(abt[:nh, :] + dtt_ref[...])
    i = lax.broadcasted_iota(jnp.int32, (ell, ell), 0)
    j = lax.broadcasted_iota(jnp.int32, (ell, ell), 1)
    lower = (i >= j).astype(F32)
    upper = (i <= j).astype(F32)
    gc_ref[0, 0] = jnp.dot(lower, g_col, preferred_element_type=F32, precision=lax.Precision.HIGHEST)
    gr_ref[0, 0] = jnp.dot(g_row, upper, preferred_element_type=F32, precision=lax.Precision.HIGHEST)
    bc_ref[0, 0] = 1.0 / (1.0 + jnp.exp(-ab[:, nh:]))
    br_ref[0, 0] = 1.0 / (1.0 + jnp.exp(-abt[nh:, :]))


def _gates(hn, w_ab, a_log, dt_bias, ell):
    n, t, d = hn.shape
    nh = a_log.shape[0]
    nc = t // ell
    row = jax.ShapeDtypeStruct((n, nc, nh, ell), F32)
    col = jax.ShapeDtypeStruct((n, nc, ell, nh), F32)
    full = lambda shape: pl.BlockSpec(shape, lambda i, j: (0,) * len(shape))
    return pl.pallas_call(
        _gates_kernel,
        grid=(n, nc),
        in_specs=[pl.BlockSpec((1, ell, d), lambda i, j: (i, j, 0)),
                  full((d, 2 * nh)), full((2 * nh, d)),
                  full((1, nh)), full((nh, 1)), full((1, nh)), full((nh, 1))],
        out_specs=[pl.BlockSpec((1, 1, nh, ell), lambda i, j: (i, j, 0, 0)),
                   pl.BlockSpec((1, 1, nh, ell), lambda i, j: (i, j, 0, 0)),
                   pl.BlockSpec((1, 1, ell, nh), lambda i, j: (i, j, 0, 0)),
                   pl.BlockSpec((1, 1, ell, nh), lambda i, j: (i, j, 0, 0))],
        out_shape=[row, row, col, col],
        compiler_params=_params("parallel", "parallel"),
        name="gdn_gates",
    )(hn, w_ab, w_ab.T, a_log.reshape(1, nh), a_log.reshape(nh, 1),
      dt_bias.reshape(1, nh), dt_bias.reshape(nh, 1))


def _bdot(a, b):
    return jnp.dot(a.astype(BF16), b.astype(BF16), preferred_element_type=F32)


def _tdot(a, b):
    return jnp.dot(a.astype(BF16), b.astype(BF16), preferred_element_type=F32)


def _scan_kernel(q_ref, k_ref, v_ref, gate_ref, gr_ref, br_ref, gc_ref, bc_ref, s0_ref, ow_ref,
                 o_ref, sout_ref, s_scr, cg_scr, cb_scr, ub_scr, w_scr, qd_scr, kd_scr, aqk_scr, o_scr,
                 *, heads_per_body):
    nh, ell, hd = q_ref.shape[1], q_ref.shape[2], q_ref.shape[3]
    cidx = pl.program_id(1)

    @pl.when(cidx == 0)
    def _():
        s_scr[...] = s0_ref[0]

    gcol = gc_ref[0, 0]
    bcol = bc_ref[0, 0]
    for h in range(nh):
        cg_scr[h] = jnp.broadcast_to(gcol[:, h:h + 1], (ell, hd))
        cb_scr[h] = jnp.broadcast_to(bcol[:, h:h + 1], (ell, hd))

    i = lax.broadcasted_iota(jnp.int32, (ell, ell), 0)
    j = lax.broadcasted_iota(jnp.int32, (ell, ell), 1)
    eye = (i == j).astype(F32)
    shifts = range(3, ell.bit_length())
    blk = [jnp.right_shift(i, sh) == jnp.right_shift(j, sh) for sh in shifts]

    nt = (((1,), (1,)), ((), ()))

    tn = (((0,), (0,)), ((), ()))
    each = lambda f, *cols: [f(*x) for x in zip(*cols)]
    dot_nt = lambda a, b: lax.dot_general(a, b, nt, preferred_element_type=F32)
    dot_nn = lambda a, b: jnp.dot(a, b, preferred_element_type=F32)
    to16 = lambda a: a.astype(BF16)

    def wy_body(it, carry):
        hs = [it * heads_per_body + r for r in range(heads_per_body)]
        q = [q_ref[0, h] for h in hs]
        k = [k_ref[0, h] for h in hs]
        v = [v_ref[0, h] for h in hs]
        g_row = [gr_ref[0, 0, pl.ds(h, 1), :] for h in hs]
        b_row = [br_ref[0, 0, pl.ds(h, 1), :] for h in hs]
        g_col = [cg_scr[h] for h in hs]
        b_col = [cb_scr[h] for h in hs]
        k16 = each(to16, k)
        kk = each(dot_nt, k16, k16)
        qk = each(dot_nt, each(to16, q), k16)
        decay = each(lambda gc, gr: jnp.exp(jnp.where(i >= j, gc[:, :ell] - gr, -jnp.inf)), g_col, g_row)
        a = each(lambda bc, x, d: jnp.where(i > j, bc[:, :ell] * x * d, 0.0), b_col, kk, decay)
        p = each(lambda x: -jnp.where(blk[0], x, 0.0), a)
        tm = each(lambda x: eye + x, p)
        for _ in range(2):
            p = each(_tdot, p, p)
            tm = each(lambda t, x: t + _tdot(t, x), tm, p)
        for lvl in range(1, len(blk)):
            ring = jnp.logical_and(blk[lvl], jnp.logical_not(blk[lvl - 1]))
            tl = each(lambda t, x: _tdot(t, jnp.where(ring, x, 0.0)), tm, a)
            tm = each(lambda t, x: t - _tdot(x, t), tm, tl)
        u_base = each(lambda t, b, x: _bdot(t * b, x), tm, b_row, v)
        w = each(lambda t, b, g, x: _bdot(t * (b * jnp.exp(g)), x), tm, b_row, g_row, k16)
        q_dec = each(lambda x, g: x * jnp.exp(g), q, g_col)
        k_dec = each(lambda x, g: x * jnp.exp(g[ell - 1:ell, :] - g), k, g_col)
        aqk = each(lambda x, d: x * d, qk, decay)
        for r, h in enumerate(hs):
            ub_scr[h] = u_base[r]
            w_scr[h] = to16(w[r])
            qd_scr[h] = to16(q_dec[r])
            kd_scr[h] = to16(k_dec[r])
            aqk_scr[h] = to16(aqk[r])
        return carry

    lax.fori_loop(0, nh // heads_per_body, wy_body, 0)

    def rec_body(it, carry):
        hs = [it * heads_per_body + r for r in range(heads_per_body)]
        s = [s_scr[h] for h in hs]
        s16 = each(to16, s)
        u = each(lambda ub, w, x: ub - dot_nn(w, x), [ub_scr[h] for h in hs], [w_scr[h] for h in hs], s16)
        u16 = each(to16, u)
        o_s = each(dot_nn, [qd_scr[h] for h in hs], s16)
        o_u = each(dot_nn, [aqk_scr[h] for h in hs], u16)
        ds = each(lambda kd, x: lax.dot_general(kd, x, tn, preferred_element_type=F32),
                  [kd_scr[h] for h in hs], u16)
        for r, h in enumerate(hs):
            o_scr[h] = o_s[r] + o_u[r]
            s_scr[h] = s[r] * jnp.exp(cg_scr[h, ell - 1:ell, :]) + ds[r]
        return carry

    lax.fori_loop(0, nh // heads_per_body, rec_body, 0)

    for h in range(nh):
        o = o_scr[h]
        o = o * lax.rsqrt(jnp.mean(o * o, axis=-1, keepdims=True) + EPS) * ow_ref[...]
        gt = gate_ref[0, :, h * hd:(h + 1) * hd]
        o_ref[0, :, h * hd:(h + 1) * hd] = (o * (gt * (1.0 / (1.0 + jnp.exp(-gt))))).astype(o_ref.dtype)

    @pl.when(cidx == pl.num_programs(1) - 1)
    def _():
        sout_ref[0] = s_scr[...]


def _gdn_scan(q, k, v, proj, gate_col, gates, s0, o_norm_w, ell):
    n, nh, t, hd = q.shape
    nc = t // ell
    hw = nh * hd
    g_row, b_row, g_col, b_col = gates
    qkv_spec = pl.BlockSpec((1, nh, ell, hd), lambda i, c: (i, 0, c, 0))
    row_spec = pl.BlockSpec((1, 1, nh, ell), lambda i, c: (i, c, 0, 0))
    col_spec = pl.BlockSpec((1, 1, ell, nh), lambda i, c: (i, c, 0, 0))
    st_spec = pl.BlockSpec((1, nh, hd, hd), lambda i, c: (i, 0, 0, 0))
    per_head = lambda width, dt: pltpu.VMEM((nh, ell, width), dt)
    return pl.pallas_call(
        functools.partial(_scan_kernel, heads_per_body=_tile(nh, 32, 1)),
        grid=(n, nc),
        in_specs=[qkv_spec, qkv_spec, qkv_spec,
                  pl.BlockSpec((1, ell, hw), lambda i, c: (i, c, gate_col // hw)),
                  row_spec, row_spec, col_spec, col_spec, st_spec,
                  pl.BlockSpec((1, hd), lambda i, c: (0, 0))],
        out_specs=[pl.BlockSpec((1, ell, hw), lambda i, c: (i, c, 0)), st_spec],
        out_shape=[jax.ShapeDtypeStruct((n, t, hw), BF16),
                   jax.ShapeDtypeStruct((n, nh, hd, hd), F32)],
        scratch_shapes=[pltpu.VMEM((nh, hd, hd), F32), per_head(hd, F32), per_head(hd, F32),
                        per_head(hd, F32), per_head(hd, BF16), per_head(hd, BF16), per_head(hd, BF16),
                        per_head(ell, BF16), per_head(hd, F32)],
        compiler_params=_params("parallel", "arbitrary"),
        name="gdn_scan",
    )(q, k, v, proj, g_row, b_row, g_col, b_col, s0, o_norm_w.reshape(1, hd))


def _swa_sconv_layer(x, pos0, k_cache, v_cache, conv_state, norm_g, w_in, sinks, conv_w, w_out):
    n, t, d = x.shape
    ch = conv_w.shape[1]
    aw = w_out.shape[0] - ch
    kvw = (w_in.shape[1] - aw - 3 * ch) // 2
    hd = aw // sinks.shape[0]
    hn = _rmsnorm(x.reshape(n * t, d), norm_g, BF16)
    proj = _matmul(hn, w_in, name="in_proj_a").reshape(n, t, -1)
    cos_t, sin_t = _rope_tables(t, pos0, hd)
    k_new = _rope_k(proj, cos_t, sin_t, aw, kvw, hd)
    v_new = proj[:, :, aw + kvw:aw + 2 * kvw]
    if k_cache is None:
        k_all, v_all = k_new, v_new
        attn = _attention(proj, k_all, v_all, cos_t, sin_t, sinks, aw=aw, hd=hd, window=WINDOW)
    else:
        k_all = jnp.concatenate([k_cache, k_new], axis=1)
        v_all = jnp.concatenate([v_cache, v_new], axis=1)
        attn = _attention(proj, k_all, v_all, cos_t, sin_t, sinks, aw=aw, hd=hd, window=0)
    col_b = aw + 2 * kvw
    z, new_state = _sconv(proj, conv_state, conv_w, col_b, col_b + ch, col_b + 2 * ch)
    mix = jnp.concatenate([attn, z], axis=-1).reshape(n * t, aw + ch)
    x = _matmul(mix, w_out, res=x.reshape(n * t, d), name="out_proj_a").reshape(n, t, d)
    return x, k_all[:, -WINDOW:], v_all[:, -WINDOW:], new_state


def _gdn_layer(x, conv_state, s0, norm_g, w_in, conv_w, a_log, dt_bias, o_norm_w, w_out, ell):
    n, t, d = x.shape
    nh = a_log.shape[0]
    hd = o_norm_w.shape[0]
    hw = nh * hd
    hn = _rmsnorm(x.reshape(n * t, d), norm_g, BF16)
    proj = _matmul(hn, w_in[:, :4 * hw], name="in_proj_g").reshape(n, t, 4 * hw)
    gates = _gates(hn.reshape(n, t, d), w_in[:, 4 * hw:], a_log, dt_bias, ell)
    q = _gdn_prep(proj, conv_state, conv_w, 0, nh, hd, "q")
    k = _gdn_prep(proj, conv_state, conv_w, hw, nh, hd, "k")
    v = _gdn_prep(proj, conv_state, conv_w, 2 * hw, nh, hd, "v")
    o, s_new = _gdn_scan(q, k, v, proj, 3 * hw, gates, s0, o_norm_w, ell)
    x = _matmul(o.reshape(n * t, hw), w_out, res=x.reshape(n * t, d), name="out_proj_g").reshape(n, t, d)
    width = conv_w.shape[0]
    return x, proj[:, t - (width - 1):, :3 * hw], s_new


def _mlp(x, norm_g, w_up, w_down):
    n, t, d = x.shape
    hn = _rmsnorm(x.reshape(n * t, d), norm_g, BF16)
    hid = _matmul(hn, w_up, act="relu2", out_dtype=BF16, name="mlp_up")
    return _matmul(hid, w_down, res=x.reshape(n * t, d), name="mlp_down").reshape(n, t, d)


def kernel(x_prompt, x_sample, cache_swa_k, cache_swa_v, state_sconv, state_dn_conv, state_dn,
           attn_norm, w_in_a, sinks, sconv_w, w_out_a,
           dn_norm, w_in_g, dn_conv_w, A_log, dt_bias, o_norm_w, w_out_g,
           mlp_norm, w_up, w_down, final_norm):
    xp, xs = x_prompt, x_sample
    nb, nd = xp.shape[0], xs.shape[0]
    depth = mlp_norm.shape[0]
    n_kv, hd_a = cache_swa_k.shape[3], cache_swa_k.shape[4]
    nh_g, hd_g = state_dn.shape[2], state_dn.shape[3]
    outs = [[] for _ in range(10)]
    for li in range(depth):
        j = li // 2
        if li % 2 == 0:
            w_in, w_out = w_in_a[j].astype(BF16), w_out_a[j].astype(BF16)
            zero_state = jnp.zeros((nb,) + state_sconv.shape[2:], F32)
            xp, kp, vp, cp = _swa_sconv_layer(xp, 0, None, None, zero_state, attn_norm[j], w_in,
                                              sinks[j], sconv_w[j], w_out)
            kc = cache_swa_k[j].reshape(nd, -1, n_kv * hd_a)
            vc = cache_swa_v[j].reshape(nd, -1, n_kv * hd_a)
            xs, ks, vs, cs = _swa_sconv_layer(xs, PAST_LEN, kc, vc, state_sconv[j], attn_norm[j], w_in,
                                              sinks[j], sconv_w[j], w_out)
            shape5 = lambda a: a.reshape(a.shape[0], a.shape[1], n_kv, hd_a)
            for lst, val in zip(outs[:6], (shape5(kp), shape5(vp), shape5(ks), shape5(vs), cp, cs)):
                lst.append(val)
        else:
            w_in, w_out = w_in_g[j].astype(BF16), w_out_g[j].astype(BF16)
            zero_conv = jnp.zeros((nb,) + state_dn_conv.shape[2:], F32)
            zero_s = jnp.zeros((nb, nh_g, hd_g, hd_g), F32)
            xp, dcp, dsp = _gdn_layer(xp, zero_conv, zero_s, dn_norm[j], w_in, dn_conv_w[j], A_log[j],
                                      dt_bias[j], o_norm_w[j], w_out, CHUNK)
            xs, dcs, dss = _gdn_layer(xs, state_dn_conv[j], state_dn[j], dn_norm[j], w_in, dn_conv_w[j],
                                      A_log[j], dt_bias[j], o_norm_w[j], w_out, xs.shape[1])
            for lst, val in zip(outs[6:], (dcp, dcs, dsp, dss)):
                lst.append(val)
        wu, wd = w_up[li].astype(BF16), w_down[li].astype(BF16)
        xp = _mlp(xp, mlp_norm[li], wu, wd)
        xs = _mlp(xs, mlp_norm[li], wu, wd)
    y_prompt = _rmsnorm(xp.reshape(-1, xp.shape[-1]), final_norm, F32).reshape(xp.shape)
    y_sample = _rmsnorm(xs.reshape(-1, xs.shape[-1]), final_norm, F32).reshape(xs.shape)
    return (y_prompt, y_sample) + tuple(jnp.stack(o, 0) for o in outs)
```

```python
import functools

import jax
import jax.numpy as jnp
from jax import lax
from jax.experimental import pallas as pl
from jax.experimental.pallas import tpu as pltpu

EPS = 1e-6
CHUNK = 64
WINDOW = 128
PAST_LEN = 4096
ROPE_THETA = 500000.0
NEG_INF = -1e30
LANES = 128
SUBLANES = 8
VMEM_LIMIT = 56 * 1024 * 1024
BF16 = jnp.bfloat16
F32 = jnp.float32


def _params(*sem):
    return pltpu.CompilerParams(dimension_semantics=sem, vmem_limit_bytes=VMEM_LIMIT)


def _tile(n, pref, align=SUBLANES):
    if n <= pref:
        return n
    t = pref - pref % align
    while t >= align:
        if n % t == 0:
            return t
        t -= align
    return n


def _rmsnorm_kernel(x_ref, g_ref, o_ref):
    x = x_ref[...]
    ms = jnp.mean(x * x, axis=-1, keepdims=True)
    o_ref[...] = (x * lax.rsqrt(ms + EPS) * g_ref[...]).astype(o_ref.dtype)


def _rmsnorm(x, g, out_dtype):
    m, d = x.shape
    tm = _tile(m, 256)
    return pl.pallas_call(
        _rmsnorm_kernel,
        grid=(m // tm,),
        in_specs=[pl.BlockSpec((tm, d), lambda i: (i, 0)),
                  pl.BlockSpec((1, d), lambda i: (0, 0))],
        out_specs=pl.BlockSpec((tm, d), lambda i: (i, 0)),
        out_shape=jax.ShapeDtypeStruct((m, d), out_dtype),
        compiler_params=_params("parallel"),
        name="rmsnorm",
    )(x, g.reshape(1, d))


def _mm_kernel(*refs, nk, act, has_res):
    a_ref, w_ref = refs[0], refs[1]
    r_ref = refs[2] if has_res else None
    o_ref = refs[2 + has_res]

    def finish(acc):
        if act == "relu2":
            acc = jnp.square(jnp.maximum(acc, 0.0))
        if has_res:
            acc = r_ref[...] + acc
        o_ref[...] = acc.astype(o_ref.dtype)

    part = jnp.dot(a_ref[...], w_ref[...], preferred_element_type=F32)
    if nk == 1:
        finish(part)
    else:
        acc_ref = refs[3 + has_res]
        k = pl.program_id(2)

        @pl.when(k == 0)
        def _():
            acc_ref[...] = part

        @pl.when(k > 0)
        def _():
            acc_ref[...] += part

        @pl.when(k == nk - 1)
        def _():
            finish(acc_ref[...])


def _matmul(a, w, *, res=None, act=None, out_dtype=F32, name="matmul"):
    m, kdim = a.shape
    n = w.shape[1]
    tm = _tile(m, 1024)
    tn = n if n < LANES else _tile(n, 1024, LANES)
    tk = _tile(kdim, 4096 if kdim <= 4096 else 2048, LANES)
    nk = kdim // tk
    in_specs = [pl.BlockSpec((tm, tk), lambda i, j, k: (i, k)),
                pl.BlockSpec((tk, tn), lambda i, j, k: (k, j))]
    args = [a, w]
    if res is not None:
        in_specs.append(pl.BlockSpec((tm, tn), lambda i, j, k: (i, j)))
        args.append(res)
    return pl.pallas_call(
        functools.partial(_mm_kernel, nk=nk, act=act, has_res=res is not None),
        grid=(m // tm, n // tn, nk),
        in_specs=in_specs,
        out_specs=pl.BlockSpec((tm, tn), lambda i, j, k: (i, j)),
        out_shape=jax.ShapeDtypeStruct((m, n), out_dtype),
        scratch_shapes=[pltpu.VMEM((tm, tn), F32)] if nk > 1 else [],
        compiler_params=_params("parallel", "parallel", "arbitrary"),
        name=name,
    )(*args)


def _rope_table_kernel(inv_ref, c_ref, s_ref, *, pos0, hd):
    shape = c_ref.shape
    pos = (pos0 + lax.broadcasted_iota(jnp.int32, shape, 0)).astype(F32)
    ang = pos * inv_ref[...]
    d = jnp.bitwise_and(lax.broadcasted_iota(jnp.int32, shape, 1), hd - 1)
    rot = hd // 4
    cos, sin = jnp.cos(ang), jnp.sin(ang)
    c_ref[...] = jnp.where(d < rot, cos, 1.0)
    s_ref[...] = jnp.where(d < rot // 2, -sin, jnp.where(d < rot, sin, 0.0))


def _rope_tables(n_pos, pos0, hd):
    half = hd // 8
    inv = ROPE_THETA ** (-jnp.arange(half, dtype=F32) / half)
    inv_lane = jnp.tile(jnp.concatenate([inv, inv, jnp.zeros((hd - 2 * half,), F32)]), LANES // hd)
    shp = jax.ShapeDtypeStruct((n_pos, LANES), F32)
    return pl.pallas_call(
        functools.partial(_rope_table_kernel, pos0=pos0, hd=hd),
        out_shape=(shp, shp),
        name="rope_tables",
    )(inv_lane.reshape(1, LANES))


def _rope(x, c, s, hd):
    w = x.shape[1]
    reps = w // LANES
    cf = jnp.concatenate([c] * reps, axis=1) if reps > 1 else c
    sf = jnp.concatenate([s] * reps, axis=1) if reps > 1 else s
    d = jnp.bitwise_and(lax.broadcasted_iota(jnp.int32, x.shape, 1), hd - 1)
    half = hd // 8
    fwd = pltpu.roll(x, w - half, 1)
    bwd = pltpu.roll(x, half, 1)
    sw = jnp.where(d < half, fwd, jnp.where(d < 2 * half, bwd, 0.0))
    return x * cf + sw * sf


def _rope_k_kernel(k_ref, c_ref, s_ref, o_ref, *, hd):
    o_ref[0] = _rope(k_ref[0], c_ref[...], s_ref[...], hd)


def _rope_k(proj, cos_t, sin_t, col0, width, hd):
    n, t, _ = proj.shape
    tb = _tile(t, 512)
    return pl.pallas_call(
        functools.partial(_rope_k_kernel, hd=hd),
        grid=(n, t // tb),
        in_specs=[pl.BlockSpec((1, tb, width), lambda i, j: (i, j, col0 // width)),
                  pl.BlockSpec((tb, LANES), lambda i, j: (j, 0)),
                  pl.BlockSpec((tb, LANES), lambda i, j: (j, 0))],
        out_specs=pl.BlockSpec((1, tb, width), lambda i, j: (i, j, 0)),
        out_shape=jax.ShapeDtypeStruct((n, t, width), F32),
        compiler_params=_params("parallel", "parallel"),
        name="rope_k",
    )(proj, cos_t, sin_t)


def _attn_kernel(sink_ref, q_ref, k_ref, v_ref, c_ref, s_ref, o_ref, *,
                 rows, band, window, n_kv, group, hd):
    tb = q_ref.shape[1]
    gw = group * hd
    for ci in range(tb // rows):
        r0 = ci * rows
        if window:
            c_abs = pl.program_id(1) * (tb // rows) + ci
            lo = c_abs * rows - window
            start = pl.multiple_of(jnp.maximum(lo, 0), rows)
            kpos = start + lax.broadcasted_iota(jnp.int32, (1, band), 1)
            valid = jnp.logical_and(kpos >= lo, kpos < lo + band)
            ksl = pl.ds(start, band)
        else:
            ksl = slice(0, band)
        c = c_ref[r0:r0 + rows, :]
        s = s_ref[r0:r0 + rows, :]
        for h in range(n_kv):
            qh = _rope(q_ref[0, r0:r0 + rows, h * gw:(h + 1) * gw], c, s, hd)
            qs = jnp.concatenate([qh[:, g * hd:(g + 1) * hd] for g in range(group)], axis=0)
            kh = k_ref[0, ksl, h * hd:(h + 1) * hd].astype(BF16)
            vh = v_ref[0, ksl, h * hd:(h + 1) * hd].astype(BF16)
            sc = lax.dot_general(qs.astype(BF16), kh, (((1,), (1,)), ((), ())),
                                 preferred_element_type=F32) * (hd ** -0.5)
            if window:
                sc = jnp.where(valid, sc, NEG_INF)
            sink = jnp.concatenate(
                [jnp.full((rows, 1), sink_ref[h * group + g], F32) for g in range(group)], axis=0)
            m = jnp.maximum(jnp.max(sc, axis=-1, keepdims=True), sink)
            p = jnp.exp(sc - m)
            denom = jnp.sum(p, axis=-1, keepdims=True) + jnp.exp(sink - m)
            o = jnp.dot(p.astype(BF16), vh, preferred_element_type=F32) / denom
            o_ref[0, r0:r0 + rows, h * gw:(h + 1) * gw] = jnp.concatenate(
                [o[g * rows:(g + 1) * rows, :] for g in range(group)], axis=1).astype(o_ref.dtype)


def _attention(proj, k_src, k_col, v_src, v_col, cos_t, sin_t, sinks, *, aw, kvw, hd, out_width, window):
    n, t, _ = proj.shape
    tk = k_src.shape[1]
    n_kv = kvw // hd
    group = aw // kvw
    if window:
        rows, band = CHUNK, window + CHUNK
        tb = _tile(t, 4 * CHUNK, CHUNK)
    else:
        rows, band, tb = t, tk, t
    return pl.pallas_call(
        functools.partial(_attn_kernel, rows=rows, band=band, window=window,
                          n_kv=n_kv, group=group, hd=hd),
        grid=(n, t // tb),
        in_specs=[pl.BlockSpec(memory_space=pltpu.SMEM),
                  pl.BlockSpec((1, tb, aw), lambda i, j: (i, j, 0)),
                  pl.BlockSpec((1, tk, kvw), lambda i, j: (i, 0, k_col // kvw)),
                  pl.BlockSpec((1, tk, kvw), lambda i, j: (i, 0, v_col // kvw)),
                  pl.BlockSpec((tb, LANES), lambda i, j: (j, 0)),
                  pl.BlockSpec((tb, LANES), lambda i, j: (j, 0))],
        out_specs=pl.BlockSpec((1, tb, aw), lambda i, j: (i, j, 0)),
        out_shape=jax.ShapeDtypeStruct((n, t, out_width), BF16),
        compiler_params=_params("parallel", "parallel"),
        name="swa_attention",
    )(sinks, proj, k_src, v_src, cos_t, sin_t)


def _sconv_kernel(b_ref, c_ref, x_ref, st_ref, w_ref, mix_ref, z_ref, ns_ref, buf):
    del mix_ref
    t = x_ref.shape[1]
    width = w_ref.shape[0]
    buf[SUBLANES:SUBLANES + t, :] = c_ref[0] * x_ref[0]
    buf[SUBLANES - (width - 1):SUBLANES, :] = st_ref[0]
    lo = SUBLANES - (width - 1)
    acc = buf[lo:lo + t, :] * w_ref[0:1, :]
    for i in range(1, width):
        acc = acc + buf[lo + i:lo + i + t, :] * w_ref[i:i + 1, :]
    z_ref[0] = (b_ref[0] * acc).astype(z_ref.dtype)
    ns_ref[0] = buf[SUBLANES + t - (width - 1):SUBLANES + t, :]


def _sconv(proj, state, w, mix, col_b, col_c, col_x, col_out):
    n, t, _ = proj.shape
    width, ch = w.shape
    cols = (ch, col_b, col_c, col_x, col_out)
    cb = next(c for c in (2048, 1024, 512, 256, LANES)
              if all(v % c == 0 for v in cols) and t * c * 4 <= 2 * 1024 * 1024)
    blk = lambda off: pl.BlockSpec((1, t, cb), lambda i, j: (i, 0, off // cb + j))
    return pl.pallas_call(
        _sconv_kernel,
        grid=(n, ch // cb),
        in_specs=[blk(col_b), blk(col_c), blk(col_x),
                  pl.BlockSpec((1, width - 1, cb), lambda i, j: (i, 0, j)),
                  pl.BlockSpec((width, cb), lambda i, j: (0, j)),
                  pl.BlockSpec(memory_space=pl.ANY)],
        out_specs=[blk(col_out),
                   pl.BlockSpec((1, width - 1, cb), lambda i, j: (i, 0, j))],
        out_shape=[jax.ShapeDtypeStruct(mix.shape, mix.dtype),
                   jax.ShapeDtypeStruct((n, width - 1, ch), F32)],
        scratch_shapes=[pltpu.VMEM((t + SUBLANES, cb), F32)],
        input_output_aliases={5: 0},
        compiler_params=_params("parallel", "parallel"),
        name="sconv",
    )(proj, proj, proj, state, w, mix)


def _softplus(x):
    return jnp.maximum(x, 0.0) + jnp.log1p(jnp.exp(-jnp.abs(x)))


def _sigmoid(x):
    return 0.5 * (jnp.tanh(0.5 * x) + 1.0)


def _bdot(a, b):
    return jnp.dot(a.astype(BF16), b.astype(BF16), preferred_element_type=F32)


def _gdn_kernel(hn_ref, q_ref, k_ref, v_ref, gate_ref, cst_ref, cw_ref, wab_ref, wabt_ref,
                al_ref, alt_ref, dt_ref, dtt_ref, s0_ref, ow_ref, o_ref, sout_ref, s_scr, buf):
    nh = al_ref.shape[1]
    ell, hw = q_ref.shape[1], q_ref.shape[2]
    hd = hw // nh
    width = cw_ref.shape[0]
    lo = SUBLANES - (width - 1)
    cidx = pl.program_id(1)
    heads = range(nh)
    each = lambda f, *cols: [f(*x) for x in zip(*cols)]
    nt = (((1,), (1,)), ((), ()))
    tn = (((0,), (0,)), ((), ()))
    dot_nt = lambda a, b: lax.dot_general(a, b, nt, preferred_element_type=F32)
    dot_nn = lambda a, b: jnp.dot(a, b, preferred_element_type=F32)
    to16 = lambda a: a.astype(BF16)

    n_slab = 3 * nh

    @pl.when(cidx == 0)
    def _():
        s_scr[...] = s0_ref[0]
        for c in range(n_slab):
            buf[c, lo:SUBLANES, :] = cst_ref[0, :, c * hd:(c + 1) * hd]

    for part, ref in enumerate((q_ref, k_ref, v_ref)):
        for h in heads:
            buf[part * nh + h, SUBLANES:SUBLANES + ell, :] = ref[0, :, h * hd:(h + 1) * hd]

    def conv_silu(c):
        acc = buf[c, lo:lo + ell, :] * cw_ref[0:1, c * hd:(c + 1) * hd]
        for t in range(1, width):
            acc = acc + buf[c, lo + t:lo + t + ell, :] * cw_ref[t:t + 1, c * hd:(c + 1) * hd]
        return acc * _sigmoid(acc)

    l2n = lambda y: y * lax.rsqrt(jnp.sum(y * y, axis=-1, keepdims=True) + EPS)
    q = [l2n(conv_silu(h)) * (hd ** -0.5) for h in heads]
    k = [l2n(conv_silu(nh + h)) for h in heads]
    v = [conv_silu(2 * nh + h) for h in heads]
    for c in range(n_slab):
        buf[c, lo:SUBLANES, :] = buf[c, SUBLANES + ell - (width - 1):SUBLANES + ell, :]

    hx = hn_ref[0]
    ab = dot_nn(hx, wab_ref[...])
    abt = dot_nt(wabt_ref[...], hx)
    i = lax.broadcasted_iota(jnp.int32, (ell, ell), 0)
    j = lax.broadcasted_iota(jnp.int32, (ell, ell), 1)
    cum_cols = jnp.dot((i >= j).astype(F32), -jnp.exp(al_ref[...]) * _softplus(ab[:, :nh] + dt_ref[...]),
                       preferred_element_type=F32, precision=lax.Precision.HIGHEST)
    cum_rows = jnp.dot(-jnp.exp(alt_ref[...]) * _softplus(abt[:nh, :] + dtt_ref[...]), (i <= j).astype(F32),
                       preferred_element_type=F32, precision=lax.Precision.HIGHEST)
    beta_cols = _sigmoid(ab[:, nh:])
    beta_rows = _sigmoid(abt[nh:, :])
    g_row = [cum_rows[h:h + 1, :] for h in heads]
    b_row = [beta_rows[h:h + 1, :] for h in heads]
    g_col = [jnp.broadcast_to(cum_cols[:, h:h + 1], (ell, hd)) for h in heads]
    b_col = [jnp.broadcast_to(beta_cols[:, h:h + 1], (ell, ell)) for h in heads]

    eye = (i == j).astype(F32)
    shifts = range(3, ell.bit_length())
    blk = [jnp.right_shift(i, sh) == jnp.right_shift(j, sh) for sh in shifts]

    k16 = each(to16, k)
    kk = each(dot_nt, k16, k16)
    qk = each(dot_nt, each(to16, q), k16)
    decay = each(lambda gc, gr: jnp.exp(jnp.where(i >= j, gc[:, :ell] - gr, -jnp.inf)), g_col, g_row)
    a = each(lambda bc, x, d: jnp.where(i > j, bc * x * d, 0.0), b_col, kk, decay)
    p = each(lambda x: -jnp.where(blk[0], x, 0.0), a)
    tm = each(lambda x: eye + x, p)
    for _ in range(2):
        p = each(_bdot, p, p)
        tm = each(lambda t, x: t + _bdot(t, x), tm, p)
    for lvl in range(1, len(blk)):
        ring = jnp.logical_and(blk[lvl], jnp.logical_not(blk[lvl - 1]))
        tl = each(lambda t, x: _bdot(t, jnp.where(ring, x, 0.0)), tm, a)
        tm = each(lambda t, x: t - _bdot(x, t), tm, tl)
    u_base = each(lambda t, b, x: _bdot(t * b, x), tm, b_row, v)
    w16 = each(lambda t, b, g, x: to16(_bdot(t * (b * jnp.exp(g)), x)), tm, b_row, g_row, k16)
    qd16 = each(lambda x, g: to16(x * jnp.exp(g)), q, g_col)
    kd16 = each(lambda x, g: to16(x * jnp.exp(g[ell - 1:ell, :] - g)), k, g_col)
    aqk16 = each(lambda x, d: to16(x * d), qk, decay)

    s = [s_scr[h] for h in heads]
    s16 = each(to16, s)
    u16 = each(lambda ub, w, x: to16(ub - dot_nn(w, x)), u_base, w16, s16)
    o_s = each(dot_nn, qd16, s16)
    o_u = each(dot_nn, aqk16, u16)
    ds = each(lambda kd, x: lax.dot_general(kd, x, tn, preferred_element_type=F32), kd16, u16)
    for h in heads:
        s_scr[h] = s[h] * jnp.exp(g_col[h][ell - 1:ell, :]) + ds[h]
        o = o_s[h] + o_u[h]
        o = o * lax.rsqrt(jnp.mean(o * o, axis=-1, keepdims=True) + EPS) * ow_ref[...]
        gt = gate_ref[0, :, h * hd:(h + 1) * hd]
        o_ref[0, :, h * hd:(h + 1) * hd] = (o * (gt * _sigmoid(gt))).astype(o_ref.dtype)

    @pl.when(cidx == pl.num_programs(1) - 1)
    def _():
        sout_ref[0] = s_scr[...]


def _gdn_mixer(hn, proj, conv_state, conv_w, w_ab, a_log, dt_bias, s0, o_norm_w, ell):
    n, t, d = hn.shape
    nh, hd = a_log.shape[0], o_norm_w.shape[0]
    hw = nh * hd
    width = conv_w.shape[0]
    col = lambda c: pl.BlockSpec((1, ell, hw), lambda i, j: (i, j, c))
    full = lambda shape: pl.BlockSpec(shape, lambda i, j: (0,) * len(shape))
    st_spec = pl.BlockSpec((1, nh, hd, hd), lambda i, j: (i, 0, 0, 0))
    return pl.pallas_call(
        _gdn_kernel,
        grid=(n, t // ell),
        in_specs=[pl.BlockSpec((1, ell, d), lambda i, j: (i, j, 0)), col(0), col(1), col(2), col(3),
                  pl.BlockSpec((1, width - 1, 3 * hw), lambda i, j: (i, 0, 0)), full((width, 3 * hw)),
                  full((d, 2 * nh)), full((2 * nh, d)),
                  full((1, nh)), full((nh, 1)), full((1, nh)), full((nh, 1)),
                  st_spec, full((1, hd))],
        out_specs=[pl.BlockSpec((1, ell, hw), lambda i, j: (i, j, 0)), st_spec],
        out_shape=[jax.ShapeDtypeStruct((n, t, hw), BF16),
                   jax.ShapeDtypeStruct((n, nh, hd, hd), F32)],
        scratch_shapes=[pltpu.VMEM((nh, hd, hd), F32), pltpu.VMEM((3 * nh, ell + SUBLANES, hd), F32)],
        compiler_params=_params("parallel", "arbitrary"),
        name="gdn_mixer",
    )(hn, proj, proj, proj, proj, conv_state, conv_w, w_ab, w_ab.T,
      a_log.reshape(1, nh), a_log.reshape(nh, 1), dt_bias.reshape(1, nh), dt_bias.reshape(nh, 1),
      s0, o_norm_w.reshape(1, hd))


def _swa_sconv_layer(x, pos0, k_cache, v_cache, conv_state, norm_g, w_in, sinks, conv_w, w_out):
    n, t, d = x.shape
    ch = conv_w.shape[1]
    aw = w_out.shape[0] - ch
    kvw = (w_in.shape[1] - aw - 3 * ch) // 2
    hd = aw // sinks.shape[0]
    hn = _rmsnorm(x.reshape(n * t, d), norm_g, BF16)
    proj = _matmul(hn, w_in, name="in_proj_a").reshape(n, t, -1)
    cos_t, sin_t = _rope_tables(t, pos0, hd)
    k_new = _rope_k(proj, cos_t, sin_t, aw, kvw, hd)
    v_new = proj[:, t - min(t, WINDOW):, aw + kvw:aw + 2 * kvw]
    attn = functools.partial(_attention, cos_t=cos_t, sin_t=sin_t, sinks=sinks,
                             aw=aw, kvw=kvw, hd=hd, out_width=aw + ch)
    if k_cache is None:
        k_all, v_win = k_new, v_new
        mix = attn(proj, k_new, 0, proj, aw + kvw, window=WINDOW)
    else:
        k_all = jnp.concatenate([k_cache, k_new], axis=1)
        v_all = jnp.concatenate([v_cache, v_new], axis=1)
        v_win = v_all
        mix = attn(proj, k_all, 0, v_all, 0, window=0)
    col_b = aw + 2 * kvw
    mix, new_state = _sconv(proj, conv_state, conv_w, mix, col_b, col_b + ch, col_b + 2 * ch, aw)
    x = _matmul(mix.reshape(n * t, aw + ch), w_out, res=x.reshape(n * t, d), name="out_proj_a")
    return x.reshape(n, t, d), k_all[:, -WINDOW:], v_win[:, -WINDOW:], new_state


def _gdn_layer(x, conv_state, s0, norm_g, w_in, conv_w, a_log, dt_bias, o_norm_w, w_out, ell):
    n, t, d = x.shape
    hw = a_log.shape[0] * o_norm_w.shape[0]
    hn = _rmsnorm(x.reshape(n * t, d), norm_g, BF16)
    proj = _matmul(hn, w_in[:, :4 * hw], name="in_proj_g").reshape(n, t, 4 * hw)
    o, s_new = _gdn_mixer(hn.reshape(n, t, d), proj, conv_state, conv_w, w_in[:, 4 * hw:],
                          a_log, dt_bias, s0, o_norm_w, ell)
    x = _matmul(o.reshape(n * t, hw), w_out, res=x.reshape(n * t, d), name="out_proj_g").reshape(n, t, d)
    width = conv_w.shape[0]
    return x, proj[:, t - (width - 1):, :3 * hw], s_new


def _mlp(x, norm_g, w_up, w_down):
    n, t, d = x.shape
    hn = _rmsnorm(x.reshape(n * t, d), norm_g, BF16)
    hid = _matmul(hn, w_up, act="relu2", out_dtype=BF16, name="mlp_up")
    return _matmul(hid, w_down, res=x.reshape(n * t, d), name="mlp_down").reshape(n, t, d)


def kernel(x_prompt, x_sample, cache_swa_k, cache_swa_v, state_sconv, state_dn_conv, state_dn,
           attn_norm, w_in_a, sinks, sconv_w, w_out_a,
           dn_norm, w_in_g, dn_conv_w, A_log, dt_bias, o_norm_w, w_out_g,
           mlp_norm, w_up, w_down, final_norm):
    xp, xs = x_prompt, x_sample
    nb, nd = xp.shape[0], xs.shape[0]
    depth = mlp_norm.shape[0]
    n_kv, hd_a = cache_swa_k.shape[3], cache_swa_k.shape[4]
    nh_g, hd_g = state_dn.shape[2], state_dn.shape[3]
    outs = [[] for _ in range(10)]
    for li in range(depth):
        j = li // 2
        if li % 2 == 0:
            w_in, w_out = w_in_a[j].astype(BF16), w_out_a[j].astype(BF16)
            zero_state = jnp.zeros((nb,) + state_sconv.shape[2:], F32)
            xp, kp, vp, cp = _swa_sconv_layer(xp, 0, None, None, zero_state, attn_norm[j], w_in,
                                              sinks[j], sconv_w[j], w_out)
            kc = cache_swa_k[j].reshape(nd, -1, n_kv * hd_a)
            vc = cache_swa_v[j].reshape(nd, -1, n_kv * hd_a)
            xs, ks, vs, cs = _swa_sconv_layer(xs, PAST_LEN, kc, vc, state_sconv[j], attn_norm[j], w_in,
                                              sinks[j], sconv_w[j], w_out)
            shape5 = lambda a: a.reshape(a.shape[0], a.shape[1], n_kv, hd_a)
            for lst, val in zip(outs[:6], (shape5(kp), shape5(vp), shape5(ks), shape5(vs), cp, cs)):
                lst.append(val)
        else:
            w_in, w_out = w_in_g[j].astype(BF16), w_out_g[j].astype(BF16)
            zero_conv = jnp.zeros((nb,) + state_dn_conv.shape[2:], F32)
            zero_s = jnp.zeros((nb, nh_g, hd_g, hd_g), F32)
            xp, dcp, dsp = _gdn_layer(xp, zero_conv, zero_s, dn_norm[j], w_in, dn_conv_w[j], A_log[j],
                                      dt_bias[j], o_norm_w[j], w_out, CHUNK)
            xs, dcs, dss = _gdn_layer(xs, state_dn_conv[j], state_dn[j], dn_norm[j], w_in, dn_conv_w[j],
                                      A_log[j], dt_bias[j], o_norm_w[j], w_out, xs.shape[1])
            for lst, val in zip(outs[6:], (dcp, dcs, dsp, dss)):
                lst.append(val)
        wu, wd = w_up[li].astype(BF16), w_down[li].astype(BF16)
        xp = _mlp(xp, mlp_norm[li], wu, wd)
        xs = _mlp(xs, mlp_norm[li], wu, wd)
    y_prompt = _rmsnorm(xp.reshape(-1, xp.shape[-1]), final_norm, F32).reshape(xp.shape)
    y_sample = _rmsnorm(xs.reshape(-1, xs.shape[-1]), final_norm, F32).reshape(xs.shape)
    return (y_prompt, y_sample) + tuple(jnp.stack(o, 0) for o in outs)
```

```python
import functools

import jax
import jax.numpy as jnp
from jax import lax
from jax.experimental import pallas as pl
from jax.experimental.pallas import tpu as pltpu

EPS = 1e-6
CHUNK = 64
WINDOW = 128
PAST_LEN = 4096
ROPE_THETA = 500000.0
NEG_INF = -1e30
LANES = 128
SUBLANES = 8
VMEM_LIMIT = 56 * 1024 * 1024
BF16 = jnp.bfloat16
F32 = jnp.float32


def _params(*sem):
    return pltpu.CompilerParams(dimension_semantics=sem, vmem_limit_bytes=VMEM_LIMIT)


def _tile(n, pref, align=SUBLANES):
    if n <= pref:
        return n
    t = pref - pref % align
    while t >= align:
        if n % t == 0:
            return t
        t -= align
    return n


def _rmsnorm_kernel(x_ref, g_ref, o_ref):
    x = x_ref[...]
    ms = jnp.mean(x * x, axis=-1, keepdims=True)
    o_ref[...] = (x * lax.rsqrt(ms + EPS) * g_ref[...]).astype(o_ref.dtype)


def _rmsnorm(x, g, out_dtype):
    m, d = x.shape
    tm = _tile(m, 256)
    return pl.pallas_call(
        _rmsnorm_kernel,
        grid=(m // tm,),
        in_specs=[pl.BlockSpec((tm, d), lambda i: (i, 0)),
                  pl.BlockSpec((1, d), lambda i: (0, 0))],
        out_specs=pl.BlockSpec((tm, d), lambda i: (i, 0)),
        out_shape=jax.ShapeDtypeStruct((m, d), out_dtype),
        compiler_params=_params("parallel"),
        name="rmsnorm",
    )(x, g.reshape(1, d))


def _cast_kernel(w_ref, o_ref):
    o_ref[...] = w_ref[0].astype(o_ref.dtype)


def _to_bf16(w, layer, n_cols=None):
    _, kdim, n = w.shape
    n_cols = n if n_cols is None else n_cols
    tr = _tile(kdim, 512)
    tc = _tile(n_cols, 2048, LANES)
    return pl.pallas_call(
        _cast_kernel,
        grid=(kdim // tr, n_cols // tc),
        in_specs=[pl.BlockSpec((1, tr, tc), lambda i, j: (layer, i, j))],
        out_specs=pl.BlockSpec((tr, tc), lambda i, j: (i, j)),
        out_shape=jax.ShapeDtypeStruct((kdim, n_cols), BF16),
        compiler_params=_params("parallel", "parallel"),
        name="weight_cast",
    )(w)


def _mm_kernel(*refs, nk, act, has_res):
    a_ref, w_ref = refs[0], refs[1]
    r_ref = refs[2] if has_res else None
    o_ref = refs[2 + has_res]

    def finish(acc):
        if act == "relu2":
            acc = jnp.square(jnp.maximum(acc, 0.0))
        if has_res:
            acc = r_ref[...] + acc
        o_ref[...] = acc.astype(o_ref.dtype)

    if nk == 1:
        finish(jnp.dot(a_ref[...], w_ref[...], preferred_element_type=F32))
    else:
        acc_ref = refs[3 + has_res]
        k = pl.program_id(2)

        @pl.when(k == 0)
        def _():
            acc_ref[...] = jnp.zeros_like(acc_ref)

        acc_ref[...] += jnp.dot(a_ref[...], w_ref[...], preferred_element_type=F32)

        @pl.when(k == nk - 1)
        def _():
            finish(acc_ref[...])


def _matmul(a, w, *, res=None, act=None, out_dtype=F32, name="matmul"):
    m, kdim = a.shape
    n = w.shape[1]
    tm = _tile(m, 1024)
    tn = n if n < LANES else _tile(n, 1024, LANES)
    tk = _tile(kdim, 4096 if kdim <= 4096 else 2048, LANES)
    nk = kdim // tk
    in_specs = [pl.BlockSpec((tm, tk), lambda i, j, k: (i, k)),
                pl.BlockSpec((tk, tn), lambda i, j, k: (k, j))]
    args = [a, w]
    if res is not None:
        in_specs.append(pl.BlockSpec((tm, tn), lambda i, j, k: (i, j)))
        args.append(res)
    return pl.pallas_call(
        functools.partial(_mm_kernel, nk=nk, act=act, has_res=res is not None),
        grid=(m // tm, n // tn, nk),
        in_specs=in_specs,
        out_specs=pl.BlockSpec((tm, tn), lambda i, j, k: (i, j)),
        out_shape=jax.ShapeDtypeStruct((m, n), out_dtype),
        scratch_shapes=[pltpu.VMEM((tm, tn), F32)] if nk > 1 else [],
        compiler_params=_params("parallel", "parallel", "arbitrary"),
        name=name,
    )(*args)


def _rope_table_kernel(inv_ref, c_ref, s_ref, *, pos0, hd):
    shape = c_ref.shape
    pos = (pos0 + lax.broadcasted_iota(jnp.int32, shape, 0)).astype(F32)
    ang = pos * inv_ref[...]
    d = jnp.bitwise_and(lax.broadcasted_iota(jnp.int32, shape, 1), hd - 1)
    rot = hd // 4
    cos, sin = jnp.cos(ang), jnp.sin(ang)
    c_ref[...] = jnp.where(d < rot, cos, 1.0)
    s_ref[...] = jnp.where(d < rot // 2, -sin, jnp.where(d < rot, sin, 0.0))


def _rope_tables(n_pos, pos0, hd):
    half = hd // 8
    inv = ROPE_THETA ** (-jnp.arange(half, dtype=F32) / half)
    inv_lane = jnp.tile(jnp.concatenate([inv, inv, jnp.zeros((hd - 2 * half,), F32)]), LANES // hd)
    shp = jax.ShapeDtypeStruct((n_pos, LANES), F32)
    return pl.pallas_call(
        functools.partial(_rope_table_kernel, pos0=pos0, hd=hd),
        out_shape=(shp, shp),
        name="rope_tables",
    )(inv_lane.reshape(1, LANES))


def _rope(x, c, s, hd):
    w = x.shape[1]
    reps = w // LANES
    cf = jnp.concatenate([c] * reps, axis=1) if reps > 1 else c
    sf = jnp.concatenate([s] * reps, axis=1) if reps > 1 else s
    d = jnp.bitwise_and(lax.broadcasted_iota(jnp.int32, x.shape, 1), hd - 1)
    half = hd // 8
    fwd = pltpu.roll(x, w - half, 1)
    bwd = pltpu.roll(x, half, 1)
    sw = jnp.where(d < half, fwd, jnp.where(d < 2 * half, bwd, 0.0))
    return x * cf + sw * sf


def _rope_k_kernel(k_ref, c_ref, s_ref, o_ref, *, hd):
    o_ref[0] = _rope(k_ref[0], c_ref[...], s_ref[...], hd)


def _rope_k(proj, cos_t, sin_t, col0, width, hd):
    n, t, _ = proj.shape
    tb = _tile(t, 512)
    return pl.pallas_call(
        functools.partial(_rope_k_kernel, hd=hd),
        grid=(n, t // tb),
        in_specs=[pl.BlockSpec((1, tb, width), lambda i, j: (i, j, col0 // width)),
                  pl.BlockSpec((tb, LANES), lambda i, j: (j, 0)),
                  pl.BlockSpec((tb, LANES), lambda i, j: (j, 0))],
        out_specs=pl.BlockSpec((1, tb, width), lambda i, j: (i, j, 0)),
        out_shape=jax.ShapeDtypeStruct((n, t, width), F32),
        compiler_params=_params("parallel", "parallel"),
        name="rope_k",
    )(proj, cos_t, sin_t)


def _attn_kernel(sink_ref, q_ref, k_ref, v_ref, c_ref, s_ref, o_ref, *,
                 rows, band, window, n_kv, group, hd):
    tb = q_ref.shape[1]
    gw = group * hd
    each = lambda f, *cols: [f(*x) for x in zip(*cols)]
    sinks = [jnp.concatenate([jnp.full((rows, 1), sink_ref[h * group + g], F32) for g in range(group)],
                             axis=0) for h in range(n_kv)]
    n_groups = tb // rows
    per_pass = 2 if n_groups % 2 == 0 else 1
    for g0 in range(0, n_groups, per_pass):
        units = [(ci, h) for ci in range(g0, g0 + per_pass) for h in range(n_kv)]
        ksl, valid = {}, {}
        for ci in range(g0, g0 + per_pass):
            if window:
                lo = (pl.program_id(1) * n_groups + ci) * rows - window
                start = pl.multiple_of(jnp.maximum(lo, 0), rows)
                kpos = start + lax.broadcasted_iota(jnp.int32, (1, band), 1)
                valid[ci] = jnp.logical_and(kpos >= lo, kpos < lo + band)
                ksl[ci] = pl.ds(start, band)
            else:
                ksl[ci] = slice(0, band)
        qh = [_rope(q_ref[0, ci * rows:(ci + 1) * rows, h * gw:(h + 1) * gw],
                    c_ref[ci * rows:(ci + 1) * rows, :], s_ref[ci * rows:(ci + 1) * rows, :], hd)
              for ci, h in units]
        qs = each(lambda x: jnp.concatenate([x[:, g * hd:(g + 1) * hd] for g in range(group)],
                                            axis=0).astype(BF16), qh)
        kh = [k_ref[0, ksl[ci], h * hd:(h + 1) * hd].astype(BF16) for ci, h in units]
        vh = [v_ref[0, ksl[ci], h * hd:(h + 1) * hd].astype(BF16) for ci, h in units]
        sc = each(lambda a, b: lax.dot_general(a, b, (((1,), (1,)), ((), ())),
                                               preferred_element_type=F32) * (hd ** -0.5), qs, kh)
        if window:
            sc = [jnp.where(valid[ci], x, NEG_INF) for (ci, _), x in zip(units, sc)]
        sk = [sinks[h] for _, h in units]
        m = each(lambda x, s: jnp.maximum(jnp.max(x, axis=-1, keepdims=True), s), sc, sk)
        p = each(lambda x, mx: jnp.exp(x - mx), sc, m)
        denom = each(lambda x, s, mx: jnp.sum(x, axis=-1, keepdims=True) + jnp.exp(s - mx), p, sk, m)
        o = each(lambda x, v, d: jnp.dot(x.astype(BF16), v, preferred_element_type=F32) / d, p, vh, denom)
        for (ci, h), x in zip(units, o):
            o_ref[0, ci * rows:(ci + 1) * rows, h * gw:(h + 1) * gw] = jnp.concatenate(
                [x[g * rows:(g + 1) * rows, :] for g in range(group)], axis=1).astype(o_ref.dtype)


def _attention(proj, k_src, k_col, v_src, v_col, cos_t, sin_t, sinks, *, aw, kvw, hd, out_width, window):
    n, t, _ = proj.shape
    tk = k_src.shape[1]
    n_kv = kvw // hd
    group = aw // kvw
    if window:
        rows, band = CHUNK, window + CHUNK
        tb = _tile(t, 4 * CHUNK, CHUNK)
    else:
        rows, band, tb = t, tk, t
    return pl.pallas_call(
        functools.partial(_attn_kernel, rows=rows, band=band, window=window,
                          n_kv=n_kv, group=group, hd=hd),
        grid=(n, t // tb),
        in_specs=[pl.BlockSpec(memory_space=pltpu.SMEM),
                  pl.BlockSpec((1, tb, aw), lambda i, j: (i, j, 0)),
                  pl.BlockSpec((1, tk, kvw), lambda i, j: (i, 0, k_col // kvw)),
                  pl.BlockSpec((1, tk, kvw), lambda i, j: (i, 0, v_col // kvw)),
                  pl.BlockSpec((tb, LANES), lambda i, j: (j, 0)),
                  pl.BlockSpec((tb, LANES), lambda i, j: (j, 0))],
        out_specs=pl.BlockSpec((1, tb, aw), lambda i, j: (i, j, 0)),
        out_shape=jax.ShapeDtypeStruct((n, t, out_width), BF16),
        compiler_params=_params("parallel", "parallel"),
        name="swa_attention",
    )(sinks, proj, k_src, v_src, cos_t, sin_t)


def _sconv_kernel(b_ref, c_ref, x_ref, st_ref, w_ref, mix_ref, z_ref, ns_ref, buf):
    del mix_ref
    t = x_ref.shape[1]
    width = w_ref.shape[0]
    buf[SUBLANES:SUBLANES + t, :] = c_ref[0] * x_ref[0]
    buf[SUBLANES - (width - 1):SUBLANES, :] = st_ref[0]
    lo = SUBLANES - (width - 1)
    acc = buf[lo:lo + t, :] * w_ref[0:1, :]
    for i in range(1, width):
        acc = acc + buf[lo + i:lo + i + t, :] * w_ref[i:i + 1, :]
    z_ref[0] = (b_ref[0] * acc).astype(z_ref.dtype)
    ns_ref[0] = buf[SUBLANES + t - (width - 1):SUBLANES + t, :]


def _sconv(proj, state, w, mix, col_b, col_c, col_x, col_out):
    n, t, _ = proj.shape
    width, ch = w.shape
    cols = (ch, col_b, col_c, col_x, col_out)
    cb = next(c for c in (2048, 1024, 512, 256, LANES)
              if all(v % c == 0 for v in cols) and t * c * 4 <= 2 * 1024 * 1024)
    blk = lambda off: pl.BlockSpec((1, t, cb), lambda i, j: (i, 0, off // cb + j))
    return pl.pallas_call(
        _sconv_kernel,
        grid=(n, ch // cb),
        in_specs=[blk(col_b), blk(col_c), blk(col_x),
                  pl.BlockSpec((1, width - 1, cb), lambda i, j: (i, 0, j)),
                  pl.BlockSpec((width, cb), lambda i, j: (0, j)),
                  pl.BlockSpec(memory_space=pl.ANY)],
        out_specs=[blk(col_out),
                   pl.BlockSpec((1, width - 1, cb), lambda i, j: (i, 0, j))],
        out_shape=[jax.ShapeDtypeStruct(mix.shape, mix.dtype),
                   jax.ShapeDtypeStruct((n, width - 1, ch), F32)],
        scratch_shapes=[pltpu.VMEM((t + SUBLANES, cb), F32)],
        input_output_aliases={5: 0},
        compiler_params=_params("parallel", "parallel"),
        name="sconv",
    )(proj, proj, proj, state, w, mix)


def _softplus(x):
    return jnp.maximum(x, 0.0) + jnp.log1p(jnp.exp(-jnp.abs(x)))


def _sigmoid(x):
    return 0.5 * (jnp.tanh(0.5 * x) + 1.0)


def _bdot(a, b):
    return jnp.dot(a.astype(BF16), b.astype(BF16), preferred_element_type=F32)


def _gdn_kernel(hn_ref, q_ref, k_ref, v_ref, gate_ref, cst_ref, cw_ref, wab_ref, wabt_ref,
                al_ref, alt_ref, dt_ref, dtt_ref, s0_ref, ow_ref, o_ref, sout_ref, s_scr, buf):
    nh = al_ref.shape[1]
    ell, hw = q_ref.shape[1], q_ref.shape[2]
    hd = hw // nh
    width = cw_ref.shape[0]
    lo = SUBLANES - (width - 1)
    cidx = pl.program_id(1)
    heads = range(nh)
    each = lambda f, *cols: [f(*x) for x in zip(*cols)]
    nt = (((1,), (1,)), ((), ()))
    tn = (((0,), (0,)), ((), ()))
    dot_nt = lambda a, b: lax.dot_general(a, b, nt, preferred_element_type=F32)
    dot_nn = lambda a, b: jnp.dot(a, b, preferred_element_type=F32)
    to16 = lambda a: a.astype(BF16)

    n_slab = 3 * nh

    @pl.when(cidx == 0)
    def _():
        s_scr[...] = s0_ref[0]
        for c in range(n_slab):
            buf[c, lo:SUBLANES, :] = cst_ref[0, :, c * hd:(c + 1) * hd]

    for part, ref in enumerate((q_ref, k_ref, v_ref)):
        for h in heads:
            buf[part * nh + h, SUBLANES:SUBLANES + ell, :] = ref[0, :, h * hd:(h + 1) * hd]

    def conv_silu(c):
        acc = buf[c, lo:lo + ell, :] * cw_ref[0:1, c * hd:(c + 1) * hd]
        for t in range(1, width):
            acc = acc + buf[c, lo + t:lo + t + ell, :] * cw_ref[t:t + 1, c * hd:(c + 1) * hd]
        return acc * _sigmoid(acc)

    l2n = lambda y: y * lax.rsqrt(jnp.sum(y * y, axis=-1, keepdims=True) + EPS)
    q = [l2n(conv_silu(h)) * (hd ** -0.5) for h in heads]
    k = [l2n(conv_silu(nh + h)) for h in heads]
    v = [conv_silu(2 * nh + h) for h in heads]
    for c in range(n_slab):
        buf[c, lo:SUBLANES, :] = buf[c, SUBLANES + ell - (width - 1):SUBLANES + ell, :]

    hx = hn_ref[0]
    ab = dot_nn(hx, wab_ref[...])
    abt = dot_nt(wabt_ref[...], hx)
    i = lax.broadcasted_iota(jnp.int32, (ell, ell), 0)
    j = lax.broadcasted_iota(jnp.int32, (ell, ell), 1)
    cum_cols = jnp.dot((i >= j).astype(F32), -jnp.exp(al_ref[...]) * _softplus(ab[:, :nh] + dt_ref[...]),
                       preferred_element_type=F32, precision=lax.Precision.HIGHEST)
    cum_rows = jnp.dot(-jnp.exp(alt_ref[...]) * _softplus(abt[:nh, :] + dtt_ref[...]), (i <= j).astype(F32),
                       preferred_element_type=F32, precision=lax.Precision.HIGHEST)
    beta_cols = _sigmoid(ab[:, nh:])
    beta_rows = _sigmoid(abt[nh:, :])
    g_row = [cum_rows[h:h + 1, :] for h in heads]
    b_row = [beta_rows[h:h + 1, :] for h in heads]
    g_col = [jnp.broadcast_to(cum_cols[:, h:h + 1], (ell, hd)) for h in heads]
    b_col = [jnp.broadcast_to(beta_cols[:, h:h + 1], (ell, ell)) for h in heads]

    eye = (i == j).astype(F32)
    shifts = range(3, ell.bit_length())
    blk = [jnp.right_shift(i, sh) == jnp.right_shift(j, sh) for sh in shifts]

    k16 = each(to16, k)
    kk = each(dot_nt, k16, k16)
    qk = each(dot_nt, each(to16, q), k16)
    decay = each(lambda gc, gr: jnp.exp(jnp.where(i >= j, gc[:, :ell] - gr, -jnp.inf)), g_col, g_row)
    a = each(lambda bc, x, d: jnp.where(i > j, bc * x * d, 0.0), b_col, kk, decay)
    p = each(lambda x: -jnp.where(blk[0], x, 0.0), a)
    tm = each(lambda x: eye + x, p)
    for _ in range(2):
        p = each(_bdot, p, p)
        tm = each(lambda t, x: t + _bdot(t, x), tm, p)
    for lvl in range(1, len(blk)):
        ring = jnp.logical_and(blk[lvl], jnp.logical_not(blk[lvl - 1]))
        tl = each(lambda t, x: _bdot(t, jnp.where(ring, x, 0.0)), tm, a)
        tm = each(lambda t, x: t - _bdot(x, t), tm, tl)
    u_base = each(lambda t, b, x: _bdot(t * b, x), tm, b_row, v)
    w16 = each(lambda t, b, g, x: to16(_bdot(t * (b * jnp.exp(g)), x)), tm, b_row, g_row, k16)
    qd16 = each(lambda x, g: to16(x * jnp.exp(g)), q, g_col)
    kd16 = each(lambda x, g: to16(x * jnp.exp(g[ell - 1:ell, :] - g)), k, g_col)
    aqk16 = each(lambda x, d: to16(x * d), qk, decay)

    s = [s_scr[h] for h in heads]
    s16 = each(to16, s)
    u16 = each(lambda ub, w, x: to16(ub - dot_nn(w, x)), u_base, w16, s16)
    o_s = each(dot_nn, qd16, s16)
    o_u = each(dot_nn, aqk16, u16)
    ds = each(lambda kd, x: lax.dot_general(kd, x, tn, preferred_element_type=F32), kd16, u16)
    for h in heads:
        s_scr[h] = s[h] * jnp.exp(g_col[h][ell - 1:ell, :]) + ds[h]
        o = o_s[h] + o_u[h]
        o = o * lax.rsqrt(jnp.mean(o * o, axis=-1, keepdims=True) + EPS) * ow_ref[...]
        gt = gate_ref[0, :, h * hd:(h + 1) * hd]
        o_ref[0, :, h * hd:(h + 1) * hd] = (o * (gt * _sigmoid(gt))).astype(o_ref.dtype)

    @pl.when(cidx == pl.num_programs(1) - 1)
    def _():
        sout_ref[0] = s_scr[...]


def _gdn_mixer(hn, proj, conv_state, conv_w, w_ab, a_log, dt_bias, s0, o_norm_w, ell):
    n, t, d = hn.shape
    nh, hd = a_log.shape[0], o_norm_w.shape[0]
    hw = nh * hd
    width = conv_w.shape[0]
    col = lambda c: pl.BlockSpec((1, ell, hw), lambda i, j: (i, j, c))
    full = lambda shape: pl.BlockSpec(shape, lambda i, j: (0,) * len(shape))
    st_spec = pl.BlockSpec((1, nh, hd, hd), lambda i, j: (i, 0, 0, 0))
    return pl.pallas_call(
        _gdn_kernel,
        grid=(n, t // ell),
        in_specs=[pl.BlockSpec((1, ell, d), lambda i, j: (i, j, 0)), col(0), col(1), col(2), col(3),
                  pl.BlockSpec((1, width - 1, 3 * hw), lambda i, j: (i, 0, 0)), full((width, 3 * hw)),
                  full((d, 2 * nh)), full((2 * nh, d)),
                  full((1, nh)), full((nh, 1)), full((1, nh)), full((nh, 1)),
                  st_spec, full((1, hd))],
        out_specs=[pl.BlockSpec((1, ell, hw), lambda i, j: (i, j, 0)), st_spec],
        out_shape=[jax.ShapeDtypeStruct((n, t, hw), BF16),
                   jax.ShapeDtypeStruct((n, nh, hd, hd), F32)],
        scratch_shapes=[pltpu.VMEM((nh, hd, hd), F32), pltpu.VMEM((3 * nh, ell + SUBLANES, hd), F32)],
        compiler_params=_params("parallel", "arbitrary"),
        name="gdn_mixer",
    )(hn, proj, proj, proj, proj, conv_state, conv_w, w_ab, w_ab.T,
      a_log.reshape(1, nh), a_log.reshape(nh, 1), dt_bias.reshape(1, nh), dt_bias.reshape(nh, 1),
      s0, o_norm_w.reshape(1, hd))


def _swa_sconv_layer(x, pos0, k_cache, v_cache, conv_state, norm_g, w_in, sinks, conv_w, w_out):
    n, t, d = x.shape
    ch = conv_w.shape[1]
    aw = w_out.shape[0] - ch
    kvw = (w_in.shape[1] - aw - 3 * ch) // 2
    hd = aw // sinks.shape[0]
    hn = _rmsnorm(x.reshape(n * t, d), norm_g, BF16)
    proj = _matmul(hn, w_in, name="in_proj_a").reshape(n, t, -1)
    cos_t, sin_t = _rope_tables(t, pos0, hd)
    k_new = _rope_k(proj, cos_t, sin_t, aw, kvw, hd)
    v_new = proj[:, t - min(t, WINDOW):, aw + kvw:aw + 2 * kvw]
    attn = functools.partial(_attention, cos_t=cos_t, sin_t=sin_t, sinks=sinks,
                             aw=aw, kvw=kvw, hd=hd, out_width=aw + ch)
    if k_cache is None:
        k_all, v_win = k_new, v_new
        mix = attn(proj, k_new, 0, proj, aw + kvw, window=WINDOW)
    else:
        k_all = jnp.concatenate([k_cache, k_new], axis=1)
        v_all = jnp.concatenate([v_cache, v_new], axis=1)
        v_win = v_all
        mix = attn(proj, k_all, 0, v_all, 0, window=0)
    col_b = aw + 2 * kvw
    mix, new_state = _sconv(proj, conv_state, conv_w, mix, col_b, col_b + ch, col_b + 2 * ch, aw)
    x = _matmul(mix.reshape(n * t, aw + ch), w_out, res=x.reshape(n * t, d), name="out_proj_a")
    return x.reshape(n, t, d), k_all[:, -WINDOW:], v_win[:, -WINDOW:], new_state


def _gdn_layer(x, conv_state, s0, norm_g, w_in, w_ab, conv_w, a_log, dt_bias, o_norm_w, w_out, ell):
    n, t, d = x.shape
    hw = a_log.shape[0] * o_norm_w.shape[0]
    hn = _rmsnorm(x.reshape(n * t, d), norm_g, BF16)
    proj = _matmul(hn, w_in, name="in_proj_g").reshape(n, t, 4 * hw)
    o, s_new = _gdn_mixer(hn.reshape(n, t, d), proj, conv_state, conv_w, w_ab,
                          a_log, dt_bias, s0, o_norm_w, ell)
    x = _matmul(o.reshape(n * t, hw), w_out, res=x.reshape(n * t, d), name="out_proj_g").reshape(n, t, d)
    width = conv_w.shape[0]
    return x, proj[:, t - (width - 1):, :3 * hw], s_new


def _mlp(x, norm_g, w_up, w_down):
    n, t, d = x.shape
    hn = _rmsnorm(x.reshape(n * t, d), norm_g, BF16)
    hid = _matmul(hn, w_up, act="relu2", out_dtype=BF16, name="mlp_up")
    return _matmul(hid, w_down, res=x.reshape(n * t, d), name="mlp_down").reshape(n, t, d)


def kernel(x_prompt, x_sample, cache_swa_k, cache_swa_v, state_sconv, state_dn_conv, state_dn,
           attn_norm, w_in_a, sinks, sconv_w, w_out_a,
           dn_norm, w_in_g, dn_conv_w, A_log, dt_bias, o_norm_w, w_out_g,
           mlp_norm, w_up, w_down, final_norm):
    xp, xs = x_prompt, x_sample
    nb, nd = xp.shape[0], xs.shape[0]
    depth = mlp_norm.shape[0]
    n_kv, hd_a = cache_swa_k.shape[3], cache_swa_k.shape[4]
    nh_g, hd_g = state_dn.shape[2], state_dn.shape[3]
    outs = [[] for _ in range(10)]
    for li in range(depth):
        j = li // 2
        if li % 2 == 0:
            w_in, w_out = _to_bf16(w_in_a, j), _to_bf16(w_out_a, j)
            zero_state = jnp.zeros((nb,) + state_sconv.shape[2:], F32)
            xp, kp, vp, cp = _swa_sconv_layer(xp, 0, None, None, zero_state, attn_norm[j], w_in,
                                              sinks[j], sconv_w[j], w_out)
            kc = cache_swa_k[j].reshape(nd, -1, n_kv * hd_a)
            vc = cache_swa_v[j].reshape(nd, -1, n_kv * hd_a)
            xs, ks, vs, cs = _swa_sconv_layer(xs, PAST_LEN, kc, vc, state_sconv[j], attn_norm[j], w_in,
                                              sinks[j], sconv_w[j], w_out)
            shape5 = lambda a: a.reshape(a.shape[0], a.shape[1], n_kv, hd_a)
            for lst, val in zip(outs[:6], (shape5(kp), shape5(vp), shape5(ks), shape5(vs), cp, cs)):
                lst.append(val)
        else:
            hw_g = nh_g * hd_g
            w_in, w_out = _to_bf16(w_in_g, j, 4 * hw_g), _to_bf16(w_out_g, j)
            w_ab = w_in_g[j, :, 4 * hw_g:].astype(BF16)
            zero_conv = jnp.zeros((nb,) + state_dn_conv.shape[2:], F32)
            zero_s = jnp.zeros((nb, nh_g, hd_g, hd_g), F32)
            xp, dcp, dsp = _gdn_layer(xp, zero_conv, zero_s, dn_norm[j], w_in, w_ab, dn_conv_w[j], A_log[j],
                                      dt_bias[j], o_norm_w[j], w_out, CHUNK)
            xs, dcs, dss = _gdn_layer(xs, state_dn_conv[j], state_dn[j], dn_norm[j], w_in, w_ab, dn_conv_w[j],
                                      A_log[j], dt_bias[j], o_norm_w[j], w_out, xs.shape[1])
            for lst, val in zip(outs[6:], (dcp, dcs, dsp, dss)):
                lst.append(val)
        wu, wd = _to_bf16(w_up, li), _to_bf16(w_down, li)
        xp = _mlp(xp, mlp_norm[li], wu, wd)
        xs = _mlp(xs, mlp_norm[li], wu, wd)
    y_prompt = _rmsnorm(xp.reshape(-1, xp.shape[-1]), final_norm, F32).reshape(xp.shape)
    y_sample = _rmsnorm(xs.reshape(-1, xs.shape[-1]), final_norm, F32).reshape(xs.shape)
    return (y_prompt, y_sample) + tuple(jnp.stack(o, 0) for o in outs)
```

```python
import functools

import jax
import jax.numpy as jnp
from jax import lax
from jax.experimental import pallas as pl
from jax.experimental.pallas import tpu as pltpu

EPS = 1e-6
CHUNK = 64
WINDOW = 128
PAST_LEN = 4096
ROPE_THETA = 500000.0
NEG_INF = -1e30
LANES = 128
SUBLANES = 8
VMEM_LIMIT = 56 * 1024 * 1024
BF16 = jnp.bfloat16
F32 = jnp.float32


def _params(*sem):
    return pltpu.CompilerParams(dimension_semantics=sem, vmem_limit_bytes=VMEM_LIMIT)


def _tile(n, pref, align=SUBLANES):
    if n <= pref:
        return n
    t = pref - pref % align
    while t >= align:
        if n % t == 0:
            return t
        t -= align
    return n


def _rmsnorm_kernel(x_ref, g_ref, o_ref):
    x = x_ref[...]
    ms = jnp.mean(x * x, axis=-1, keepdims=True)
    o_ref[...] = (x * lax.rsqrt(ms + EPS) * g_ref[...]).astype(o_ref.dtype)


def _rmsnorm(x, g, out_dtype):
    m, d = x.shape
    tm = _tile(m, 256)
    return pl.pallas_call(
        _rmsnorm_kernel,
        grid=(m // tm,),
        in_specs=[pl.BlockSpec((tm, d), lambda i: (i, 0)),
                  pl.BlockSpec((1, d), lambda i: (0, 0))],
        out_specs=pl.BlockSpec((tm, d), lambda i: (i, 0)),
        out_shape=jax.ShapeDtypeStruct((m, d), out_dtype),
        compiler_params=_params("parallel"),
        name="rmsnorm",
    )(x, g.reshape(1, d))


def _cast_kernel(*refs, has_gain, transposed):
    w = refs[0][0]
    if has_gain:
        w = w * refs[1][...]
    refs[-1][...] = (w.T if transposed else w).astype(refs[-1].dtype)


def _to_bf16(w, layer, col0=0, n_cols=None, gain=None, transposed=False):
    kdim, n = (w.shape[2], w.shape[1]) if transposed else (w.shape[1], w.shape[2])
    n_cols = n - col0 if n_cols is None else n_cols
    tr = _tile(kdim, 512, LANES)
    tc = _tile(n_cols, 2048, LANES) if n_cols >= LANES else n_cols
    assert col0 % tc == 0 and (transposed or tc % LANES == 0)
    if transposed:
        in_specs = [pl.BlockSpec((1, tc, tr), lambda i, j: (layer, col0 // tc + j, i))]
        gain_spec, gain_shape = pl.BlockSpec((1, tr), lambda i, j: (0, i)), (1, kdim)
    else:
        in_specs = [pl.BlockSpec((1, tr, tc), lambda i, j: (layer, i, col0 // tc + j))]
        gain_spec, gain_shape = pl.BlockSpec((tr, 1), lambda i, j: (i, 0)), (kdim, 1)
    args = [w]
    if gain is not None:
        in_specs.append(gain_spec)
        args.append(gain.reshape(gain_shape))
    return pl.pallas_call(
        functools.partial(_cast_kernel, has_gain=gain is not None, transposed=transposed),
        grid=(kdim // tr, n_cols // tc),
        in_specs=in_specs,
        out_specs=pl.BlockSpec((tr, tc), lambda i, j: (i, j)),
        out_shape=jax.ShapeDtypeStruct((kdim, n_cols), BF16),
        compiler_params=_params("parallel", "parallel"),
        name="weight_cast",
    )(*args)


def _row_rms_scale(ss, d):
    return lax.rsqrt(jnp.sum(ss, axis=-1, keepdims=True) * (1.0 / d) + EPS)


def _mm_kernel(*refs, nk, act, has_res, has_ss, emit_norm, d_norm):
    refs = list(refs)
    a_ref, w_ref = refs.pop(0), refs.pop(0)
    r_ref = refs.pop(0) if has_res else None
    ssin_ref = refs.pop(0) if has_ss else None
    o_ref = refs.pop(0)
    o16_ref, ssout_ref = (refs.pop(0), refs.pop(0)) if emit_norm else (None, None)

    def finish(acc_of):
        tm = o_ref.shape[0]
        rc = min(tm, 256)
        parts = []
        for r0 in range(0, tm, rc):
            rows = slice(r0, r0 + rc)
            acc = acc_of(rows)
            if has_ss:
                acc = acc * _row_rms_scale(ssin_ref[rows, :], d_norm)
            if act == "relu2":
                acc = jnp.square(jnp.maximum(acc, 0.0))
            if has_res:
                acc = r_ref[rows, :] + acc
            o_ref[rows, :] = acc.astype(o_ref.dtype)
            if emit_norm:
                o16_ref[rows, :] = acc.astype(BF16)
                sq = acc * acc
                part = sq[:, :LANES]
                for c in range(1, sq.shape[1] // LANES):
                    part = part + sq[:, c * LANES:(c + 1) * LANES]
                parts.append(part)
        if emit_norm:
            part = jnp.concatenate(parts, axis=0) if len(parts) > 1 else parts[0]
            j = pl.program_id(1)

            @pl.when(j == 0)
            def _():
                ssout_ref[...] = part

            @pl.when(j > 0)
            def _():
                ssout_ref[...] += part

    if nk == 1:
        full = jnp.dot(a_ref[...], w_ref[...], preferred_element_type=F32)
        finish(lambda rows: full[rows, :])
    else:
        acc_ref = refs.pop(0)
        k = pl.program_id(2)

        @pl.when(k == 0)
        def _():
            acc_ref[...] = jnp.zeros_like(acc_ref)

        acc_ref[...] += jnp.dot(a_ref[...], w_ref[...], preferred_element_type=F32)

        @pl.when(k == nk - 1)
        def _():
            finish(lambda rows: acc_ref[rows, :])


def _matmul(a, w, *, res=None, act=None, row_ss=None, emit_norm=False, out_dtype=F32, name="matmul"):
    m, kdim = a.shape
    n = w.shape[1]
    tm = _tile(m, 1024)
    tk = _tile(kdim, 4096 if kdim <= 4096 else 2048, LANES)
    nk = kdim // tk
    tn = n if n < LANES else _tile(n, 512 if (emit_norm and nk == 1) else 1024, LANES)
    row_blk = lambda width: pl.BlockSpec((tm, width), lambda i, j, k: (i, 0))
    out_blk = pl.BlockSpec((tm, tn), lambda i, j, k: (i, j))
    in_specs = [pl.BlockSpec((tm, tk), lambda i, j, k: (i, k)),
                pl.BlockSpec((tk, tn), lambda i, j, k: (k, j))]
    args = [a, w]
    if res is not None:
        in_specs.append(out_blk)
        args.append(res)
    if row_ss is not None:
        in_specs.append(row_blk(LANES))
        args.append(row_ss)
    out_specs, out_shape = [out_blk], [jax.ShapeDtypeStruct((m, n), out_dtype)]
    if emit_norm:
        out_specs += [out_blk, row_blk(LANES)]
        out_shape += [jax.ShapeDtypeStruct((m, n), BF16), jax.ShapeDtypeStruct((m, LANES), F32)]
    out = pl.pallas_call(
        functools.partial(_mm_kernel, nk=nk, act=act, has_res=res is not None,
                          has_ss=row_ss is not None, emit_norm=emit_norm, d_norm=kdim),
        grid=(m // tm, n // tn, nk),
        in_specs=in_specs,
        out_specs=out_specs,
        out_shape=out_shape,
        scratch_shapes=[pltpu.VMEM((tm, tn), F32)] if nk > 1 else [],
        compiler_params=_params("parallel", "arbitrary", "arbitrary"),
        name=name,
    )(*args)
    return out if emit_norm else out[0]


def _rope_table_kernel(inv_ref, c_ref, s_ref, *, pos0, hd):
    shape = c_ref.shape
    pos = (pos0 + lax.broadcasted_iota(jnp.int32, shape, 0)).astype(F32)
    ang = pos * inv_ref[...]
    d = jnp.bitwise_and(lax.broadcasted_iota(jnp.int32, shape, 1), hd - 1)
    rot = hd // 4
    cos, sin = jnp.cos(ang), jnp.sin(ang)
    c_ref[...] = jnp.where(d < rot, cos, 1.0)
    s_ref[...] = jnp.where(d < rot // 2, -sin, jnp.where(d < rot, sin, 0.0))


def _rope_tables(n_pos, pos0, hd):
    half = hd // 8
    inv = ROPE_THETA ** (-jnp.arange(half, dtype=F32) / half)
    inv_lane = jnp.tile(jnp.concatenate([inv, inv, jnp.zeros((hd - 2 * half,), F32)]), LANES // hd)
    shp = jax.ShapeDtypeStruct((n_pos, LANES), F32)
    return pl.pallas_call(
        functools.partial(_rope_table_kernel, pos0=pos0, hd=hd),
        out_shape=(shp, shp),
        name="rope_tables",
    )(inv_lane.reshape(1, LANES))


def _rope(x, c, s, hd):
    w = x.shape[1]
    reps = w // LANES
    cf = jnp.concatenate([c] * reps, axis=1) if reps > 1 else c
    sf = jnp.concatenate([s] * reps, axis=1) if reps > 1 else s
    d = jnp.bitwise_and(lax.broadcasted_iota(jnp.int32, x.shape, 1), hd - 1)
    half = hd // 8
    fwd = pltpu.roll(x, w - half, 1)
    bwd = pltpu.roll(x, half, 1)
    sw = jnp.where(d < half, fwd, jnp.where(d < 2 * half, bwd, 0.0))
    return x * cf + sw * sf


def _rope_k_kernel(k_ref, c_ref, s_ref, o_ref, *, hd):
    o_ref[0] = _rope(k_ref[0], c_ref[...], s_ref[...], hd)


def _rope_k(proj, cos_t, sin_t, col0, width, hd):
    n, t, _ = proj.shape
    tb = _tile(t, 512)
    return pl.pallas_call(
        functools.partial(_rope_k_kernel, hd=hd),
        grid=(n, t // tb),
        in_specs=[pl.BlockSpec((1, tb, width), lambda i, j: (i, j, col0 // width)),
                  pl.BlockSpec((tb, LANES), lambda i, j: (j, 0)),
                  pl.BlockSpec((tb, LANES), lambda i, j: (j, 0))],
        out_specs=pl.BlockSpec((1, tb, width), lambda i, j: (i, j, 0)),
        out_shape=jax.ShapeDtypeStruct((n, t, width), F32),
        compiler_params=_params("parallel", "parallel"),
        name="rope_k",
    )(proj, cos_t, sin_t)


def _attn_kernel(sink_ref, q_ref, k_ref, v_ref, c_ref, s_ref, o_ref, *,
                 rows, band, window, n_kv, group, hd):
    tb = q_ref.shape[1]
    gw = group * hd
    each = lambda f, *cols: [f(*x) for x in zip(*cols)]
    sinks = [jnp.concatenate([jnp.full((rows, 1), sink_ref[h * group + g], F32) for g in range(group)],
                             axis=0) for h in range(n_kv)]
    n_groups = tb // rows
    per_pass = 2 if n_groups % 2 == 0 else 1
    for g0 in range(0, n_groups, per_pass):
        units = [(ci, h) for ci in range(g0, g0 + per_pass) for h in range(n_kv)]
        ksl, valid = {}, {}
        for ci in range(g0, g0 + per_pass):
            if window:
                lo = (pl.program_id(1) * n_groups + ci) * rows - window
                start = pl.multiple_of(jnp.maximum(lo, 0), rows)
                kpos = start + lax.broadcasted_iota(jnp.int32, (1, band), 1)
                valid[ci] = jnp.logical_and(kpos >= lo, kpos < lo + band)
                ksl[ci] = pl.ds(start, band)
            else:
                ksl[ci] = slice(0, band)
        qh = [_rope(q_ref[0, ci * rows:(ci + 1) * rows, h * gw:(h + 1) * gw],
                    c_ref[ci * rows:(ci + 1) * rows, :], s_ref[ci * rows:(ci + 1) * rows, :], hd)
              for ci, h in units]
        qs = each(lambda x: jnp.concatenate([x[:, g * hd:(g + 1) * hd] for g in range(group)],
                                            axis=0).astype(BF16), qh)
        kh = [k_ref[0, ksl[ci], h * hd:(h + 1) * hd].astype(BF16) for ci, h in units]
        vh = [v_ref[0, ksl[ci], h * hd:(h + 1) * hd].astype(BF16) for ci, h in units]
        sc = each(lambda a, b: lax.dot_general(a, b, (((1,), (1,)), ((), ())),
                                               preferred_element_type=F32) * (hd ** -0.5), qs, kh)
        if window:
            sc = [jnp.where(valid[ci], x, NEG_INF) for (ci, _), x in zip(units, sc)]
        sk = [sinks[h] for _, h in units]
        m = each(lambda x, s: jnp.maximum(jnp.max(x, axis=-1, keepdims=True), s), sc, sk)
        p = each(lambda x, mx: jnp.exp(x - mx), sc, m)
        denom = each(lambda x, s, mx: jnp.sum(x, axis=-1, keepdims=True) + jnp.exp(s - mx), p, sk, m)
        o = each(lambda x, v, d: jnp.dot(x.astype(BF16), v, preferred_element_type=F32) / d, p, vh, denom)
        for (ci, h), x in zip(units, o):
            o_ref[0, ci * rows:(ci + 1) * rows, h * gw:(h + 1) * gw] = jnp.concatenate(
                [x[g * rows:(g + 1) * rows, :] for g in range(group)], axis=1).astype(o_ref.dtype)


def _attention(proj, k_src, k_col, v_src, v_col, cos_t, sin_t, sinks, *, aw, kvw, hd, out_width, window):
    n, t, _ = proj.shape
    tk = k_src.shape[1]
    n_kv = kvw // hd
    group = aw // kvw
    if window:
        rows, band = CHUNK, window + CHUNK
        tb = _tile(t, 4 * CHUNK, CHUNK)
    else:
        rows, band, tb = t, tk, t
    return pl.pallas_call(
        functools.partial(_attn_kernel, rows=rows, band=band, window=window,
                          n_kv=n_kv, group=group, hd=hd),
        grid=(n, t // tb),
        in_specs=[pl.BlockSpec(memory_space=pltpu.SMEM),
                  pl.BlockSpec((1, tb, aw), lambda i, j: (i, j, 0)),
                  pl.BlockSpec((1, tk, kvw), lambda i, j: (i, 0, k_col // kvw)),
                  pl.BlockSpec((1, tk, kvw), lambda i, j: (i, 0, v_col // kvw)),
                  pl.BlockSpec((tb, LANES), lambda i, j: (j, 0)),
                  pl.BlockSpec((tb, LANES), lambda i, j: (j, 0))],
        out_specs=pl.BlockSpec((1, tb, aw), lambda i, j: (i, j, 0)),
        out_shape=jax.ShapeDtypeStruct((n, t, out_width), BF16),
        compiler_params=_params("parallel", "parallel"),
        name="swa_attention",
    )(sinks, proj, k_src, v_src, cos_t, sin_t)


def _sconv_kernel(b_ref, c_ref, x_ref, st_ref, w_ref, mix_ref, z_ref, ns_ref, buf):
    del mix_ref
    t = x_ref.shape[1]
    width = w_ref.shape[0]
    buf[SUBLANES:SUBLANES + t, :] = c_ref[0] * x_ref[0]
    buf[SUBLANES - (width - 1):SUBLANES, :] = st_ref[0]
    lo = SUBLANES - (width - 1)
    acc = buf[lo:lo + t, :] * w_ref[0:1, :]
    for i in range(1, width):
        acc = acc + buf[lo + i:lo + i + t, :] * w_ref[i:i + 1, :]
    z_ref[0] = (b_ref[0] * acc).astype(z_ref.dtype)
    ns_ref[0] = buf[SUBLANES + t - (width - 1):SUBLANES + t, :]


def _sconv(proj, state, w, mix, col_b, col_c, col_x, col_out):
    n, t, _ = proj.shape
    width, ch = w.shape
    cols = (ch, col_b, col_c, col_x, col_out)
    cb = next(c for c in (2048, 1024, 512, 256, LANES)
              if all(v % c == 0 for v in cols) and t * c * 4 <= 2 * 1024 * 1024)
    blk = lambda off: pl.BlockSpec((1, t, cb), lambda i, j: (i, 0, off // cb + j))
    return pl.pallas_call(
        _sconv_kernel,
        grid=(n, ch // cb),
        in_specs=[blk(col_b), blk(col_c), blk(col_x),
                  pl.BlockSpec((1, width - 1, cb), lambda i, j: (i, 0, j)),
                  pl.BlockSpec((width, cb), lambda i, j: (0, j)),
                  pl.BlockSpec(memory_space=pl.ANY)],
        out_specs=[blk(col_out),
                   pl.BlockSpec((1, width - 1, cb), lambda i, j: (i, 0, j))],
        out_shape=[jax.ShapeDtypeStruct(mix.shape, mix.dtype),
                   jax.ShapeDtypeStruct((n, width - 1, ch), F32)],
        scratch_shapes=[pltpu.VMEM((t + SUBLANES, cb), F32)],
        input_output_aliases={5: 0},
        compiler_params=_params("parallel", "parallel"),
        name="sconv",
    )(proj, proj, proj, state, w, mix)


def _softplus(x):
    return jnp.maximum(x, 0.0) + jnp.log1p(jnp.exp(-jnp.abs(x)))


def _sigmoid(x):
    return 0.5 * (jnp.tanh(0.5 * x) + 1.0)


def _bdot(a, b):
    return jnp.dot(a.astype(BF16), b.astype(BF16), preferred_element_type=F32)


def _gdn_kernel(x16_ref, ss_ref, q_ref, k_ref, v_ref, gate_ref, cst_ref, cw_ref, wab_ref, wabt_ref,
                al_ref, alt_ref, dt_ref, dtt_ref, s0_ref, ow_ref, o_ref, sout_ref, s_scr, buf):
    nh = al_ref.shape[1]
    ell, hw = q_ref.shape[1], q_ref.shape[2]
    hd = hw // nh
    width = cw_ref.shape[0]
    lo = SUBLANES - (width - 1)
    cidx = pl.program_id(1)
    heads = range(nh)
    each = lambda f, *cols: [f(*x) for x in zip(*cols)]
    nt = (((1,), (1,)), ((), ()))
    tn = (((0,), (0,)), ((), ()))
    dot_nt = lambda a, b: lax.dot_general(a, b, nt, preferred_element_type=F32)
    dot_nn = lambda a, b: jnp.dot(a, b, preferred_element_type=F32)
    to16 = lambda a: a.astype(BF16)

    n_slab = 3 * nh

    @pl.when(cidx == 0)
    def _():
        s_scr[...] = s0_ref[0]
        for c in range(n_slab):
            buf[c, lo:SUBLANES, :] = cst_ref[0, :, c * hd:(c + 1) * hd]

    for part, ref in enumerate((q_ref, k_ref, v_ref)):
        for h in heads:
            buf[part * nh + h, SUBLANES:SUBLANES + ell, :] = ref[0, :, h * hd:(h + 1) * hd]

    def conv_silu(c):
        acc = buf[c, lo:lo + ell, :] * cw_ref[0:1, c * hd:(c + 1) * hd]
        for t in range(1, width):
            acc = acc + buf[c, lo + t:lo + t + ell, :] * cw_ref[t:t + 1, c * hd:(c + 1) * hd]
        return acc * _sigmoid(acc)

    l2n = lambda y: y * lax.rsqrt(jnp.sum(y * y, axis=-1, keepdims=True) + EPS)
    q = [l2n(conv_silu(h)) * (hd ** -0.5) for h in heads]
    k = [l2n(conv_silu(nh + h)) for h in heads]
    v = [conv_silu(2 * nh + h) for h in heads]
    for c in range(n_slab):
        buf[c, lo:SUBLANES, :] = buf[c, SUBLANES + ell - (width - 1):SUBLANES + ell, :]

    i = lax.broadcasted_iota(jnp.int32, (ell, ell), 0)
    j = lax.broadcasted_iota(jnp.int32, (ell, ell), 1)
    hx = x16_ref[0]
    r_col = _row_rms_scale(ss_ref[0], hx.shape[1])
    r_row = jnp.sum(jnp.where(i == j, jnp.broadcast_to(r_col, (ell, ell)), 0.0), axis=0, keepdims=True)
    ab = dot_nn(hx, wab_ref[...]) * r_col
    abt = dot_nt(wabt_ref[...], hx) * r_row
    cum_cols = jnp.dot((i >= j).astype(F32), -jnp.exp(al_ref[...]) * _softplus(ab[:, :nh] + dt_ref[...]),
                       preferred_element_type=F32, precision=lax.Precision.HIGHEST)
    cum_rows = jnp.dot(-jnp.exp(alt_ref[...]) * _softplus(abt[:nh, :] + dtt_ref[...]), (i <= j).astype(F32),
                       preferred_element_type=F32, precision=lax.Precision.HIGHEST)
    beta_cols = _sigmoid(ab[:, nh:])
    beta_rows = _sigmoid(abt[nh:, :])
    g_row = [cum_rows[h:h + 1, :] for h in heads]
    b_row = [beta_rows[h:h + 1, :] for h in heads]
    g_col = [jnp.broadcast_to(cum_cols[:, h:h + 1], (ell, hd)) for h in heads]
    b_col = [jnp.broadcast_to(beta_cols[:, h:h + 1], (ell, ell)) for h in heads]

    eye = (i == j).astype(F32)
    shifts = range(3, ell.bit_length())
    blk = [jnp.right_shift(i, sh) == jnp.right_shift(j, sh) for sh in shifts]

    k16 = each(to16, k)
    kk = each(dot_nt, k16, k16)
    qk = each(dot_nt, each(to16, q), k16)
    decay = each(lambda gc, gr: jnp.exp(jnp.where(i >= j, gc[:, :ell] - gr, -jnp.inf)), g_col, g_row)
    a = each(lambda bc, x, d: jnp.where(i > j, bc * x * d, 0.0), b_col, kk, decay)
    p = each(lambda x: -jnp.where(blk[0], x, 0.0), a)
    tm = each(lambda x: eye + x, p)
    for _ in range(2):
        p = each(_bdot, p, p)
        tm = each(lambda t, x: t + _bdot(t, x), tm, p)
    for lvl in range(1, len(blk)):
        ring = jnp.logical_and(blk[lvl], jnp.logical_not(blk[lvl - 1]))
        tl = each(lambda t, x: _bdot(t, jnp.where(ring, x, 0.0)), tm, a)
        tm = each(lambda t, x: t - _bdot(x, t), tm, tl)
    u_base = each(lambda t, b, x: _bdot(t * b, x), tm, b_row, v)
    w16 = each(lambda t, b, g, x: to16(_bdot(t * (b * jnp.exp(g)), x)), tm, b_row, g_row, k16)
    qd16 = each(lambda x, g: to16(x * jnp.exp(g)), q, g_col)
    kd16 = each(lambda x, g: to16(x * jnp.exp(g[ell - 1:ell, :] - g)), k, g_col)
    aqk16 = each(lambda x, d: to16(x * d), qk, decay)

    s = [s_scr[h] for h in heads]
    s16 = each(to16, s)
    u16 = each(lambda ub, w, x: to16(ub - dot_nn(w, x)), u_base, w16, s16)
    o_s = each(dot_nn, qd16, s16)
    o_u = each(dot_nn, aqk16, u16)
    ds = each(lambda kd, x: lax.dot_general(kd, x, tn, preferred_element_type=F32), kd16, u16)
    for h in heads:
        s_scr[h] = s[h] * jnp.exp(g_col[h][ell - 1:ell, :]) + ds[h]
        o = o_s[h] + o_u[h]
        o = o * lax.rsqrt(jnp.mean(o * o, axis=-1, keepdims=True) + EPS) * ow_ref[...]
        gt = gate_ref[0, :, h * hd:(h + 1) * hd]
        o_ref[0, :, h * hd:(h + 1) * hd] = (o * (gt * _sigmoid(gt))).astype(o_ref.dtype)

    @pl.when(cidx == pl.num_programs(1) - 1)
    def _():
        sout_ref[0] = s_scr[...]


def _gdn_mixer(x16, ss, proj, conv_state, conv_w, w_ab, a_log, dt_bias, s0, o_norm_w, ell):
    n, t, d = x16.shape
    nh, hd = a_log.shape[0], o_norm_w.shape[0]
    hw = nh * hd
    width = conv_w.shape[0]
    col = lambda c: pl.BlockSpec((1, ell, hw), lambda i, j: (i, j, c))
    full = lambda shape: pl.BlockSpec(shape, lambda i, j: (0,) * len(shape))
    st_spec = pl.BlockSpec((1, nh, hd, hd), lambda i, j: (i, 0, 0, 0))
    return pl.pallas_call(
        _gdn_kernel,
        grid=(n, t // ell),
        in_specs=[pl.BlockSpec((1, ell, d), lambda i, j: (i, j, 0)),
                  pl.BlockSpec((1, ell, LANES), lambda i, j: (i, j, 0)), col(0), col(1), col(2), col(3),
                  pl.BlockSpec((1, width - 1, 3 * hw), lambda i, j: (i, 0, 0)), full((width, 3 * hw)),
                  full((d, 2 * nh)), full((2 * nh, d)),
                  full((1, nh)), full((nh, 1)), full((1, nh)), full((nh, 1)),
                  st_spec, full((1, hd))],
        out_specs=[pl.BlockSpec((1, ell, hw), lambda i, j: (i, j, 0)), st_spec],
        out_shape=[jax.ShapeDtypeStruct((n, t, hw), BF16),
                   jax.ShapeDtypeStruct((n, nh, hd, hd), F32)],
        scratch_shapes=[pltpu.VMEM((nh, hd, hd), F32), pltpu.VMEM((3 * nh, ell + SUBLANES, hd), F32)],
        compiler_params=_params("parallel", "arbitrary"),
        name="gdn_mixer",
    )(x16, ss, proj, proj, proj, proj, conv_state, conv_w, w_ab, w_ab.T,
      a_log.reshape(1, nh), a_log.reshape(nh, 1), dt_bias.reshape(1, nh), dt_bias.reshape(nh, 1),
      s0, o_norm_w.reshape(1, hd))


def _norm_prep_kernel(x_ref, o16_ref, ss_ref):
    x = x_ref[...]
    o16_ref[...] = x.astype(BF16)
    sq = x * x
    part = sq[:, :LANES]
    for c in range(1, sq.shape[1] // LANES):
        part = part + sq[:, c * LANES:(c + 1) * LANES]
    ss_ref[...] = part


def _norm_prep(x):
    m, d = x.shape
    tm = _tile(m, 256)
    x16, ss = pl.pallas_call(
        _norm_prep_kernel,
        grid=(m // tm,),
        in_specs=[pl.BlockSpec((tm, d), lambda i: (i, 0))],
        out_specs=[pl.BlockSpec((tm, d), lambda i: (i, 0)), pl.BlockSpec((tm, LANES), lambda i: (i, 0))],
        out_shape=[jax.ShapeDtypeStruct((m, d), BF16), jax.ShapeDtypeStruct((m, LANES), F32)],
        compiler_params=_params("parallel"),
        name="norm_prep",
    )(x)
    return x, x16, ss


def _swa_sconv_layer(stream, n, t, pos0, k_cache, v_cache, conv_state, w_in, sinks, conv_w, w_out):
    x, x16, ss = stream
    ch = conv_w.shape[1]
    aw = w_out.shape[0] - ch
    kvw = (w_in.shape[1] - aw - 3 * ch) // 2
    hd = aw // sinks.shape[0]
    proj = _matmul(x16, w_in, row_ss=ss, name="in_proj_a").reshape(n, t, -1)
    cos_t, sin_t = _rope_tables(t, pos0, hd)
    k_new = _rope_k(proj, cos_t, sin_t, aw, kvw, hd)
    v_new = proj[:, t - min(t, WINDOW):, aw + kvw:aw + 2 * kvw]
    attn = functools.partial(_attention, cos_t=cos_t, sin_t=sin_t, sinks=sinks,
                             aw=aw, kvw=kvw, hd=hd, out_width=aw + ch)
    if k_cache is None:
        k_all, v_win = k_new, v_new
        mix = attn(proj, k_new, 0, proj, aw + kvw, window=WINDOW)
    else:
        k_all = jnp.concatenate([k_cache, k_new], axis=1)
        v_all = jnp.concatenate([v_cache, v_new], axis=1)
        v_win = v_all
        mix = attn(proj, k_all, 0, v_all, 0, window=0)
    col_b = aw + 2 * kvw
    mix, new_state = _sconv(proj, conv_state, conv_w, mix, col_b, col_b + ch, col_b + 2 * ch, aw)
    stream = _matmul(mix.reshape(n * t, aw + ch), w_out, res=x, emit_norm=True, name="out_proj_a")
    return stream, k_all[:, -WINDOW:], v_win[:, -WINDOW:], new_state


def _gdn_layer(stream, n, t, conv_state, s0, w_in, w_ab, conv_w, a_log, dt_bias, o_norm_w, w_out, ell):
    x, x16, ss = stream
    d = x.shape[1]
    hw = a_log.shape[0] * o_norm_w.shape[0]
    proj = _matmul(x16, w_in, row_ss=ss, name="in_proj_g").reshape(n, t, 4 * hw)
    o, s_new = _gdn_mixer(x16.reshape(n, t, d), ss.reshape(n, t, LANES), proj, conv_state, conv_w, w_ab,
                          a_log, dt_bias, s0, o_norm_w, ell)
    stream = _matmul(o.reshape(n * t, hw), w_out, res=x, emit_norm=True, name="out_proj_g")
    width = conv_w.shape[0]
    return stream, proj[:, t - (width - 1):, :3 * hw], s_new


def _mlp(stream, w_up, w_down, last):
    x, x16, ss = stream
    hid = _matmul(x16, w_up, act="relu2", row_ss=ss, out_dtype=BF16, name="mlp_up")
    if last:
        return _matmul(hid, w_down, res=x, name="mlp_down"), None, None
    return _matmul(hid, w_down, res=x, emit_norm=True, name="mlp_down")


def kernel(x_prompt, x_sample, cache_swa_k, cache_swa_v, state_sconv, state_dn_conv, state_dn,
           attn_norm, w_in_a, sinks, sconv_w, w_out_a,
           dn_norm, w_in_g, dn_conv_w, A_log, dt_bias, o_norm_w, w_out_g,
           mlp_norm, w_up, w_down, final_norm):
    (nb, tp, d), (nd, ts, _) = x_prompt.shape, x_sample.shape
    depth = mlp_norm.shape[0]
    n_kv, hd_a = cache_swa_k.shape[3], cache_swa_k.shape[4]
    nh_g, hd_g = state_dn.shape[2], state_dn.shape[3]
    sp = _norm_prep(x_prompt.reshape(nb * tp, d))
    ss = _norm_prep(x_sample.reshape(nd * ts, d))
    outs = [[] for _ in range(10)]
    for li in range(depth):
        j = li // 2
        if li % 2 == 0:
            w_in, w_out = _to_bf16(w_in_a, j, gain=attn_norm[j]), _to_bf16(w_out_a, j)
            zero_state = jnp.zeros((nb,) + state_sconv.shape[2:], F32)
            sp, kp, vp, cp = _swa_sconv_layer(sp, nb, tp, 0, None, None, zero_state, w_in,
                                              sinks[j], sconv_w[j], w_out)
            kc = cache_swa_k[j].reshape(nd, -1, n_kv * hd_a)
            vc = cache_swa_v[j].reshape(nd, -1, n_kv * hd_a)
            ss, ks, vs, cs = _swa_sconv_layer(ss, nd, ts, PAST_LEN, kc, vc, state_sconv[j], w_in,
                                              sinks[j], sconv_w[j], w_out)
            shape5 = lambda a: a.reshape(a.shape[0], a.shape[1], n_kv, hd_a)
            for lst, val in zip(outs[:6], (shape5(kp), shape5(vp), shape5(ks), shape5(vs), cp, cs)):
                lst.append(val)
        else:
            hw_g = nh_g * hd_g
            w_t = jnp.swapaxes(w_in_g, 1, 2)
            w_in = _to_bf16(w_t, j, 0, 4 * hw_g, gain=dn_norm[j], transposed=True)
            w_ab = _to_bf16(w_t, j, 4 * hw_g, gain=dn_norm[j], transposed=True)
            w_out = _to_bf16(w_out_g, j)
            zero_conv = jnp.zeros((nb,) + state_dn_conv.shape[2:], F32)
            zero_s = jnp.zeros((nb, nh_g, hd_g, hd_g), F32)
            sp, dcp, dsp = _gdn_layer(sp, nb, tp, zero_conv, zero_s, w_in, w_ab, dn_conv_w[j], A_log[j],
                                      dt_bias[j], o_norm_w[j], w_out, CHUNK)
            ss, dcs, dss = _gdn_layer(ss, nd, ts, state_dn_conv[j], state_dn[j], w_in, w_ab, dn_conv_w[j],
                                      A_log[j], dt_bias[j], o_norm_w[j], w_out, ts)
            for lst, val in zip(outs[6:], (dcp, dcs, dsp, dss)):
                lst.append(val)
        wu, wd = _to_bf16(w_up, li, gain=mlp_norm[li]), _to_bf16(w_down, li)
        sp = _mlp(sp, wu, wd, li == depth - 1)
        ss = _mlp(ss, wu, wd, li == depth - 1)
    y_prompt = _rmsnorm(sp[0], final_norm, F32).reshape(x_prompt.shape)
    y_sample = _rmsnorm(ss[0], final_norm, F32).reshape(x_sample.shape)
    return (y_prompt, y_sample) + tuple(jnp.stack(o, 0) for o in outs)
```

```python
import functools
from typing import NamedTuple, Optional

import jax
import jax.numpy as jnp
from jax import lax
from jax.experimental import pallas as pl
from jax.experimental.pallas import tpu as pltpu

EPS = 1e-6
CHUNK = 64
WINDOW = 128
PAST_LEN = 4096
ROPE_THETA = 500000.0
NEG_INF = -1e30
LANES = 128
SUBLANES = 8
VMEM_LIMIT = 56 * 1024 * 1024
GDN_HEAD_GROUP = 32
BF16 = jnp.bfloat16
F32 = jnp.float32


def _params(*sem):
    return pltpu.CompilerParams(dimension_semantics=sem, vmem_limit_bytes=VMEM_LIMIT)


def _tile(n, pref, align=SUBLANES):
    if n <= pref:
        return n
    t = pref - pref % align
    while t >= align:
        if n % t == 0:
            return t
        t -= align
    return n


def _rmsnorm_kernel(x_ref, g_ref, o_ref):
    x = x_ref[...]
    ms = jnp.mean(x * x, axis=-1, keepdims=True)
    o_ref[...] = (x * lax.rsqrt(ms + EPS) * g_ref[...]).astype(o_ref.dtype)


def _rmsnorm(x, g, out_dtype):
    m, d = x.shape
    tm = _tile(m, 256)
    return pl.pallas_call(
        _rmsnorm_kernel,
        grid=(m // tm,),
        in_specs=[pl.BlockSpec((tm, d), lambda i: (i, 0)),
                  pl.BlockSpec((1, d), lambda i: (0, 0))],
        out_specs=pl.BlockSpec((tm, d), lambda i: (i, 0)),
        out_shape=jax.ShapeDtypeStruct((m, d), out_dtype),
        compiler_params=_params("parallel"),
        name="rmsnorm",
    )(x, g.reshape(1, d))


def _cast_kernel(*refs, has_gain, transposed):
    w = refs[0][0]
    if has_gain:
        w = w * refs[1][...]
    refs[-1][...] = (w.T if transposed else w).astype(refs[-1].dtype)


def _to_bf16(w, layer, col0=0, n_cols=None, gain=None, transposed=False):
    kdim, n = (w.shape[2], w.shape[1]) if transposed else (w.shape[1], w.shape[2])
    n_cols = n - col0 if n_cols is None else n_cols
    tr = _tile(kdim, 512, LANES)
    tc = _tile(n_cols, 2048, LANES) if n_cols >= LANES else n_cols
    assert col0 % tc == 0 and (transposed or tc % LANES == 0)
    if transposed:
        in_specs = [pl.BlockSpec((1, tc, tr), lambda i, j: (layer, col0 // tc + j, i))]
        gain_spec, gain_shape = pl.BlockSpec((1, tr), lambda i, j: (0, i)), (1, kdim)
    else:
        in_specs = [pl.BlockSpec((1, tr, tc), lambda i, j: (layer, i, col0 // tc + j))]
        gain_spec, gain_shape = pl.BlockSpec((tr, 1), lambda i, j: (i, 0)), (kdim, 1)
    args = [w]
    if gain is not None:
        in_specs.append(gain_spec)
        args.append(gain.reshape(gain_shape))
    return pl.pallas_call(
        functools.partial(_cast_kernel, has_gain=gain is not None, transposed=transposed),
        grid=(kdim // tr, n_cols // tc),
        in_specs=in_specs,
        out_specs=pl.BlockSpec((tr, tc), lambda i, j: (i, j)),
        out_shape=jax.ShapeDtypeStruct((kdim, n_cols), BF16),
        compiler_params=_params("parallel", "parallel"),
        name="weight_cast",
    )(*args)


def _row_rms_scale(ss, d):
    return lax.rsqrt(jnp.sum(ss, axis=-1, keepdims=True) * (1.0 / d) + EPS)


def _mm_kernel(*refs, nk, act, has_res, has_ss, emit_norm, d_norm, side):
    refs = list(refs)
    a_ref, w_ref = refs.pop(0), refs.pop(0)
    r_ref = refs.pop(0) if has_res else None
    ssin_ref = refs.pop(0) if has_ss else None
    side_in = [(refs.pop(0), refs.pop(0) if has_gain else None) for has_gain, _ in side]
    o_ref = refs.pop(0)
    o16_ref, ssout_ref = (refs.pop(0), refs.pop(0)) if emit_norm else (None, None)

    for (sw_ref, g_ref), (_, transposed) in zip(side_in, side):
        so_ref = refs.pop(0)
        wv = sw_ref[0] if g_ref is None else sw_ref[0] * g_ref[...]
        so_ref[...] = (wv.T if transposed else wv).astype(so_ref.dtype)

    def finish(acc_of):
        tm = o_ref.shape[0]
        rc = min(tm, 256)
        parts = []
        for r0 in range(0, tm, rc):
            rows = slice(r0, r0 + rc)
            acc = acc_of(rows)
            if has_ss:
                acc = acc * _row_rms_scale(ssin_ref[rows, :], d_norm)
            if act == "relu2":
                acc = jnp.square(jnp.maximum(acc, 0.0))
            if has_res:
                acc = r_ref[rows, :] + acc
            o_ref[rows, :] = acc.astype(o_ref.dtype)
            if emit_norm:
                o16_ref[rows, :] = acc.astype(BF16)
                sq = acc * acc
                part = sq[:, :LANES]
                for c in range(1, sq.shape[1] // LANES):
                    part = part + sq[:, c * LANES:(c + 1) * LANES]
                parts.append(part)
        if emit_norm:
            part = jnp.concatenate(parts, axis=0) if len(parts) > 1 else parts[0]
            j = pl.program_id(1)

            @pl.when(j == 0)
            def _():
                ssout_ref[...] = part

            @pl.when(j > 0)
            def _():
                ssout_ref[...] += part

    if nk == 1:
        full = jnp.dot(a_ref[...], w_ref[...], preferred_element_type=F32)
        finish(lambda rows: full[rows, :])
    else:
        acc_ref = refs.pop(0)
        k = pl.program_id(2)

        @pl.when(k == 0)
        def _():
            acc_ref[...] = jnp.zeros_like(acc_ref)

        acc_ref[...] += jnp.dot(a_ref[...], w_ref[...], preferred_element_type=F32)

        @pl.when(k == nk - 1)
        def _():
            finish(lambda rows: acc_ref[rows, :])


class CastJob(NamedTuple):
    w: jax.Array
    layer: int
    gain: Optional[jax.Array] = None
    transposed: bool = False
    n_cols: Optional[int] = None


def _run_cast(job):
    return _to_bf16(job.w, job.layer, 0, job.n_cols, gain=job.gain, transposed=job.transposed)


SIDE_BLOCK_BYTES = 2 * 1024 * 1024


def _side_plan(job, n_steps):
    if job.transposed:
        n_cols, kdim = job.n_cols or job.w.shape[1], job.w.shape[2]
        n_blk, in_blk, out_blk = n_cols // LANES, (1, LANES, kdim), (kdim, LANES)
        if n_cols % LANES or kdim % LANES:
            return None
    else:
        kdim, n_cols = job.w.shape[1], job.n_cols or job.w.shape[2]
        n_blk = 1
        while n_blk * 2 <= n_steps and kdim % (n_blk * 2 * 16) == 0:
            n_blk *= 2
        in_blk, out_blk = (1, kdim // n_blk, n_cols), (kdim // n_blk, n_cols)
        if n_cols != job.w.shape[2]:
            return None
    if n_blk > n_steps or 4 * in_blk[1] * in_blk[2] > SIDE_BLOCK_BYTES:
        return None
    return n_blk, in_blk, out_blk, (kdim, n_cols)


def _matmul(a, w, *, res=None, act=None, row_ss=None, emit_norm=False, side_casts=(), out_dtype=F32,
            name="matmul"):
    m, kdim = a.shape
    n = w.shape[1]
    tm = _tile(m, 1024)
    tk = _tile(kdim, 4096 if kdim <= 4096 else 2048, LANES)
    nk = kdim // tk
    tn = n if n < LANES else _tile(n, 512 if (emit_norm and nk == 1) else 1024, LANES)
    nj = n // tn
    n_steps = (m // tm) * nj * nk
    row_blk = lambda width: pl.BlockSpec((tm, width), lambda i, j, k: (i, 0))
    out_blk = pl.BlockSpec((tm, tn), lambda i, j, k: (i, j))
    in_specs = [pl.BlockSpec((tm, tk), lambda i, j, k: (i, k)),
                pl.BlockSpec((tk, tn), lambda i, j, k: (k, j))]
    args = [a, w]
    if res is not None:
        in_specs.append(out_blk)
        args.append(res)
    if row_ss is not None:
        in_specs.append(row_blk(LANES))
        args.append(row_ss)
    out_specs, out_shape = [out_blk], [jax.ShapeDtypeStruct((m, n), out_dtype)]
    if emit_norm:
        out_specs += [out_blk, row_blk(LANES)]
        out_shape += [jax.ShapeDtypeStruct((m, n), BF16), jax.ShapeDtypeStruct((m, LANES), F32)]
    plans = [_side_plan(job, n_steps) for job in side_casts]
    side = []
    for job, plan in zip(side_casts, plans):
        if plan is None:
            continue
        n_blk, in_b, out_b, shape = plan
        blk = lambda i, j, k, rep=n_steps // n_blk, last=n_blk - 1: jnp.minimum(((i * nj + j) * nk + k) // rep, last)
        layer = job.layer
        if job.transposed:
            in_specs.append(pl.BlockSpec(in_b, lambda i, j, k, blk=blk, layer=layer: (layer, blk(i, j, k), 0)))
            gain_spec, gain_shape = pl.BlockSpec((1, shape[0]), lambda i, j, k: (0, 0)), (1, shape[0])
            out_specs.append(pl.BlockSpec(out_b, lambda i, j, k, blk=blk: (0, blk(i, j, k))))
        else:
            in_specs.append(pl.BlockSpec(in_b, lambda i, j, k, blk=blk, layer=layer: (layer, blk(i, j, k), 0)))
            gain_spec = pl.BlockSpec((in_b[1], 1), lambda i, j, k, blk=blk: (blk(i, j, k), 0))
            gain_shape = (shape[0], 1)
            out_specs.append(pl.BlockSpec(out_b, lambda i, j, k, blk=blk: (blk(i, j, k), 0)))
        args.append(job.w)
        if job.gain is not None:
            in_specs.append(gain_spec)
            args.append(job.gain.reshape(gain_shape))
        out_shape.append(jax.ShapeDtypeStruct(shape, BF16))
        side.append((job.gain is not None, job.transposed))
    out = pl.pallas_call(
        functools.partial(_mm_kernel, nk=nk, act=act, has_res=res is not None,
                          has_ss=row_ss is not None, emit_norm=emit_norm, d_norm=kdim, side=tuple(side)),
        grid=(m // tm, nj, nk),
        in_specs=in_specs,
        out_specs=out_specs,
        out_shape=out_shape,
        scratch_shapes=[pltpu.VMEM((tm, tn), F32)] if nk > 1 else [],
        compiler_params=_params("arbitrary", "arbitrary", "arbitrary"),
        name=name,
    )(*args)
    out = list(out)
    n_main = 3 if emit_norm else 1
    hosted = iter(out[n_main:])
    casts = [next(hosted) if plan is not None else _run_cast(job) for job, plan in zip(side_casts, plans)]
    result = out[:n_main] + casts
    return result[0] if len(result) == 1 else tuple(result)


def _rope_table_kernel(inv_ref, c_ref, s_ref, *, pos0, hd):
    shape = c_ref.shape
    pos = (pos0 + lax.broadcasted_iota(jnp.int32, shape, 0)).astype(F32)
    ang = pos * inv_ref[...]
    d = jnp.bitwise_and(lax.broadcasted_iota(jnp.int32, shape, 1), hd - 1)
    rot = hd // 4
    cos, sin = jnp.cos(ang), jnp.sin(ang)
    c_ref[...] = jnp.where(d < rot, cos, 1.0)
    s_ref[...] = jnp.where(d < rot // 2, -sin, jnp.where(d < rot, sin, 0.0))


def _rope_tables(n_pos, pos0, hd):
    half = hd // 8
    inv = ROPE_THETA ** (-jnp.arange(half, dtype=F32) / half)
    inv_lane = jnp.tile(jnp.concatenate([inv, inv, jnp.zeros((hd - 2 * half,), F32)]), LANES // hd)
    shp = jax.ShapeDtypeStruct((n_pos, LANES), F32)
    return pl.pallas_call(
        functools.partial(_rope_table_kernel, pos0=pos0, hd=hd),
        out_shape=(shp, shp),
        name="rope_tables",
    )(inv_lane.reshape(1, LANES))


def _rope(x, c, s, hd):
    w = x.shape[1]
    reps = w // LANES
    cf = jnp.concatenate([c] * reps, axis=1) if reps > 1 else c
    sf = jnp.concatenate([s] * reps, axis=1) if reps > 1 else s
    d = jnp.bitwise_and(lax.broadcasted_iota(jnp.int32, x.shape, 1), hd - 1)
    half = hd // 8
    fwd = pltpu.roll(x, w - half, 1)
    bwd = pltpu.roll(x, half, 1)
    sw = jnp.where(d < half, fwd, jnp.where(d < 2 * half, bwd, 0.0))
    return x * cf + sw * sf


def _rope_k_kernel(k_ref, c_ref, s_ref, o_ref, *, hd):
    o_ref[0] = _rope(k_ref[0], c_ref[...], s_ref[...], hd)


def _rope_k(proj, cos_t, sin_t, col0, width, hd):
    n, t, _ = proj.shape
    tb = _tile(t, 512)
    return pl.pallas_call(
        functools.partial(_rope_k_kernel, hd=hd),
        grid=(n, t // tb),
        in_specs=[pl.BlockSpec((1, tb, width), lambda i, j: (i, j, col0 // width)),
                  pl.BlockSpec((tb, LANES), lambda i, j: (j, 0)),
                  pl.BlockSpec((tb, LANES), lambda i, j: (j, 0))],
        out_specs=pl.BlockSpec((1, tb, width), lambda i, j: (i, j, 0)),
        out_shape=jax.ShapeDtypeStruct((n, t, width), F32),
        compiler_params=_params("parallel", "parallel"),
        name="rope_k",
    )(proj, cos_t, sin_t)


def _attn_kernel(sink_ref, q_ref, k_ref, v_ref, c_ref, s_ref, o_ref, *,
                 rows, band, window, n_kv, group, hd):
    tb = q_ref.shape[1]
    gw = group * hd
    each = lambda f, *cols: [f(*x) for x in zip(*cols)]
    sinks = [jnp.concatenate([jnp.full((rows, 1), sink_ref[h * group + g], F32) for g in range(group)],
                             axis=0) for h in range(n_kv)]
    n_groups = tb // rows
    per_pass = 2 if n_groups % 2 == 0 else 1
    for g0 in range(0, n_groups, per_pass):
        units = [(ci, h) for ci in range(g0, g0 + per_pass) for h in range(n_kv)]
        ksl, valid = {}, {}
        for ci in range(g0, g0 + per_pass):
            if window:
                lo = (pl.program_id(1) * n_groups + ci) * rows - window
                start = pl.multiple_of(jnp.maximum(lo, 0), rows)
                kpos = start + lax.broadcasted_iota(jnp.int32, (1, band), 1)
                valid[ci] = jnp.logical_and(kpos >= lo, kpos < lo + band)
                ksl[ci] = pl.ds(start, band)
            else:
                ksl[ci] = slice(0, band)
        qh = [_rope(q_ref[0, ci * rows:(ci + 1) * rows, h * gw:(h + 1) * gw],
                    c_ref[ci * rows:(ci + 1) * rows, :], s_ref[ci * rows:(ci + 1) * rows, :], hd)
              for ci, h in units]
        qs = each(lambda x: jnp.concatenate([x[:, g * hd:(g + 1) * hd] for g in range(group)],
                                            axis=0).astype(BF16), qh)
        kh = [k_ref[0, ksl[ci], h * hd:(h + 1) * hd].astype(BF16) for ci, h in units]
        vh = [v_ref[0, ksl[ci], h * hd:(h + 1) * hd].astype(BF16) for ci, h in units]
        sc = each(lambda a, b: lax.dot_general(a, b, (((1,), (1,)), ((), ())),
                                               preferred_element_type=F32) * (hd ** -0.5), qs, kh)
        if window:
            sc = [jnp.where(valid[ci], x, NEG_INF) for (ci, _), x in zip(units, sc)]
        sk = [sinks[h] for _, h in units]
        m = each(lambda x, s: jnp.maximum(jnp.max(x, axis=-1, keepdims=True), s), sc, sk)
        p = each(lambda x, mx: jnp.exp(x - mx), sc, m)
        denom = each(lambda x, s, mx: jnp.sum(x, axis=-1, keepdims=True) + jnp.exp(s - mx), p, sk, m)
        o = each(lambda x, v, d: jnp.dot(x.astype(BF16), v, preferred_element_type=F32) / d, p, vh, denom)
        for (ci, h), x in zip(units, o):
            o_ref[0, ci * rows:(ci + 1) * rows, h * gw:(h + 1) * gw] = jnp.concatenate(
                [x[g * rows:(g + 1) * rows, :] for g in range(group)], axis=1).astype(o_ref.dtype)


def _attention(proj, k_src, k_col, v_src, v_col, cos_t, sin_t, sinks, *, aw, kvw, hd, out_width, window):
    n, t, _ = proj.shape
    tk = k_src.shape[1]
    n_kv = kvw // hd
    group = aw // kvw
    if window:
        rows, band = CHUNK, window + CHUNK
        tb = _tile(t, 4 * CHUNK, CHUNK)
    else:
        rows, band, tb = t, tk, t
    return pl.pallas_call(
        functools.partial(_attn_kernel, rows=rows, band=band, window=window,
                          n_kv=n_kv, group=group, hd=hd),
        grid=(n, t // tb),
        in_specs=[pl.BlockSpec(memory_space=pltpu.SMEM),
                  pl.BlockSpec((1, tb, aw), lambda i, j: (i, j, 0)),
                  pl.BlockSpec((1, tk, kvw), lambda i, j: (i, 0, k_col // kvw)),
                  pl.BlockSpec((1, tk, kvw), lambda i, j: (i, 0, v_col // kvw)),
                  pl.BlockSpec((tb, LANES), lambda i, j: (j, 0)),
                  pl.BlockSpec((tb, LANES), lambda i, j: (j, 0))],
        out_specs=pl.BlockSpec((1, tb, aw), lambda i, j: (i, j, 0)),
        out_shape=jax.ShapeDtypeStruct((n, t, out_width), BF16),
        compiler_params=_params("parallel", "parallel"),
        name="swa_attention",
    )(sinks, proj, k_src, v_src, cos_t, sin_t)


def _sconv_kernel(b_ref, c_ref, x_ref, st_ref, w_ref, mix_ref, z_ref, ns_ref, buf):
    del mix_ref
    t = x_ref.shape[1]
    width = w_ref.shape[0]
    buf[SUBLANES:SUBLANES + t, :] = c_ref[0] * x_ref[0]
    buf[SUBLANES - (width - 1):SUBLANES, :] = st_ref[0]
    lo = SUBLANES - (width - 1)
    acc = buf[lo:lo + t, :] * w_ref[0:1, :]
    for i in range(1, width):
        acc = acc + buf[lo + i:lo + i + t, :] * w_ref[i:i + 1, :]
    z_ref[0] = (b_ref[0] * acc).astype(z_ref.dtype)
    ns_ref[0] = buf[SUBLANES + t - (width - 1):SUBLANES + t, :]


def _sconv(proj, state, w, mix, col_b, col_c, col_x, col_out):
    n, t, _ = proj.shape
    width, ch = w.shape
    cols = (ch, col_b, col_c, col_x, col_out)
    cb = next(c for c in (2048, 1024, 512, 256, LANES)
              if all(v % c == 0 for v in cols) and t * c * 4 <= 2 * 1024 * 1024)
    blk = lambda off: pl.BlockSpec((1, t, cb), lambda i, j: (i, 0, off // cb + j))
    return pl.pallas_call(
        _sconv_kernel,
        grid=(n, ch // cb),
        in_specs=[blk(col_b), blk(col_c), blk(col_x),
                  pl.BlockSpec((1, width - 1, cb), lambda i, j: (i, 0, j)),
                  pl.BlockSpec((width, cb), lambda i, j: (0, j)),
                  pl.BlockSpec(memory_space=pl.ANY)],
        out_specs=[blk(col_out),
                   pl.BlockSpec((1, width - 1, cb), lambda i, j: (i, 0, j))],
        out_shape=[jax.ShapeDtypeStruct(mix.shape, mix.dtype),
                   jax.ShapeDtypeStruct((n, width - 1, ch), F32)],
        scratch_shapes=[pltpu.VMEM((t + SUBLANES, cb), F32)],
        input_output_aliases={5: 0},
        compiler_params=_params("parallel", "parallel"),
        name="sconv",
    )(proj, proj, proj, state, w, mix)


def _softplus(x):
    return jnp.maximum(x, 0.0) + jnp.log1p(jnp.exp(-jnp.abs(x)))


def _sigmoid(x):
    return 0.5 * (jnp.tanh(0.5 * x) + 1.0)


def _silu(x):
    h = 0.5 * x
    return h + h * jnp.tanh(h)


def _bdot(a, b):
    return jnp.dot(a.astype(BF16), b.astype(BF16), preferred_element_type=F32)


def _gdn_kernel(x16_ref, ss_ref, q_ref, k_ref, v_ref, gate_ref, cst_ref, cw_ref, wab_ref, wabt_ref,
                al_ref, alt_ref, dt_ref, dtt_ref, s0_ref, ow_ref, o_ref, sout_ref, s_scr, buf, *, group):
    nh = al_ref.shape[1]
    ell, hw = q_ref.shape[1], q_ref.shape[2]
    hd = hw // nh
    width = cw_ref.shape[0]
    lo = SUBLANES - (width - 1)
    cidx = pl.program_id(1)
    each = lambda f, *cols: [f(*x) for x in zip(*cols)]
    nt = (((1,), (1,)), ((), ()))
    tn = (((0,), (0,)), ((), ()))
    dot_nt = lambda a, b: lax.dot_general(a, b, nt, preferred_element_type=F32)
    dot_nn = lambda a, b: jnp.dot(a, b, preferred_element_type=F32)
    to16 = lambda a: a.astype(BF16)

    l2n = lambda y: y * lax.rsqrt(jnp.sum(y * y, axis=-1, keepdims=True) + EPS)

    def conv_silu(c):
        buf[c, SUBLANES:SUBLANES + ell, :] = (q_ref, k_ref, v_ref)[c // nh][0, :, (c % nh) * hd:(c % nh + 1) * hd]
        acc = buf[c, lo:lo + ell, :] * cw_ref[0:1, c * hd:(c + 1) * hd]
        for t in range(1, width):
            acc = acc + buf[c, lo + t:lo + t + ell, :] * cw_ref[t:t + 1, c * hd:(c + 1) * hd]
        buf[c, lo:SUBLANES, :] = buf[c, SUBLANES + ell - (width - 1):SUBLANES + ell, :]
        return _silu(acc)

    @pl.when(cidx == 0)
    def _():
        s_scr[...] = s0_ref[0]
        for c in range(3 * nh):
            buf[c, lo:SUBLANES, :] = cst_ref[0, :, c * hd:(c + 1) * hd]

    i = lax.broadcasted_iota(jnp.int32, (ell, ell), 0)
    j = lax.broadcasted_iota(jnp.int32, (ell, ell), 1)
    hx = x16_ref[0]
    r_col = _row_rms_scale(ss_ref[0], hx.shape[1])
    r_row = jnp.sum(jnp.where(i == j, jnp.broadcast_to(r_col, (ell, ell)), 0.0), axis=0, keepdims=True)
    ab = dot_nn(hx, wab_ref[...]) * r_col
    abt = dot_nt(wabt_ref[...], hx) * r_row
    cum_cols = jnp.dot((i >= j).astype(F32), -jnp.exp(al_ref[...]) * _softplus(ab[:, :nh] + dt_ref[...]),
                       preferred_element_type=F32, precision=lax.Precision.HIGHEST)
    cum_rows = jnp.dot(-jnp.exp(alt_ref[...]) * _softplus(abt[:nh, :] + dtt_ref[...]), (i <= j).astype(F32),
                       preferred_element_type=F32, precision=lax.Precision.HIGHEST)
    beta_cols = _sigmoid(ab[:, nh:])
    beta_rows = _sigmoid(abt[nh:, :])

    eye = (i == j).astype(F32)
    shifts = range(3, ell.bit_length())
    blk = [jnp.right_shift(i, sh) == jnp.right_shift(j, sh) for sh in shifts]

    def run(hs):
        q = [l2n(conv_silu(h)) * (hd ** -0.5) for h in hs]
        k = [l2n(conv_silu(nh + h)) for h in hs]
        v = [conv_silu(2 * nh + h) for h in hs]
        g_row = [cum_rows[h:h + 1, :] for h in hs]
        b_row = [beta_rows[h:h + 1, :] for h in hs]
        g_col = [jnp.broadcast_to(cum_cols[:, h:h + 1], (ell, hd)) for h in hs]
        b_col = [jnp.broadcast_to(beta_cols[:, h:h + 1], (ell, ell)) for h in hs]
        k16 = each(to16, k)
        kk = each(dot_nt, k16, k16)
        qk = each(dot_nt, each(to16, q), k16)
        decay = each(lambda gc, gr: jnp.exp(jnp.where(i >= j, gc[:, :ell] - gr, -jnp.inf)), g_col, g_row)
        a = each(lambda bc, x, d: jnp.where(i > j, bc * x * d, 0.0), b_col, kk, decay)
        p = each(lambda x: -jnp.where(blk[0], x, 0.0), a)
        tm = each(lambda x: eye + x, p)
        for _ in range(2):
            p = each(_bdot, p, p)
            tm = each(lambda t, x: t + _bdot(t, x), tm, p)
        for lvl in range(1, len(blk)):
            ring = jnp.logical_and(blk[lvl], jnp.logical_not(blk[lvl - 1]))
            tl = each(lambda t, x: _bdot(t, jnp.where(ring, x, 0.0)), tm, a)
            tm = each(lambda t, x: t - _bdot(x, t), tm, tl)
        u_base = each(lambda t, b, x: _bdot(t * b, x), tm, b_row, v)
        w16 = each(lambda t, b, g, x: to16(_bdot(t * (b * jnp.exp(g)), x)), tm, b_row, g_row, k16)
        qd16 = each(lambda x, g: to16(x * jnp.exp(g)), q, g_col)
        kd16 = each(lambda x, g: to16(x * jnp.exp(g[ell - 1:ell, :] - g)), k, g_col)
        aqk16 = each(lambda x, d: to16(x * d), qk, decay)
        s = [s_scr[h] for h in hs]
        s16 = each(to16, s)
        u16 = each(lambda ub, w, x: to16(ub - dot_nn(w, x)), u_base, w16, s16)
        o_s = each(dot_nn, qd16, s16)
        o_u = each(dot_nn, aqk16, u16)
        ds = each(lambda kd, x: lax.dot_general(kd, x, tn, preferred_element_type=F32), kd16, u16)
        for r, h in enumerate(hs):
            s_scr[h] = s[r] * jnp.exp(g_col[r][ell - 1:ell, :]) + ds[r]
            o = o_s[r] + o_u[r]
            o = o * lax.rsqrt(jnp.mean(o * o, axis=-1, keepdims=True) + EPS) * ow_ref[...]
            gt = gate_ref[0, :, h * hd:(h + 1) * hd]
            o_ref[0, :, h * hd:(h + 1) * hd] = (o * _silu(gt)).astype(o_ref.dtype)

    for h0 in range(0, nh, group):
        run(range(h0, min(h0 + group, nh)))

    @pl.when(cidx == pl.num_programs(1) - 1)
    def _():
        sout_ref[0] = s_scr[...]


def _gdn_mixer(x16, ss, proj, conv_state, conv_w, w_ab, a_log, dt_bias, s0, o_norm_w, ell):
    n, t, d = x16.shape
    nh, hd = a_log.shape[0], o_norm_w.shape[0]
    hw = nh * hd
    width = conv_w.shape[0]
    col = lambda c: pl.BlockSpec((1, ell, hw), lambda i, j: (i, j, c))
    full = lambda shape: pl.BlockSpec(shape, lambda i, j: (0,) * len(shape))
    st_spec = pl.BlockSpec((1, nh, hd, hd), lambda i, j: (i, 0, 0, 0))
    return pl.pallas_call(
        functools.partial(_gdn_kernel, group=_tile(nh, GDN_HEAD_GROUP, 1)),
        grid=(n, t // ell),
        in_specs=[pl.BlockSpec((1, ell, d), lambda i, j: (i, j, 0)),
                  pl.BlockSpec((1, ell, LANES), lambda i, j: (i, j, 0)), col(0), col(1), col(2), col(3),
                  pl.BlockSpec((1, width - 1, 3 * hw), lambda i, j: (i, 0, 0)), full((width, 3 * hw)),
                  full((d, 2 * nh)), full((2 * nh, d)),
                  full((1, nh)), full((nh, 1)), full((1, nh)), full((nh, 1)),
                  st_spec, full((1, hd))],
        out_specs=[pl.BlockSpec((1, ell, hw), lambda i, j: (i, j, 0)), st_spec],
        out_shape=[jax.ShapeDtypeStruct((n, t, hw), BF16),
                   jax.ShapeDtypeStruct((n, nh, hd, hd), F32)],
        scratch_shapes=[pltpu.VMEM((nh, hd, hd), F32), pltpu.VMEM((3 * nh, ell + SUBLANES, hd), F32)],
        compiler_params=_params("parallel", "arbitrary"),
        name="gdn_mixer",
    )(x16, ss, proj, proj, proj, proj, conv_state, conv_w, w_ab, w_ab.T,
      a_log.reshape(1, nh), a_log.reshape(nh, 1), dt_bias.reshape(1, nh), dt_bias.reshape(nh, 1),
      s0, o_norm_w.reshape(1, hd))


def _norm_prep_kernel(x_ref, o16_ref, ss_ref):
    x = x_ref[...]
    o16_ref[...] = x.astype(BF16)
    sq = x * x
    part = sq[:, :LANES]
    for c in range(1, sq.shape[1] // LANES):
        part = part + sq[:, c * LANES:(c + 1) * LANES]
    ss_ref[...] = part


def _norm_prep(x):
    m, d = x.shape
    tm = _tile(m, 256)
    x16, ss = pl.pallas_call(
        _norm_prep_kernel,
        grid=(m // tm,),
        in_specs=[pl.BlockSpec((tm, d), lambda i: (i, 0))],
        out_specs=[pl.BlockSpec((tm, d), lambda i: (i, 0)), pl.BlockSpec((tm, LANES), lambda i: (i, 0))],
        out_shape=[jax.ShapeDtypeStruct((m, d), BF16), jax.ShapeDtypeStruct((m, LANES), F32)],
        compiler_params=_params("parallel"),
        name="norm_prep",
    )(x)
    return x, x16, ss


def _with_casts(result, n_casts):
    if not n_casts:
        return result, []
    main = result[:len(result) - n_casts]
    return (main[0] if len(main) == 1 else main), list(result[len(result) - n_casts:])


def _swa_sconv_layer(stream, n, t, pos0, k_cache, v_cache, conv_state, w_in, sinks, conv_w, w_out,
                     side_casts=()):
    x, x16, ss = stream
    ch = conv_w.shape[1]
    aw = w_out.shape[0] - ch
    kvw = (w_in.shape[1] - aw - 3 * ch) // 2
    hd = aw // sinks.shape[0]
    proj, casts = _with_casts(_matmul(x16, w_in, row_ss=ss, side_casts=side_casts, name="in_proj_a"),
                              len(side_casts))
    proj = proj.reshape(n, t, -1)
    cos_t, sin_t = _rope_tables(t, pos0, hd)
    k_new = _rope_k(proj, cos_t, sin_t, aw, kvw, hd)
    v_new = proj[:, t - min(t, WINDOW):, aw + kvw:aw + 2 * kvw]
    attn = functools.partial(_attention, cos_t=cos_t, sin_t=sin_t, sinks=sinks,
                             aw=aw, kvw=kvw, hd=hd, out_width=aw + ch)
    if k_cache is None:
        k_all, v_win = k_new, v_new
        mix = attn(proj, k_new, 0, proj, aw + kvw, window=WINDOW)
    else:
        k_all = jnp.concatenate([k_cache, k_new], axis=1)
        v_all = jnp.concatenate([v_cache, v_new], axis=1)
        v_win = v_all
        mix = attn(proj, k_all, 0, v_all, 0, window=0)
    col_b = aw + 2 * kvw
    mix, new_state = _sconv(proj, conv_state, conv_w, mix, col_b, col_b + ch, col_b + 2 * ch, aw)
    stream = _matmul(mix.reshape(n * t, aw + ch), w_out, res=x, emit_norm=True, name="out_proj_a")
    return stream, k_all[:, -WINDOW:], v_win[:, -WINDOW:], new_state, casts


def _gdn_layer(stream, n, t, conv_state, s0, w_in, w_ab, conv_w, a_log, dt_bias, o_norm_w, w_out, ell,
               side_casts=()):
    x, x16, ss = stream
    d = x.shape[1]
    hw = a_log.shape[0] * o_norm_w.shape[0]
    proj, casts = _with_casts(_matmul(x16, w_in, row_ss=ss, side_casts=side_casts, name="in_proj_g"),
                              len(side_casts))
    proj = proj.reshape(n, t, 4 * hw)
    o, s_new = _gdn_mixer(x16.reshape(n, t, d), ss.reshape(n, t, LANES), proj, conv_state, conv_w, w_ab,
                          a_log, dt_bias, s0, o_norm_w, ell)
    stream = _matmul(o.reshape(n * t, hw), w_out, res=x, emit_norm=True, name="out_proj_g")
    width = conv_w.shape[0]
    return stream, proj[:, t - (width - 1):, :3 * hw], s_new, casts


def _mlp(stream, w_up, w_down, last, side_casts=()):
    x, x16, ss = stream
    hid, casts = _with_casts(_matmul(x16, w_up, act="relu2", row_ss=ss, side_casts=side_casts,
                                     out_dtype=BF16, name="mlp_up"), len(side_casts))
    if isinstance(w_down, CastJob):
        w_down = casts.pop(0)
    if last:
        return (_matmul(hid, w_down, res=x, name="mlp_down"), None, None), w_down, casts
    return _matmul(hid, w_down, res=x, emit_norm=True, name="mlp_down"), w_down, casts


def kernel(x_prompt, x_sample, cache_swa_k, cache_swa_v, state_sconv, state_dn_conv, state_dn,
           attn_norm, w_in_a, sinks, sconv_w, w_out_a,
           dn_norm, w_in_g, dn_conv_w, A_log, dt_bias, o_norm_w, w_out_g,
           mlp_norm, w_up, w_down, final_norm):
    (nb, tp, d), (nd, ts, _) = x_prompt.shape, x_sample.shape
    depth = mlp_norm.shape[0]
    n_kv, hd_a = cache_swa_k.shape[3], cache_swa_k.shape[4]
    nh_g, hd_g = state_dn.shape[2], state_dn.shape[3]
    sp = _norm_prep(x_prompt.reshape(nb * tp, d))
    ss = _norm_prep(x_sample.reshape(nd * ts, d))
    outs = [[] for _ in range(10)]
    hw_g = nh_g * hd_g
    w_in_g_t = jnp.swapaxes(w_in_g, 1, 2)

    def in_proj_job(li):
        if li % 2 == 0:
            return CastJob(w_in_a, li // 2, attn_norm[li // 2])
        return CastJob(w_in_g_t, li // 2, dn_norm[li // 2], True, 4 * hw_g)

    w_in = _run_cast(in_proj_job(0))
    for li in range(depth):
        j = li // 2
        up_job = [CastJob(w_up, li, mlp_norm[li])]
        next_job = [in_proj_job(li + 1)] if li + 1 < depth else []
        if li % 2 == 0:
            w_out = _to_bf16(w_out_a, j)
            zero_state = jnp.zeros((nb,) + state_sconv.shape[2:], F32)
            sp, kp, vp, cp, (wu,) = _swa_sconv_layer(sp, nb, tp, 0, None, None, zero_state, w_in,
                                                     sinks[j], sconv_w[j], w_out, up_job)
            kc = cache_swa_k[j].reshape(nd, -1, n_kv * hd_a)
            vc = cache_swa_v[j].reshape(nd, -1, n_kv * hd_a)
            ss, ks, vs, cs, _ = _swa_sconv_layer(ss, nd, ts, PAST_LEN, kc, vc, state_sconv[j], w_in,
                                                 sinks[j], sconv_w[j], w_out)
            shape5 = lambda a: a.reshape(a.shape[0], a.shape[1], n_kv, hd_a)
            for lst, val in zip(outs[:6], (shape5(kp), shape5(vp), shape5(ks), shape5(vs), cp, cs)):
                lst.append(val)
        else:
            w_ab = _to_bf16(w_in_g_t, j, 4 * hw_g, gain=dn_norm[j], transposed=True)
            w_out = _to_bf16(w_out_g, j)
            zero_conv = jnp.zeros((nb,) + state_dn_conv.shape[2:], F32)
            zero_s = jnp.zeros((nb, nh_g, hd_g, hd_g), F32)
            sp, dcp, dsp, (wu,) = _gdn_layer(sp, nb, tp, zero_conv, zero_s, w_in, w_ab, dn_conv_w[j],
                                             A_log[j], dt_bias[j], o_norm_w[j], w_out, CHUNK, up_job)
            ss, dcs, dss, _ = _gdn_layer(ss, nd, ts, state_dn_conv[j], state_dn[j], w_in, w_ab, dn_conv_w[j],
                                         A_log[j], dt_bias[j], o_norm_w[j], w_out, ts)
            for lst, val in zip(outs[6:], (dcp, dcs, dsp, dss)):
                lst.append(val)
        last = li == depth - 1
        sp, wd, nxt = _mlp(sp, wu, CastJob(w_down, li), last, [CastJob(w_down, li)] + next_job)
        ss, _, _ = _mlp(ss, wu, wd, last)
        w_in = nxt[0] if nxt else None
    y_prompt = _rmsnorm(sp[0], final_norm, F32).reshape(x_prompt.shape)
    y_sample = _rmsnorm(ss[0], final_norm, F32).reshape(x_sample.shape)
    return (y_prompt, y_sample) + tuple(jnp.stack(o, 0) for o in outs)
```

```python
import functools
from typing import NamedTuple, Optional

import jax
import jax.numpy as jnp
from jax import lax
from jax.experimental import pallas as pl
from jax.experimental.pallas import tpu as pltpu

EPS = 1e-6
CHUNK = 64
WINDOW = 128
PAST_LEN = 4096
ROPE_THETA = 500000.0
NEG_INF = -1e30
LANES = 128
SUBLANES = 8
VMEM_LIMIT = 56 * 1024 * 1024
GDN_HEAD_GROUP = 32
CAST_BLOCK_ELEMS = 1024 * 1024
BF16 = jnp.bfloat16
F32 = jnp.float32


def _params(*sem):
    return pltpu.CompilerParams(dimension_semantics=sem, vmem_limit_bytes=VMEM_LIMIT)


def _tile(n, pref, align=SUBLANES):
    if n <= pref:
        return n
    t = pref - pref % align
    while t >= align:
        if n % t == 0:
            return t
        t -= align
    return n


def _rmsnorm_kernel(x_ref, g_ref, o_ref):
    x = x_ref[...]
    ms = jnp.mean(x * x, axis=-1, keepdims=True)
    o_ref[...] = (x * lax.rsqrt(ms + EPS) * g_ref[...]).astype(o_ref.dtype)


def _rmsnorm(x, g, out_dtype):
    m, d = x.shape
    tm = _tile(m, 256)
    return pl.pallas_call(
        _rmsnorm_kernel,
        grid=(m // tm,),
        in_specs=[pl.BlockSpec((tm, d), lambda i: (i, 0)),
                  pl.BlockSpec((1, d), lambda i: (0, 0))],
        out_specs=pl.BlockSpec((tm, d), lambda i: (i, 0)),
        out_shape=jax.ShapeDtypeStruct((m, d), out_dtype),
        compiler_params=_params("parallel"),
        name="rmsnorm",
    )(x, g.reshape(1, d))


def _cast_kernel(*refs, has_gain, transposed):
    w = refs[0][0]
    if has_gain:
        w = w * refs[1][...]
    refs[-1][...] = (w.T if transposed else w).astype(refs[-1].dtype)


def _to_bf16(w, layer, col0=0, n_cols=None, gain=None, transposed=False):
    kdim, n = (w.shape[2], w.shape[1]) if transposed else (w.shape[1], w.shape[2])
    n_cols = n - col0 if n_cols is None else n_cols
    tc = _tile(n_cols, 2048, LANES) if n_cols >= LANES else n_cols
    tr = _tile(kdim, max(512, CAST_BLOCK_ELEMS // tc), LANES)
    assert col0 % tc == 0 and (transposed or tc % LANES == 0)
    if transposed:
        in_specs = [pl.BlockSpec((1, tc, tr), lambda i, j: (layer, col0 // tc + j, i))]
        gain_spec, gain_shape = pl.BlockSpec((1, tr), lambda i, j: (0, i)), (1, kdim)
    else:
        in_specs = [pl.BlockSpec((1, tr, tc), lambda i, j: (layer, i, col0 // tc + j))]
        gain_spec, gain_shape = pl.BlockSpec((tr, 1), lambda i, j: (i, 0)), (kdim, 1)
    args = [w]
    if gain is not None:
        in_specs.append(gain_spec)
        args.append(gain.reshape(gain_shape))
    return pl.pallas_call(
        functools.partial(_cast_kernel, has_gain=gain is not None, transposed=transposed),
        grid=(kdim // tr, n_cols // tc),
        in_specs=in_specs,
        out_specs=pl.BlockSpec((tr, tc), lambda i, j: (i, j)),
        out_shape=jax.ShapeDtypeStruct((kdim, n_cols), BF16),
        compiler_params=_params("parallel", "parallel"),
        name="weight_cast",
    )(*args)


def _row_rms_scale(ss, d):
    return lax.rsqrt(jnp.sum(ss, axis=-1, keepdims=True) * (1.0 / d) + EPS)


def _mm_kernel(*refs, nk, act, has_res, has_ss, emit_norm, d_norm, side):
    refs = list(refs)
    a_ref, w_ref = refs.pop(0), refs.pop(0)
    r_ref = refs.pop(0) if has_res else None
    ssin_ref = refs.pop(0) if has_ss else None
    side_in = [(refs.pop(0), refs.pop(0) if has_gain else None) for has_gain, _ in side]
    o_ref = refs.pop(0)
    o16_ref, ssout_ref = (refs.pop(0), refs.pop(0)) if emit_norm else (None, None)

    for (sw_ref, g_ref), (_, transposed) in zip(side_in, side):
        so_ref = refs.pop(0)
        wv = sw_ref[0] if g_ref is None else sw_ref[0] * g_ref[...]
        so_ref[...] = (wv.T if transposed else wv).astype(so_ref.dtype)

    def finish(acc_of):
        tm = o_ref.shape[0]
        rc = min(tm, 256)
        parts = []
        for r0 in range(0, tm, rc):
            rows = slice(r0, r0 + rc)
            acc = acc_of(rows)
            if has_ss:
                acc = acc * _row_rms_scale(ssin_ref[rows, :], d_norm)
            if act == "relu2":
                acc = jnp.square(jnp.maximum(acc, 0.0))
            if has_res:
                acc = r_ref[rows, :] + acc
            o_ref[rows, :] = acc.astype(o_ref.dtype)
            if emit_norm:
                o16_ref[rows, :] = acc.astype(BF16)
                sq = acc * acc
                part = sq[:, :LANES]
                for c in range(1, sq.shape[1] // LANES):
                    part = part + sq[:, c * LANES:(c + 1) * LANES]
                parts.append(part)
        if emit_norm:
            part = jnp.concatenate(parts, axis=0) if len(parts) > 1 else parts[0]
            j = pl.program_id(1)

            @pl.when(j == 0)
            def _():
                ssout_ref[...] = part

            @pl.when(j > 0)
            def _():
                ssout_ref[...] += part

    if nk == 1:
        full = jnp.dot(a_ref[...], w_ref[...], preferred_element_type=F32)
        finish(lambda rows: full[rows, :])
    else:
        acc_ref = refs.pop(0)
        k = pl.program_id(2)

        @pl.when(k == 0)
        def _():
            acc_ref[...] = jnp.zeros_like(acc_ref)

        acc_ref[...] += jnp.dot(a_ref[...], w_ref[...], preferred_element_type=F32)

        @pl.when(k == nk - 1)
        def _():
            finish(lambda rows: acc_ref[rows, :])


class CastJob(NamedTuple):
    w: jax.Array
    layer: int
    gain: Optional[jax.Array] = None
    transposed: bool = False
    n_cols: Optional[int] = None


def _run_cast(job):
    return _to_bf16(job.w, job.layer, 0, job.n_cols, gain=job.gain, transposed=job.transposed)


SIDE_BLOCK_BYTES = 2 * 1024 * 1024


def _side_plan(job, n_steps):
    if job.transposed:
        n_cols, kdim = job.n_cols or job.w.shape[1], job.w.shape[2]
        n_blk, in_blk, out_blk = n_cols // LANES, (1, LANES, kdim), (kdim, LANES)
        if n_cols % LANES or kdim % LANES:
            return None
    else:
        kdim, n_cols = job.w.shape[1], job.n_cols or job.w.shape[2]
        n_blk = 1
        while n_blk * 2 <= n_steps and kdim % (n_blk * 2 * 16) == 0:
            n_blk *= 2
        in_blk, out_blk = (1, kdim // n_blk, n_cols), (kdim // n_blk, n_cols)
        if n_cols != job.w.shape[2]:
            return None
    if n_blk > n_steps or 4 * in_blk[1] * in_blk[2] > SIDE_BLOCK_BYTES:
        return None
    return n_blk, in_blk, out_blk, (kdim, n_cols)


def _matmul(a, w, *, res=None, act=None, row_ss=None, emit_norm=False, side_casts=(), out_dtype=F32,
            name="matmul"):
    m, kdim = a.shape
    n = w.shape[1]
    tm = _tile(m, 1024)
    tk = _tile(kdim, 4096 if (kdim <= 4096 or tm <= 256) else 2048, LANES)
    nk = kdim // tk
    tn = n if n < LANES else _tile(n, 512 if (emit_norm and nk == 1) else 1024, LANES)
    nj = n // tn
    n_steps = (m // tm) * nj * nk
    row_blk = lambda width: pl.BlockSpec((tm, width), lambda i, j, k: (i, 0))
    out_blk = pl.BlockSpec((tm, tn), lambda i, j, k: (i, j))
    in_specs = [pl.BlockSpec((tm, tk), lambda i, j, k: (i, k)),
                pl.BlockSpec((tk, tn), lambda i, j, k: (k, j))]
    args = [a, w]
    if res is not None:
        in_specs.append(out_blk)
        args.append(res)
    if row_ss is not None:
        in_specs.append(row_blk(LANES))
        args.append(row_ss)
    out_specs, out_shape = [out_blk], [jax.ShapeDtypeStruct((m, n), out_dtype)]
    if emit_norm:
        out_specs += [out_blk, row_blk(LANES)]
        out_shape += [jax.ShapeDtypeStruct((m, n), BF16), jax.ShapeDtypeStruct((m, LANES), F32)]
    plans = [_side_plan(job, n_steps) for job in side_casts]
    side = []
    for job, plan in zip(side_casts, plans):
        if plan is None:
            continue
        n_blk, in_b, out_b, shape = plan
        blk = lambda i, j, k, rep=n_steps // n_blk, last=n_blk - 1: jnp.minimum(((i * nj + j) * nk + k) // rep, last)
        layer = job.layer
        if job.transposed:
            in_specs.append(pl.BlockSpec(in_b, lambda i, j, k, blk=blk, layer=layer: (layer, blk(i, j, k), 0)))
            gain_spec, gain_shape = pl.BlockSpec((1, shape[0]), lambda i, j, k: (0, 0)), (1, shape[0])
            out_specs.append(pl.BlockSpec(out_b, lambda i, j, k, blk=blk: (0, blk(i, j, k))))
        else:
            in_specs.append(pl.BlockSpec(in_b, lambda i, j, k, blk=blk, layer=layer: (layer, blk(i, j, k), 0)))
            gain_spec = pl.BlockSpec((in_b[1], 1), lambda i, j, k, blk=blk: (blk(i, j, k), 0))
            gain_shape = (shape[0], 1)
            out_specs.append(pl.BlockSpec(out_b, lambda i, j, k, blk=blk: (blk(i, j, k), 0)))
        args.append(job.w)
        if job.gain is not None:
            in_specs.append(gain_spec)
            args.append(job.gain.reshape(gain_shape))
        out_shape.append(jax.ShapeDtypeStruct(shape, BF16))
        side.append((job.gain is not None, job.transposed))
    out = pl.pallas_call(
        functools.partial(_mm_kernel, nk=nk, act=act, has_res=res is not None,
                          has_ss=row_ss is not None, emit_norm=emit_norm, d_norm=kdim, side=tuple(side)),
        grid=(m // tm, nj, nk),
        in_specs=in_specs,
        out_specs=out_specs,
        out_shape=out_shape,
        scratch_shapes=[pltpu.VMEM((tm, tn), F32)] if nk > 1 else [],
        compiler_params=_params("arbitrary", "arbitrary", "arbitrary"),
        name=name,
    )(*args)
    out = list(out)
    n_main = 3 if emit_norm else 1
    hosted = iter(out[n_main:])
    casts = [next(hosted) if plan is not None else _run_cast(job) for job, plan in zip(side_casts, plans)]
    result = out[:n_main] + casts
    return result[0] if len(result) == 1 else tuple(result)


def _rope_table_kernel(inv_ref, c_ref, s_ref, *, pos0, hd):
    shape = c_ref.shape
    pos = (pos0 + lax.broadcasted_iota(jnp.int32, shape, 0)).astype(F32)
    ang = pos * inv_ref[...]
    d = jnp.bitwise_and(lax.broadcasted_iota(jnp.int32, shape, 1), hd - 1)
    rot = hd // 4
    cos, sin = jnp.cos(ang), jnp.sin(ang)
    c_ref[...] = jnp.where(d < rot, cos, 1.0)
    s_ref[...] = jnp.where(d < rot // 2, -sin, jnp.where(d < rot, sin, 0.0))


def _rope_tables(n_pos, pos0, hd):
    half = hd // 8
    inv = ROPE_THETA ** (-jnp.arange(half, dtype=F32) / half)
    inv_lane = jnp.tile(jnp.concatenate([inv, inv, jnp.zeros((hd - 2 * half,), F32)]), LANES // hd)
    shp = jax.ShapeDtypeStruct((n_pos, LANES), F32)
    return pl.pallas_call(
        functools.partial(_rope_table_kernel, pos0=pos0, hd=hd),
        out_shape=(shp, shp),
        name="rope_tables",
    )(inv_lane.reshape(1, LANES))


def _rope(x, c, s, hd):
    w = x.shape[1]
    reps = w // LANES
    cf = jnp.concatenate([c] * reps, axis=1) if reps > 1 else c
    sf = jnp.concatenate([s] * reps, axis=1) if reps > 1 else s
    d = jnp.bitwise_and(lax.broadcasted_iota(jnp.int32, x.shape, 1), hd - 1)
    half = hd // 8
    fwd = pltpu.roll(x, w - half, 1)
    bwd = pltpu.roll(x, half, 1)
    sw = jnp.where(d < half, fwd, jnp.where(d < 2 * half, bwd, 0.0))
    return x * cf + sw * sf


def _rope_k_kernel(k_ref, c_ref, s_ref, o_ref, *, hd):
    o_ref[0] = _rope(k_ref[0], c_ref[...], s_ref[...], hd)


def _rope_k(proj, cos_t, sin_t, col0, width, hd):
    n, t, _ = proj.shape
    tb = _tile(t, 512)
    return pl.pallas_call(
        functools.partial(_rope_k_kernel, hd=hd),
        grid=(n, t // tb),
        in_specs=[pl.BlockSpec((1, tb, width), lambda i, j: (i, j, col0 // width)),
                  pl.BlockSpec((tb, LANES), lambda i, j: (j, 0)),
                  pl.BlockSpec((tb, LANES), lambda i, j: (j, 0))],
        out_specs=pl.BlockSpec((1, tb, width), lambda i, j: (i, j, 0)),
        out_shape=jax.ShapeDtypeStruct((n, t, width), F32),
        compiler_params=_params("parallel", "parallel"),
        name="rope_k",
    )(proj, cos_t, sin_t)


def _attn_kernel(sink_ref, q_ref, k_ref, v_ref, c_ref, s_ref, o_ref, *,
                 rows, band, window, n_kv, group, hd):
    tb = q_ref.shape[1]
    gw = group * hd
    each = lambda f, *cols: [f(*x) for x in zip(*cols)]
    sinks = [jnp.concatenate([jnp.full((rows, 1), sink_ref[h * group + g], F32) for g in range(group)],
                             axis=0) for h in range(n_kv)]
    n_groups = tb // rows
    per_pass = 2 if n_groups % 2 == 0 else 1
    for g0 in range(0, n_groups, per_pass):
        units = [(ci, h) for ci in range(g0, g0 + per_pass) for h in range(n_kv)]
        ksl, valid = {}, {}
        for ci in range(g0, g0 + per_pass):
            if window:
                lo = (pl.program_id(1) * n_groups + ci) * rows - window
                start = pl.multiple_of(jnp.maximum(lo, 0), rows)
                kpos = start + lax.broadcasted_iota(jnp.int32, (1, band), 1)
                valid[ci] = jnp.logical_and(kpos >= lo, kpos < lo + band)
                ksl[ci] = pl.ds(start, band)
            else:
                ksl[ci] = slice(0, band)
        qh = [_rope(q_ref[0, ci * rows:(ci + 1) * rows, h * gw:(h + 1) * gw],
                    c_ref[ci * rows:(ci + 1) * rows, :], s_ref[ci * rows:(ci + 1) * rows, :], hd)
              for ci, h in units]
        qs = each(lambda x: jnp.concatenate([x[:, g * hd:(g + 1) * hd] for g in range(group)],
                                            axis=0).astype(BF16), qh)
        kh = [k_ref[0, ksl[ci], h * hd:(h + 1) * hd].astype(BF16) for ci, h in units]
        vh = [v_ref[0, ksl[ci], h * hd:(h + 1) * hd].astype(BF16) for ci, h in units]
        sc = each(lambda a, b: lax.dot_general(a, b, (((1,), (1,)), ((), ())),
                                               preferred_element_type=F32) * (hd ** -0.5), qs, kh)
        if window:
            sc = [jnp.where(valid[ci], x, NEG_INF) for (ci, _), x in zip(units, sc)]
        sk = [sinks[h] for _, h in units]
        m = each(lambda x, s: jnp.maximum(jnp.max(x, axis=-1, keepdims=True), s), sc, sk)
        p = each(lambda x, mx: jnp.exp(x - mx).astype(BF16), sc, m)
        ones = jnp.ones((band, hd), BF16)
        denom = each(lambda x, s, mx: jnp.dot(x, ones, preferred_element_type=F32) + jnp.exp(s - mx), p, sk, m)
        o = each(lambda x, v, d: jnp.dot(x, v, preferred_element_type=F32) / d, p, vh, denom)
        for (ci, h), x in zip(units, o):
            o_ref[0, ci * rows:(ci + 1) * rows, h * gw:(h + 1) * gw] = jnp.concatenate(
                [x[g * rows:(g + 1) * rows, :] for g in range(group)], axis=1).astype(o_ref.dtype)


def _attention(proj, k_src, k_col, v_src, v_col, cos_t, sin_t, sinks, *, aw, kvw, hd, out_width, window):
    n, t, _ = proj.shape
    tk = k_src.shape[1]
    n_kv = kvw // hd
    group = aw // kvw
    if window:
        rows, band = CHUNK, window + CHUNK
        tb = _tile(t, 4 * CHUNK, CHUNK)
    else:
        rows, band, tb = t, tk, t
    return pl.pallas_call(
        functools.partial(_attn_kernel, rows=rows, band=band, window=window,
                          n_kv=n_kv, group=group, hd=hd),
        grid=(n, t // tb),
        in_specs=[pl.BlockSpec(memory_space=pltpu.SMEM),
                  pl.BlockSpec((1, tb, aw), lambda i, j: (i, j, 0)),
                  pl.BlockSpec((1, tk, kvw), lambda i, j: (i, 0, k_col // kvw)),
                  pl.BlockSpec((1, tk, kvw), lambda i, j: (i, 0, v_col // kvw)),
                  pl.BlockSpec((tb, LANES), lambda i, j: (j, 0)),
                  pl.BlockSpec((tb, LANES), lambda i, j: (j, 0))],
        out_specs=pl.BlockSpec((1, tb, aw), lambda i, j: (i, j, 0)),
        out_shape=jax.ShapeDtypeStruct((n, t, out_width), BF16),
        compiler_params=_params("parallel", "parallel"),
        name="swa_attention",
    )(sinks, proj, k_src, v_src, cos_t, sin_t)


def _sconv_kernel(b_ref, c_ref, x_ref, st_ref, w_ref, mix_ref, z_ref, ns_ref, buf):
    del mix_ref
    t = x_ref.shape[1]
    width = w_ref.shape[0]
    buf[SUBLANES:SUBLANES + t, :] = c_ref[0] * x_ref[0]
    buf[SUBLANES - (width - 1):SUBLANES, :] = st_ref[0]
    lo = SUBLANES - (width - 1)
    acc = buf[lo:lo + t, :] * w_ref[0:1, :]
    for i in range(1, width):
        acc = acc + buf[lo + i:lo + i + t, :] * w_ref[i:i + 1, :]
    z_ref[0] = (b_ref[0] * acc).astype(z_ref.dtype)
    ns_ref[0] = buf[SUBLANES + t - (width - 1):SUBLANES + t, :]


def _sconv(proj, state, w, mix, col_b, col_c, col_x, col_out):
    n, t, _ = proj.shape
    width, ch = w.shape
    cols = (ch, col_b, col_c, col_x, col_out)
    cb = next(c for c in (2048, 1024, 512, 256, LANES)
              if all(v % c == 0 for v in cols) and t * c * 4 <= 2 * 1024 * 1024)
    blk = lambda off: pl.BlockSpec((1, t, cb), lambda i, j: (i, 0, off // cb + j))
    return pl.pallas_call(
        _sconv_kernel,
        grid=(n, ch // cb),
        in_specs=[blk(col_b), blk(col_c), blk(col_x),
                  pl.BlockSpec((1, width - 1, cb), lambda i, j: (i, 0, j)),
                  pl.BlockSpec((width, cb), lambda i, j: (0, j)),
                  pl.BlockSpec(memory_space=pl.ANY)],
        out_specs=[blk(col_out),
                   pl.BlockSpec((1, width - 1, cb), lambda i, j: (i, 0, j))],
        out_shape=[jax.ShapeDtypeStruct(mix.shape, mix.dtype),
                   jax.ShapeDtypeStruct((n, width - 1, ch), F32)],
        scratch_shapes=[pltpu.VMEM((t + SUBLANES, cb), F32)],
        input_output_aliases={5: 0},
        compiler_params=_params("parallel", "parallel"),
        name="sconv",
    )(proj, proj, proj, state, w, mix)


def _softplus(x):
    return jnp.maximum(x, 0.0) + jnp.log1p(jnp.exp(-jnp.abs(x)))


def _sigmoid(x):
    return 0.5 * (jnp.tanh(0.5 * x) + 1.0)


def _silu(x):
    h = 0.5 * x
    return h + h * jnp.tanh(h)


def _bdot(a, b):
    return jnp.dot(a.astype(BF16), b.astype(BF16), preferred_element_type=F32)


def _gdn_kernel(x16_ref, ss_ref, q_ref, k_ref, v_ref, gate_ref, cst_ref, cw_ref, wab_ref, wabt_ref,
                al_ref, alt_ref, dt_ref, dtt_ref, s0_ref, ow_ref, o_ref, sout_ref, s_scr, buf, *, group):
    nh = al_ref.shape[1]
    ell, hw = q_ref.shape[1], q_ref.shape[2]
    hd = hw // nh
    width = cw_ref.shape[0]
    lo = SUBLANES - (width - 1)
    cidx = pl.program_id(1)
    each = lambda f, *cols: [f(*x) for x in zip(*cols)]
    nt = (((1,), (1,)), ((), ()))
    tn = (((0,), (0,)), ((), ()))
    dot_nt = lambda a, b: lax.dot_general(a, b, nt, preferred_element_type=F32)
    dot_nn = lambda a, b: jnp.dot(a, b, preferred_element_type=F32)
    to16 = lambda a: a.astype(BF16)

    l2n = lambda y: y * lax.rsqrt(jnp.sum(y * y, axis=-1, keepdims=True) + EPS)

    def conv_silu(c):
        buf[c, SUBLANES:SUBLANES + ell, :] = (q_ref, k_ref, v_ref)[c // nh][0, :, (c % nh) * hd:(c % nh + 1) * hd]
        acc = buf[c, lo:lo + ell, :] * cw_ref[0:1, c * hd:(c + 1) * hd]
        for t in range(1, width):
            acc = acc + buf[c, lo + t:lo + t + ell, :] * cw_ref[t:t + 1, c * hd:(c + 1) * hd]
        buf[c, lo:SUBLANES, :] = buf[c, SUBLANES + ell - (width - 1):SUBLANES + ell, :]
        return _silu(acc)

    @pl.when(cidx == 0)
    def _():
        s_scr[...] = s0_ref[0]
        for c in range(3 * nh):
            buf[c, lo:SUBLANES, :] = cst_ref[0, :, c * hd:(c + 1) * hd]

    i = lax.broadcasted_iota(jnp.int32, (ell, ell), 0)
    j = lax.broadcasted_iota(jnp.int32, (ell, ell), 1)
    hx = x16_ref[0]
    r_col = _row_rms_scale(ss_ref[0], hx.shape[1])
    r_row = jnp.sum(jnp.where(i == j, jnp.broadcast_to(r_col, (ell, ell)), 0.0), axis=0, keepdims=True)
    ab = dot_nn(hx, wab_ref[...]) * r_col
    abt = dot_nt(wabt_ref[...], hx) * r_row
    cum_cols = jnp.dot((i >= j).astype(F32), -jnp.exp(al_ref[...]) * _softplus(ab[:, :nh] + dt_ref[...]),
                       preferred_element_type=F32, precision=lax.Precision.HIGHEST)
    cum_rows = jnp.dot(-jnp.exp(alt_ref[...]) * _softplus(abt[:nh, :] + dtt_ref[...]), (i <= j).astype(F32),
                       preferred_element_type=F32, precision=lax.Precision.HIGHEST)
    beta_cols = _sigmoid(ab[:, nh:])
    beta_rows = _sigmoid(abt[nh:, :])

    eye = (i == j).astype(F32)
    shifts = range(3, ell.bit_length())
    blk = [jnp.right_shift(i, sh) == jnp.right_shift(j, sh) for sh in shifts]

    def run(hs):
        q = [l2n(conv_silu(h)) * (hd ** -0.5) for h in hs]
        k = [l2n(conv_silu(nh + h)) for h in hs]
        v = [conv_silu(2 * nh + h) for h in hs]
        g_row = [cum_rows[h:h + 1, :] for h in hs]
        b_row = [beta_rows[h:h + 1, :] for h in hs]
        g_col = [jnp.broadcast_to(cum_cols[:, h:h + 1], (ell, hd)) for h in hs]
        b_col = [jnp.broadcast_to(beta_cols[:, h:h + 1], (ell, ell)) for h in hs]
        k16 = each(to16, k)
        kk = each(dot_nt, k16, k16)
        qk = each(dot_nt, each(to16, q), k16)
        decay = each(lambda gc, gr: jnp.exp(jnp.where(i >= j, gc[:, :ell] - gr, -jnp.inf)), g_col, g_row)
        a = each(lambda bc, x, d: jnp.where(i > j, bc * x * d, 0.0), b_col, kk, decay)
        p = each(lambda x: -jnp.where(blk[0], x, 0.0), a)
        tm = each(lambda x: eye + x, p)
        for _ in range(2):
            p = each(_bdot, p, p)
            tm = each(lambda t, x: t + _bdot(t, x), tm, p)
        for lvl in range(1, len(blk)):
            ring = jnp.logical_and(blk[lvl], jnp.logical_not(blk[lvl - 1]))
            tl = each(lambda t, x: _bdot(t, jnp.where(ring, x, 0.0)), tm, a)
            tm = each(lambda t, x: t - _bdot(x, t), tm, tl)
        u_base = each(lambda t, b, x: _bdot(t * b, x), tm, b_row, v)
        w16 = each(lambda t, b, g, x: to16(_bdot(t * (b * jnp.exp(g)), x)), tm, b_row, g_row, k16)
        qd16 = each(lambda x, g: to16(x * jnp.exp(g)), q, g_col)
        kd16 = each(lambda x, g: to16(x * jnp.exp(g[ell - 1:ell, :] - g)), k, g_col)
        aqk16 = each(lambda x, d: to16(x * d), qk, decay)
        s = [s_scr[h] for h in hs]
        s16 = each(to16, s)
        u16 = each(lambda ub, w, x: to16(ub - dot_nn(w, x)), u_base, w16, s16)
        o_s = each(dot_nn, qd16, s16)
        o_u = each(dot_nn, aqk16, u16)
        ds = each(lambda kd, x: lax.dot_general(kd, x, tn, preferred_element_type=F32), kd16, u16)
        for r, h in enumerate(hs):
            s_scr[h] = s[r] * jnp.exp(g_col[r][ell - 1:ell, :]) + ds[r]
            o = o_s[r] + o_u[r]
            o = o * lax.rsqrt(jnp.mean(o * o, axis=-1, keepdims=True) + EPS) * ow_ref[...]
            gt = gate_ref[0, :, h * hd:(h + 1) * hd]
            o_ref[0, :, h * hd:(h + 1) * hd] = (o * _silu(gt)).astype(o_ref.dtype)

    for h0 in range(0, nh, group):
        run(range(h0, min(h0 + group, nh)))

    @pl.when(cidx == pl.num_programs(1) - 1)
    def _():
        sout_ref[0] = s_scr[...]


def _gdn_mixer(x16, ss, proj, conv_state, conv_w, w_ab, a_log, dt_bias, s0, o_norm_w, ell):
    n, t, d = x16.shape
    nh, hd = a_log.shape[0], o_norm_w.shape[0]
    hw = nh * hd
    width = conv_w.shape[0]
    col = lambda c: pl.BlockSpec((1, ell, hw), lambda i, j: (i, j, c))
    full = lambda shape: pl.BlockSpec(shape, lambda i, j: (0,) * len(shape))
    st_spec = pl.BlockSpec((1, nh, hd, hd), lambda i, j: (i, 0, 0, 0))
    return pl.pallas_call(
        functools.partial(_gdn_kernel, group=_tile(nh, GDN_HEAD_GROUP, 1)),
        grid=(n, t // ell),
        in_specs=[pl.BlockSpec((1, ell, d), lambda i, j: (i, j, 0)),
                  pl.BlockSpec((1, ell, LANES), lambda i, j: (i, j, 0)), col(0), col(1), col(2), col(3),
                  pl.BlockSpec((1, width - 1, 3 * hw), lambda i, j: (i, 0, 0)), full((width, 3 * hw)),
                  full((d, 2 * nh)), full((2 * nh, d)),
                  full((1, nh)), full((nh, 1)), full((1, nh)), full((nh, 1)),
                  st_spec, full((1, hd))],
        out_specs=[pl.BlockSpec((1, ell, hw), lambda i, j: (i, j, 0)), st_spec],
        out_shape=[jax.ShapeDtypeStruct((n, t, hw), BF16),
                   jax.ShapeDtypeStruct((n, nh, hd, hd), F32)],
        scratch_shapes=[pltpu.VMEM((nh, hd, hd), F32), pltpu.VMEM((3 * nh, ell + SUBLANES, hd), F32)],
        compiler_params=_params("parallel", "arbitrary"),
        name="gdn_mixer",
    )(x16, ss, proj, proj, proj, proj, conv_state, conv_w, w_ab, w_ab.T,
      a_log.reshape(1, nh), a_log.reshape(nh, 1), dt_bias.reshape(1, nh), dt_bias.reshape(nh, 1),
      s0, o_norm_w.reshape(1, hd))


def _norm_prep_kernel(x_ref, o16_ref, ss_ref):
    x = x_ref[...]
    o16_ref[...] = x.astype(BF16)
    sq = x * x
    part = sq[:, :LANES]
    for c in range(1, sq.shape[1] // LANES):
        part = part + sq[:, c * LANES:(c + 1) * LANES]
    ss_ref[...] = part


def _norm_prep(x):
    m, d = x.shape
    tm = _tile(m, 256)
    x16, ss = pl.pallas_call(
        _norm_prep_kernel,
        grid=(m // tm,),
        in_specs=[pl.BlockSpec((tm, d), lambda i: (i, 0))],
        out_specs=[pl.BlockSpec((tm, d), lambda i: (i, 0)), pl.BlockSpec((tm, LANES), lambda i: (i, 0))],
        out_shape=[jax.ShapeDtypeStruct((m, d), BF16), jax.ShapeDtypeStruct((m, LANES), F32)],
        compiler_params=_params("parallel"),
        name="norm_prep",
    )(x)
    return x, x16, ss


def _with_casts(result, n_casts):
    if not n_casts:
        return result, []
    main = result[:len(result) - n_casts]
    return (main[0] if len(main) == 1 else main), list(result[len(result) - n_casts:])


def _swa_sconv_layer(stream, n, t, pos0, k_cache, v_cache, conv_state, w_in, sinks, conv_w, w_out,
                     side_casts=()):
    x, x16, ss = stream
    ch = conv_w.shape[1]
    aw = w_out.shape[0] - ch
    kvw = (w_in.shape[1] - aw - 3 * ch) // 2
    hd = aw // sinks.shape[0]
    proj, casts = _with_casts(_matmul(x16, w_in, row_ss=ss, side_casts=side_casts, name="in_proj_a"),
                              len(side_casts))
    proj = proj.reshape(n, t, -1)
    cos_t, sin_t = _rope_tables(t, pos0, hd)
    k_new = _rope_k(proj, cos_t, sin_t, aw, kvw, hd)
    v_new = proj[:, t - min(t, WINDOW):, aw + kvw:aw + 2 * kvw]
    attn = functools.partial(_attention, cos_t=cos_t, sin_t=sin_t, sinks=sinks,
                             aw=aw, kvw=kvw, hd=hd, out_width=aw + ch)
    if k_cache is None:
        k_all, v_win = k_new, v_new
        mix = attn(proj, k_new, 0, proj, aw + kvw, window=WINDOW)
    else:
        k_all = jnp.concatenate([k_cache, k_new], axis=1)
        v_all = jnp.concatenate([v_cache, v_new], axis=1)
        v_win = v_all
        mix = attn(proj, k_all, 0, v_all, 0, window=0)
    col_b = aw + 2 * kvw
    mix, new_state = _sconv(proj, conv_state, conv_w, mix, col_b, col_b + ch, col_b + 2 * ch, aw)
    stream = _matmul(mix.reshape(n * t, aw + ch), w_out, res=x, emit_norm=True, name="out_proj_a")
    return stream, k_all[:, -WINDOW:], v_win[:, -WINDOW:], new_state, casts


def _gdn_layer(stream, n, t, conv_state, s0, w_in, w_ab, conv_w, a_log, dt_bias, o_norm_w, w_out, ell,
               side_casts=()):
    x, x16, ss = stream
    d = x.shape[1]
    hw = a_log.shape[0] * o_norm_w.shape[0]
    proj, casts = _with_casts(_matmul(x16, w_in, row_ss=ss, side_casts=side_casts, name="in_proj_g"),
                              len(side_casts))
    proj = proj.reshape(n, t, 4 * hw)
    o, s_new = _gdn_mixer(x16.reshape(n, t, d), ss.reshape(n, t, LANES), proj, conv_state, conv_w, w_ab,
                          a_log, dt_bias, s0, o_norm_w, ell)
    stream = _matmul(o.reshape(n * t, hw), w_out, res=x, emit_norm=True, name="out_proj_g")
    width = conv_w.shape[0]
    return stream, proj[:, t - (width - 1):, :3 * hw], s_new, casts


def _mlp(stream, w_up, w_down, last, side_casts=()):
    x, x16, ss = stream
    hid, casts = _with_casts(_matmul(x16, w_up, act="relu2", row_ss=ss, side_casts=side_casts,
                                     out_dtype=BF16, name="mlp_up"), len(side_casts))
    if isinstance(w_down, CastJob):
        w_down = casts.pop(0)
    if last:
        return (_matmul(hid, w_down, res=x, name="mlp_down"), None, None), w_down, casts
    return _matmul(hid, w_down, res=x, emit_norm=True, name="mlp_down"), w_down, casts


def kernel(x_prompt, x_sample, cache_swa_k, cache_swa_v, state_sconv, state_dn_conv, state_dn,
           attn_norm, w_in_a, sinks, sconv_w, w_out_a,
           dn_norm, w_in_g, dn_conv_w, A_log, dt_bias, o_norm_w, w_out_g,
           mlp_norm, w_up, w_down, final_norm):
    (nb, tp, d), (nd, ts, _) = x_prompt.shape, x_sample.shape
    depth = mlp_norm.shape[0]
    n_kv, hd_a = cache_swa_k.shape[3], cache_swa_k.shape[4]
    nh_g, hd_g = state_dn.shape[2], state_dn.shape[3]
    sp = _norm_prep(x_prompt.reshape(nb * tp, d))
    ss = _norm_prep(x_sample.reshape(nd * ts, d))
    outs = [[] for _ in range(10)]
    hw_g = nh_g * hd_g
    w_in_g_t = jnp.swapaxes(w_in_g, 1, 2)

    def in_proj_job(li):
        if li % 2 == 0:
            return CastJob(w_in_a, li // 2, attn_norm[li // 2])
        return CastJob(w_in_g_t, li // 2, dn_norm[li // 2], True, 4 * hw_g)

    w_in = _run_cast(in_proj_job(0))
    for li in range(depth):
        j = li // 2
        up_job = [CastJob(w_up, li, mlp_norm[li])]
        next_job = [in_proj_job(li + 1)] if li + 1 < depth else []
        if li % 2 == 0:
            w_out = _to_bf16(w_out_a, j)
            zero_state = jnp.zeros((nb,) + state_sconv.shape[2:], F32)
            sp, kp, vp, cp, (wu,) = _swa_sconv_layer(sp, nb, tp, 0, None, None, zero_state, w_in,
                                                     sinks[j], sconv_w[j], w_out, up_job)
            kc = cache_swa_k[j].reshape(nd, -1, n_kv * hd_a)
            vc = cache_swa_v[j].reshape(nd, -1, n_kv * hd_a)
            ss, ks, vs, cs, _ = _swa_sconv_layer(ss, nd, ts, PAST_LEN, kc, vc, state_sconv[j], w_in,
                                                 sinks[j], sconv_w[j], w_out)
            shape5 = lambda a: a.reshape(a.shape[0], a.shape[1], n_kv, hd_a)
            for lst, val in zip(outs[:6], (shape5(kp), shape5(vp), shape5(ks), shape5(vs), cp, cs)):
                lst.append(val)
        else:
            w_ab = _to_bf16(w_in_g_t, j, 4 * hw_g, gain=dn_norm[j], transposed=True)
            w_out = _to_bf16(w_out_g, j)
            zero_conv = jnp.zeros((nb,) + state_dn_conv.shape[2:], F32)
            zero_s = jnp.zeros((nb, nh_g, hd_g, hd_g), F32)
            sp, dcp, dsp, (wu,) = _gdn_layer(sp, nb, tp, zero_conv, zero_s, w_in, w_ab, dn_conv_w[j],
                                             A_log[j], dt_bias[j], o_norm_w[j], w_out, CHUNK, up_job)
            ss, dcs, dss, _ = _gdn_layer(ss, nd, ts, state_dn_conv[j], state_dn[j], w_in, w_ab, dn_conv_w[j],
                                         A_log[j], dt_bias[j], o_norm_w[j], w_out, ts)
            for lst, val in zip(outs[6:], (dcp, dcs, dsp, dss)):
                lst.append(val)
        last = li == depth - 1
        sp, wd, nxt = _mlp(sp, wu, CastJob(w_down, li), last, [CastJob(w_down, li)] + next_job)
        ss, _, _ = _mlp(ss, wu, wd, last)
        w_in = nxt[0] if nxt else None
    y_prompt = _rmsnorm(sp[0], final_norm, F32).reshape(x_prompt.shape)
    y_sample = _rmsnorm(ss[0], final_norm, F32).reshape(x_sample.shape)
    return (y_prompt, y_sample) + tuple(jnp.stack(o, 0) for o in outs)
```

```python
import functools
from typing import NamedTuple, Optional

import jax
import jax.numpy as jnp
from jax import lax
from jax.experimental import pallas as pl
from jax.experimental.pallas import tpu as pltpu

EPS = 1e-6
CHUNK = 64
WINDOW = 128
PAST_LEN = 4096
ROPE_THETA = 500000.0
NEG_INF = -1e30
LANES = 128
SUBLANES = 8
VMEM_LIMIT = 56 * 1024 * 1024
GDN_HEAD_GROUP = 32
GDN_BLOCKS_PER_STEP = 2
CAST_BLOCK_ELEMS = 1024 * 1024
BF16 = jnp.bfloat16
F32 = jnp.float32


def _params(*sem):
    return pltpu.CompilerParams(dimension_semantics=sem, vmem_limit_bytes=VMEM_LIMIT)


def _tile(n, pref, align=SUBLANES):
    if n <= pref:
        return n
    t = pref - pref % align
    while t >= align:
        if n % t == 0:
            return t
        t -= align
    return n


def _rmsnorm_kernel(x_ref, g_ref, o_ref):
    x = x_ref[...]
    ms = jnp.mean(x * x, axis=-1, keepdims=True)
    o_ref[...] = (x * lax.rsqrt(ms + EPS) * g_ref[...]).astype(o_ref.dtype)


def _rmsnorm(x, g, out_dtype):
    m, d = x.shape
    tm = _tile(m, 256)
    return pl.pallas_call(
        _rmsnorm_kernel,
        grid=(m // tm,),
        in_specs=[pl.BlockSpec((tm, d), lambda i: (i, 0)),
                  pl.BlockSpec((1, d), lambda i: (0, 0))],
        out_specs=pl.BlockSpec((tm, d), lambda i: (i, 0)),
        out_shape=jax.ShapeDtypeStruct((m, d), out_dtype),
        compiler_params=_params("parallel"),
        name="rmsnorm",
    )(x, g.reshape(1, d))


def _cast_kernel(*refs, has_gain, transposed):
    w = refs[0][0]
    if has_gain:
        w = w * refs[1][...]
    refs[-1][...] = (w.T if transposed else w).astype(refs[-1].dtype)


def _to_bf16(w, layer, col0=0, n_cols=None, gain=None, transposed=False):
    kdim, n = (w.shape[2], w.shape[1]) if transposed else (w.shape[1], w.shape[2])
    n_cols = n - col0 if n_cols is None else n_cols
    tc = _tile(n_cols, 2048, LANES) if n_cols >= LANES else n_cols
    tr = _tile(kdim, max(512, CAST_BLOCK_ELEMS // tc), LANES)
    assert col0 % tc == 0 and (transposed or tc % LANES == 0)
    if transposed:
        in_specs = [pl.BlockSpec((1, tc, tr), lambda i, j: (layer, col0 // tc + j, i))]
        gain_spec, gain_shape = pl.BlockSpec((1, tr), lambda i, j: (0, i)), (1, kdim)
    else:
        in_specs = [pl.BlockSpec((1, tr, tc), lambda i, j: (layer, i, col0 // tc + j))]
        gain_spec, gain_shape = pl.BlockSpec((tr, 1), lambda i, j: (i, 0)), (kdim, 1)
    args = [w]
    if gain is not None:
        in_specs.append(gain_spec)
        args.append(gain.reshape(gain_shape))
    return pl.pallas_call(
        functools.partial(_cast_kernel, has_gain=gain is not None, transposed=transposed),
        grid=(kdim // tr, n_cols // tc),
        in_specs=in_specs,
        out_specs=pl.BlockSpec((tr, tc), lambda i, j: (i, j)),
        out_shape=jax.ShapeDtypeStruct((kdim, n_cols), BF16),
        compiler_params=_params("parallel", "parallel"),
        name="weight_cast",
    )(*args)


def _row_rms_scale(ss, d):
    return lax.rsqrt(jnp.sum(ss, axis=-1, keepdims=True) * (1.0 / d) + EPS)


def _mm_kernel(*refs, nk, act, has_res, has_ss, emit_norm, d_norm, side):
    refs = list(refs)
    a_ref, w_ref = refs.pop(0), refs.pop(0)
    r_ref = refs.pop(0) if has_res else None
    ssin_ref = refs.pop(0) if has_ss else None
    side_in = [(refs.pop(0), refs.pop(0) if has_gain else None) for has_gain, _ in side]
    o_ref = refs.pop(0)
    o16_ref, ssout_ref = (refs.pop(0), refs.pop(0)) if emit_norm else (None, None)

    for (sw_ref, g_ref), (_, transposed) in zip(side_in, side):
        so_ref = refs.pop(0)
        wv = sw_ref[0] if g_ref is None else sw_ref[0] * g_ref[...]
        so_ref[...] = (wv.T if transposed else wv).astype(so_ref.dtype)

    def finish(acc_of):
        tm = o_ref.shape[0]
        rc = min(tm, 256)
        parts = []
        for r0 in range(0, tm, rc):
            rows = slice(r0, r0 + rc)
            acc = acc_of(rows)
            if has_ss:
                acc = acc * _row_rms_scale(ssin_ref[rows, :], d_norm)
            if act == "relu2":
                acc = jnp.square(jnp.maximum(acc, 0.0))
            if has_res:
                acc = r_ref[rows, :] + acc
            o_ref[rows, :] = acc.astype(o_ref.dtype)
            if emit_norm:
                o16_ref[rows, :] = acc.astype(BF16)
                sq = acc * acc
                part = sq[:, :LANES]
                for c in range(1, sq.shape[1] // LANES):
                    part = part + sq[:, c * LANES:(c + 1) * LANES]
                parts.append(part)
        if emit_norm:
            part = jnp.concatenate(parts, axis=0) if len(parts) > 1 else parts[0]
            j = pl.program_id(1)

            @pl.when(j == 0)
            def _():
                ssout_ref[...] = part

            @pl.when(j > 0)
            def _():
                ssout_ref[...] += part

    if nk == 1:
        full = jnp.dot(a_ref[...], w_ref[...], preferred_element_type=F32)
        finish(lambda rows: full[rows, :])
    else:
        acc_ref = refs.pop(0)
        k = pl.program_id(2)

        @pl.when(k == 0)
        def _():
            acc_ref[...] = jnp.zeros_like(acc_ref)

        acc_ref[...] += jnp.dot(a_ref[...], w_ref[...], preferred_element_type=F32)

        @pl.when(k == nk - 1)
        def _():
            finish(lambda rows: acc_ref[rows, :])


class CastJob(NamedTuple):
    w: jax.Array
    layer: int
    gain: Optional[jax.Array] = None
    transposed: bool = False
    n_cols: Optional[int] = None


def _run_cast(job):
    return _to_bf16(job.w, job.layer, 0, job.n_cols, gain=job.gain, transposed=job.transposed)


SIDE_BLOCK_BYTES = 2 * 1024 * 1024


def _side_plan(job, n_steps):
    if job.transposed:
        n_cols, kdim = job.n_cols or job.w.shape[1], job.w.shape[2]
        n_blk, in_blk, out_blk = n_cols // LANES, (1, LANES, kdim), (kdim, LANES)
        if n_cols % LANES or kdim % LANES:
            return None
    else:
        kdim, n_cols = job.w.shape[1], job.n_cols or job.w.shape[2]
        n_blk = 1
        while n_blk * 2 <= n_steps and kdim % (n_blk * 2 * 16) == 0:
            n_blk *= 2
        in_blk, out_blk = (1, kdim // n_blk, n_cols), (kdim // n_blk, n_cols)
        if n_cols != job.w.shape[2]:
            return None
    if n_blk > n_steps or 4 * in_blk[1] * in_blk[2] > SIDE_BLOCK_BYTES:
        return None
    return n_blk, in_blk, out_blk, (kdim, n_cols)


def _matmul(a, w, *, res=None, act=None, row_ss=None, emit_norm=False, side_casts=(), out_dtype=F32,
            name="matmul"):
    m, kdim = a.shape
    n = w.shape[1]
    tm = _tile(m, 1024)
    tk = _tile(kdim, 4096 if (kdim <= 4096 or tm <= 256) else 2048, LANES)
    nk = kdim // tk
    tn = n if n < LANES else _tile(n, 512 if (emit_norm and nk == 1) else 1024, LANES)
    nj = n // tn
    n_steps = (m // tm) * nj * nk
    row_blk = lambda width: pl.BlockSpec((tm, width), lambda i, j, k: (i, 0))
    out_blk = pl.BlockSpec((tm, tn), lambda i, j, k: (i, j))
    in_specs = [pl.BlockSpec((tm, tk), lambda i, j, k: (i, k)),
                pl.BlockSpec((tk, tn), lambda i, j, k: (k, j))]
    args = [a, w]
    if res is not None:
        in_specs.append(out_blk)
        args.append(res)
    if row_ss is not None:
        in_specs.append(row_blk(LANES))
        args.append(row_ss)
    out_specs, out_shape = [out_blk], [jax.ShapeDtypeStruct((m, n), out_dtype)]
    if emit_norm:
        out_specs += [out_blk, row_blk(LANES)]
        out_shape += [jax.ShapeDtypeStruct((m, n), BF16), jax.ShapeDtypeStruct((m, LANES), F32)]
    plans = [_side_plan(job, n_steps) for job in side_casts]
    side = []
    for job, plan in zip(side_casts, plans):
        if plan is None:
            continue
        n_blk, in_b, out_b, shape = plan
        blk = lambda i, j, k, rep=n_steps // n_blk, last=n_blk - 1: jnp.minimum(((i * nj + j) * nk + k) // rep, last)
        layer = job.layer
        if job.transposed:
            in_specs.append(pl.BlockSpec(in_b, lambda i, j, k, blk=blk, layer=layer: (layer, blk(i, j, k), 0)))
            gain_spec, gain_shape = pl.BlockSpec((1, shape[0]), lambda i, j, k: (0, 0)), (1, shape[0])
            out_specs.append(pl.BlockSpec(out_b, lambda i, j, k, blk=blk: (0, blk(i, j, k))))
        else:
            in_specs.append(pl.BlockSpec(in_b, lambda i, j, k, blk=blk, layer=layer: (layer, blk(i, j, k), 0)))
            gain_spec = pl.BlockSpec((in_b[1], 1), lambda i, j, k, blk=blk: (blk(i, j, k), 0))
            gain_shape = (shape[0], 1)
            out_specs.append(pl.BlockSpec(out_b, lambda i, j, k, blk=blk: (blk(i, j, k), 0)))
        args.append(job.w)
        if job.gain is not None:
            in_specs.append(gain_spec)
            args.append(job.gain.reshape(gain_shape))
        out_shape.append(jax.ShapeDtypeStruct(shape, BF16))
        side.append((job.gain is not None, job.transposed))
    out = pl.pallas_call(
        functools.partial(_mm_kernel, nk=nk, act=act, has_res=res is not None,
                          has_ss=row_ss is not None, emit_norm=emit_norm, d_norm=kdim, side=tuple(side)),
        grid=(m // tm, nj, nk),
        in_specs=in_specs,
        out_specs=out_specs,
        out_shape=out_shape,
        scratch_shapes=[pltpu.VMEM((tm, tn), F32)] if nk > 1 else [],
        compiler_params=_params("arbitrary", "arbitrary", "arbitrary"),
        name=name,
    )(*args)
    out = list(out)
    n_main = 3 if emit_norm else 1
    hosted = iter(out[n_main:])
    casts = [next(hosted) if plan is not None else _run_cast(job) for job, plan in zip(side_casts, plans)]
    result = out[:n_main] + casts
    return result[0] if len(result) == 1 else tuple(result)


def _rope_table_kernel(inv_ref, c_ref, s_ref, *, pos0, hd):
    shape = c_ref.shape
    pos = (pos0 + lax.broadcasted_iota(jnp.int32, shape, 0)).astype(F32)
    ang = pos * inv_ref[...]
    d = jnp.bitwise_and(lax.broadcasted_iota(jnp.int32, shape, 1), hd - 1)
    rot = hd // 4
    cos, sin = jnp.cos(ang), jnp.sin(ang)
    c_ref[...] = jnp.where(d < rot, cos, 1.0)
    s_ref[...] = jnp.where(d < rot // 2, -sin, jnp.where(d < rot, sin, 0.0))


def _rope_tables(n_pos, pos0, hd):
    half = hd // 8
    inv = ROPE_THETA ** (-jnp.arange(half, dtype=F32) / half)
    inv_lane = jnp.tile(jnp.concatenate([inv, inv, jnp.zeros((hd - 2 * half,), F32)]), LANES // hd)
    shp = jax.ShapeDtypeStruct((n_pos, LANES), F32)
    return pl.pallas_call(
        functools.partial(_rope_table_kernel, pos0=pos0, hd=hd),
        out_shape=(shp, shp),
        name="rope_tables",
    )(inv_lane.reshape(1, LANES))


def _rope(x, c, s, hd):
    w = x.shape[1]
    reps = w // LANES
    cf = jnp.concatenate([c] * reps, axis=1) if reps > 1 else c
    sf = jnp.concatenate([s] * reps, axis=1) if reps > 1 else s
    d = jnp.bitwise_and(lax.broadcasted_iota(jnp.int32, x.shape, 1), hd - 1)
    half = hd // 8
    fwd = pltpu.roll(x, w - half, 1)
    bwd = pltpu.roll(x, half, 1)
    sw = jnp.where(d < half, fwd, jnp.where(d < 2 * half, bwd, 0.0))
    return x * cf + sw * sf


def _rope_k_kernel(k_ref, c_ref, s_ref, o_ref, *, hd):
    o_ref[0] = _rope(k_ref[0], c_ref[...], s_ref[...], hd)


def _rope_k(proj, cos_t, sin_t, col0, width, hd):
    n, t, _ = proj.shape
    tb = _tile(t, 512)
    return pl.pallas_call(
        functools.partial(_rope_k_kernel, hd=hd),
        grid=(n, t // tb),
        in_specs=[pl.BlockSpec((1, tb, width), lambda i, j: (i, j, col0 // width)),
                  pl.BlockSpec((tb, LANES), lambda i, j: (j, 0)),
                  pl.BlockSpec((tb, LANES), lambda i, j: (j, 0))],
        out_specs=pl.BlockSpec((1, tb, width), lambda i, j: (i, j, 0)),
        out_shape=jax.ShapeDtypeStruct((n, t, width), F32),
        compiler_params=_params("parallel", "parallel"),
        name="rope_k",
    )(proj, cos_t, sin_t)


def _attn_kernel(sink_ref, q_ref, k_ref, v_ref, c_ref, s_ref, o_ref, *,
                 rows, band, window, n_kv, group, hd):
    tb = q_ref.shape[1]
    gw = group * hd
    each = lambda f, *cols: [f(*x) for x in zip(*cols)]
    sinks = [jnp.concatenate([jnp.full((rows, 1), sink_ref[h * group + g], F32) for g in range(group)],
                             axis=0) for h in range(n_kv)]
    n_groups = tb // rows
    per_pass = 2 if n_groups % 2 == 0 else 1
    for g0 in range(0, n_groups, per_pass):
        units = [(ci, h) for ci in range(g0, g0 + per_pass) for h in range(n_kv)]
        ksl, valid = {}, {}
        for ci in range(g0, g0 + per_pass):
            if window:
                lo = (pl.program_id(1) * n_groups + ci) * rows - window
                start = pl.multiple_of(jnp.maximum(lo, 0), rows)
                kpos = start + lax.broadcasted_iota(jnp.int32, (1, band), 1)
                valid[ci] = jnp.logical_and(kpos >= lo, kpos < lo + band)
                ksl[ci] = pl.ds(start, band)
            else:
                ksl[ci] = slice(0, band)
        qh = [_rope(q_ref[0, ci * rows:(ci + 1) * rows, h * gw:(h + 1) * gw],
                    c_ref[ci * rows:(ci + 1) * rows, :], s_ref[ci * rows:(ci + 1) * rows, :], hd)
              for ci, h in units]
        qs = each(lambda x: jnp.concatenate([x[:, g * hd:(g + 1) * hd] for g in range(group)],
                                            axis=0).astype(BF16), qh)
        kh = [k_ref[0, ksl[ci], h * hd:(h + 1) * hd].astype(BF16) for ci, h in units]
        vh = [v_ref[0, ksl[ci], h * hd:(h + 1) * hd].astype(BF16) for ci, h in units]
        sc = each(lambda a, b: lax.dot_general(a, b, (((1,), (1,)), ((), ())),
                                               preferred_element_type=F32) * (hd ** -0.5), qs, kh)
        if window:
            sc = [jnp.where(valid[ci], x, NEG_INF) for (ci, _), x in zip(units, sc)]
        sk = [sinks[h] for _, h in units]
        m = each(lambda x, s: jnp.maximum(jnp.max(x, axis=-1, keepdims=True), s), sc, sk)
        p = each(lambda x, mx: jnp.exp(x - mx).astype(BF16), sc, m)
        ones = jnp.ones((band, hd), BF16)
        denom = each(lambda x, s, mx: jnp.dot(x, ones, preferred_element_type=F32) + jnp.exp(s - mx), p, sk, m)
        o = each(lambda x, v, d: jnp.dot(x, v, preferred_element_type=F32) / d, p, vh, denom)
        for (ci, h), x in zip(units, o):
            o_ref[0, ci * rows:(ci + 1) * rows, h * gw:(h + 1) * gw] = jnp.concatenate(
                [x[g * rows:(g + 1) * rows, :] for g in range(group)], axis=1).astype(o_ref.dtype)


def _attention(proj, k_src, k_col, v_src, v_col, cos_t, sin_t, sinks, *, aw, kvw, hd, out_width, window):
    n, t, _ = proj.shape
    tk = k_src.shape[1]
    n_kv = kvw // hd
    group = aw // kvw
    if window:
        rows, band = CHUNK, window + CHUNK
        tb = _tile(t, 4 * CHUNK, CHUNK)
    else:
        rows, band, tb = t, tk, t
    return pl.pallas_call(
        functools.partial(_attn_kernel, rows=rows, band=band, window=window,
                          n_kv=n_kv, group=group, hd=hd),
        grid=(n, t // tb),
        in_specs=[pl.BlockSpec(memory_space=pltpu.SMEM),
                  pl.BlockSpec((1, tb, aw), lambda i, j: (i, j, 0)),
                  pl.BlockSpec((1, tk, kvw), lambda i, j: (i, 0, k_col // kvw)),
                  pl.BlockSpec((1, tk, kvw), lambda i, j: (i, 0, v_col // kvw)),
                  pl.BlockSpec((tb, LANES), lambda i, j: (j, 0)),
                  pl.BlockSpec((tb, LANES), lambda i, j: (j, 0))],
        out_specs=pl.BlockSpec((1, tb, aw), lambda i, j: (i, j, 0)),
        out_shape=jax.ShapeDtypeStruct((n, t, out_width), BF16),
        compiler_params=_params("parallel", "parallel"),
        name="swa_attention",
    )(sinks, proj, k_src, v_src, cos_t, sin_t)


def _sconv_kernel(b_ref, c_ref, x_ref, st_ref, w_ref, mix_ref, z_ref, ns_ref, buf):
    del mix_ref
    t = x_ref.shape[1]
    width = w_ref.shape[0]
    buf[SUBLANES:SUBLANES + t, :] = c_ref[0] * x_ref[0]
    buf[SUBLANES - (width - 1):SUBLANES, :] = st_ref[0]
    lo = SUBLANES - (width - 1)
    acc = buf[lo:lo + t, :] * w_ref[0:1, :]
    for i in range(1, width):
        acc = acc + buf[lo + i:lo + i + t, :] * w_ref[i:i + 1, :]
    z_ref[0] = (b_ref[0] * acc).astype(z_ref.dtype)
    ns_ref[0] = buf[SUBLANES + t - (width - 1):SUBLANES + t, :]


def _sconv(proj, state, w, mix, col_b, col_c, col_x, col_out):
    n, t, _ = proj.shape
    width, ch = w.shape
    cols = (ch, col_b, col_c, col_x, col_out)
    cb = next(c for c in (2048, 1024, 512, 256, LANES)
              if all(v % c == 0 for v in cols) and t * c * 4 <= 2 * 1024 * 1024)
    blk = lambda off: pl.BlockSpec((1, t, cb), lambda i, j: (i, 0, off // cb + j))
    return pl.pallas_call(
        _sconv_kernel,
        grid=(n, ch // cb),
        in_specs=[blk(col_b), blk(col_c), blk(col_x),
                  pl.BlockSpec((1, width - 1, cb), lambda i, j: (i, 0, j)),
                  pl.BlockSpec((width, cb), lambda i, j: (0, j)),
                  pl.BlockSpec(memory_space=pl.ANY)],
        out_specs=[blk(col_out),
                   pl.BlockSpec((1, width - 1, cb), lambda i, j: (i, 0, j))],
        out_shape=[jax.ShapeDtypeStruct(mix.shape, mix.dtype),
                   jax.ShapeDtypeStruct((n, width - 1, ch), F32)],
        scratch_shapes=[pltpu.VMEM((t + SUBLANES, cb), F32)],
        input_output_aliases={5: 0},
        compiler_params=_params("parallel", "parallel"),
        name="sconv",
    )(proj, proj, proj, state, w, mix)


def _softplus(x):
    return jnp.maximum(x, 0.0) + jnp.log1p(jnp.exp(-jnp.abs(x)))


def _sigmoid(x):
    return 0.5 * (jnp.tanh(0.5 * x) + 1.0)


def _silu(x):
    h = 0.5 * x
    return h + h * jnp.tanh(h)


def _bdot(a, b):
    return jnp.dot(a.astype(BF16), b.astype(BF16), preferred_element_type=F32)


def _gdn_kernel(x16_ref, ss_ref, q_ref, k_ref, v_ref, gate_ref, cst_ref, cw_ref, wab_ref, wabt_ref,
                al_ref, alt_ref, dt_ref, dtt_ref, s0_ref, ow_ref, o_ref, sout_ref, s_scr, buf, *, ell, group):
    nh = al_ref.shape[1]
    rows, hw = q_ref.shape[1], q_ref.shape[2]
    blocks = range(rows // ell)
    sub = lambda x, u: x[u * ell:(u + 1) * ell]
    hd = hw // nh
    width = cw_ref.shape[0]
    lo = SUBLANES - (width - 1)
    cidx = pl.program_id(1)
    each = lambda f, *cols: [f(*x) for x in zip(*cols)]
    nt = (((1,), (1,)), ((), ()))
    tn = (((0,), (0,)), ((), ()))
    dot_nt = lambda a, b: lax.dot_general(a, b, nt, preferred_element_type=F32)
    dot_nn = lambda a, b: jnp.dot(a, b, preferred_element_type=F32)
    to16 = lambda a: a.astype(BF16)

    l2n = lambda y: y * lax.rsqrt(jnp.sum(y * y, axis=-1, keepdims=True) + EPS)

    def conv_silu(c):
        buf[c, SUBLANES:SUBLANES + rows, :] = (q_ref, k_ref, v_ref)[c // nh][0, :, (c % nh) * hd:(c % nh + 1) * hd]
        acc = buf[c, lo:lo + rows, :] * cw_ref[0:1, c * hd:(c + 1) * hd]
        for t in range(1, width):
            acc = acc + buf[c, lo + t:lo + t + rows, :] * cw_ref[t:t + 1, c * hd:(c + 1) * hd]
        buf[c, lo:SUBLANES, :] = buf[c, SUBLANES + rows - (width - 1):SUBLANES + rows, :]
        return _silu(acc)

    @pl.when(cidx == 0)
    def _():
        s_scr[...] = s0_ref[0]
        for c in range(3 * nh):
            buf[c, lo:SUBLANES, :] = cst_ref[0, :, c * hd:(c + 1) * hd]

    i = lax.broadcasted_iota(jnp.int32, (ell, ell), 0)
    j = lax.broadcasted_iota(jnp.int32, (ell, ell), 1)
    hx = x16_ref[0]
    r_all = _row_rms_scale(ss_ref[0], hx.shape[1])
    ab_all = dot_nn(hx, wab_ref[...]) * r_all
    abt_all = dot_nt(wabt_ref[...], hx)
    cum_cols, cum_rows, beta_cols, beta_rows = [], [], [], []
    for u in blocks:
        r_row = jnp.sum(jnp.where(i == j, jnp.broadcast_to(sub(r_all, u), (ell, ell)), 0.0),
                        axis=0, keepdims=True)
        ab = sub(ab_all, u)
        abt = abt_all[:, u * ell:(u + 1) * ell] * r_row
        cum_cols.append(jnp.dot((i >= j).astype(F32),
                                -jnp.exp(al_ref[...]) * _softplus(ab[:, :nh] + dt_ref[...]),
                                preferred_element_type=F32, precision=lax.Precision.HIGHEST))
        cum_rows.append(jnp.dot(-jnp.exp(alt_ref[...]) * _softplus(abt[:nh, :] + dtt_ref[...]),
                                (i <= j).astype(F32),
                                preferred_element_type=F32, precision=lax.Precision.HIGHEST))
        beta_cols.append(_sigmoid(ab[:, nh:]))
        beta_rows.append(_sigmoid(abt[nh:, :]))

    pack = LANES // ell
    pw = pack * ell
    i = lax.broadcasted_iota(jnp.int32, (ell, pw), 0)
    lane = lax.broadcasted_iota(jnp.int32, (ell, pw), 1)
    j = jnp.bitwise_and(lane, ell - 1)
    slot = jnp.right_shift(lane, ell.bit_length() - 1)
    eye = (i == j).astype(F32)
    shifts = range(3, ell.bit_length())
    blk = [jnp.right_shift(i, sh) == jnp.right_shift(j, sh) for sh in shifts]
    cat = lambda xs, axis: xs[0] if len(xs) == 1 else jnp.concatenate(xs, axis=axis)

    def by_slot(xs):
        out = xs[-1]
        for s in range(pack - 2, -1, -1):
            out = jnp.where(slot == s, xs[s], out)
        return out

    def bdiag16(x):
        return to16(cat([jnp.where(slot == s, x, 0.0) for s in range(pack)], 0))

    def bdiag_wide16(xs):
        zero = jnp.zeros((ell, hd), BF16)
        return cat([cat([to16(x) if c == s else zero for c in range(pack)], 1) for s, x in enumerate(xs)], 0)

    pdot = lambda x, y16: jnp.dot(to16(x), y16, preferred_element_type=F32)

    def run(hs):
        items = [(u, h) for u in blocks for h in hs]
        per_unit = lambda xs: [xs[t:t + pack] for t in range(0, len(xs), pack)]
        units = per_unit(items)
        q_all = [l2n(conv_silu(h)) * (hd ** -0.5) for h in hs]
        k_all = [l2n(conv_silu(nh + h)) for h in hs]
        v_all = [conv_silu(2 * nh + h) for h in hs]
        q = [sub(x, u) for u in blocks for x in q_all]
        k = [sub(x, u) for u in blocks for x in k_all]
        v = [sub(x, u) for u in blocks for x in v_all]
        g_col = [jnp.broadcast_to(cum_cols[u][:, h:h + 1], (ell, hd)) for u, h in items]
        g_row = [cat([cum_rows[u][h:h + 1, :] for u, h in un], 1) for un in units]
        b_row = [cat([beta_rows[u][h:h + 1, :] for u, h in un], 1) for un in units]
        g_colp = [by_slot([x[:, :pw] for x in xs]) for xs in per_unit(g_col)]
        b_colp = [by_slot([jnp.broadcast_to(beta_cols[u][:, h:h + 1], (ell, pw)) for u, h in un])
                  for un in units]
        k_wide = each(bdiag_wide16, per_unit(k))
        k_cat = [to16(cat(xs, 1)) for xs in per_unit(k)]
        q_cat = [to16(cat(xs, 1)) for xs in per_unit(q)]
        kk = each(dot_nt, k_cat, k_wide)
        qk = each(dot_nt, q_cat, k_wide)
        decay = each(lambda gc, gr: jnp.exp(jnp.where(i >= j, gc - gr, -jnp.inf)), g_colp, g_row)
        a = each(lambda bc, x, d: jnp.where(i > j, bc * x * d, 0.0), b_colp, kk, decay)
        p = each(lambda x: -jnp.where(blk[0], x, 0.0), a)
        tm = each(lambda x: eye + x, p)
        for _ in range(2):
            p = each(pdot, p, each(bdiag16, p))
            tm = each(lambda t, x: t + pdot(t, bdiag16(x)), tm, p)
        for lvl in range(1, len(blk)):
            ring = jnp.logical_and(blk[lvl], jnp.logical_not(blk[lvl - 1]))
            tl = each(lambda t, x: pdot(t, bdiag16(jnp.where(ring, x, 0.0))), tm, a)
            tm = each(lambda t, x: t - pdot(x, bdiag16(t)), tm, tl)
        ub_cat = each(lambda t, b, xs: pdot(t * b, bdiag_wide16(xs)), tm, b_row, per_unit(v))
        w_cat = each(lambda t, b, g, kw: pdot(t * (b * jnp.exp(g)), kw), tm, b_row, g_row, k_wide)
        aqk16 = each(lambda x, d: to16(x * d), qk, decay)
        split = lambda cats: [x[:, s * hd:(s + 1) * hd] for x in cats for s in range(pack)]
        u_base, w16 = split(ub_cat), each(to16, split(w_cat))
        qd16 = each(lambda x, g: to16(x * jnp.exp(g)), q, g_col)
        kd16 = each(lambda x, g: to16(x * jnp.exp(g[ell - 1:ell, :] - g)), k, g_col)
        s = [s_scr[h] for h in hs]
        n_it, n_un = len(hs), len(hs) // pack
        for u in blocks:
            at = lambda xs: xs[u * n_it:(u + 1) * n_it]
            s16 = each(to16, s)
            u16 = each(lambda ub, w, x: to16(ub - dot_nn(w, x)), at(u_base), at(w16), s16)
            o_s = each(dot_nn, at(qd16), s16)
            o_u = split(each(lambda x, us: jnp.dot(x, bdiag_wide16(us), preferred_element_type=F32),
                             aqk16[u * n_un:(u + 1) * n_un], per_unit(u16)))
            ds = each(lambda kd, x: lax.dot_general(kd, x, tn, preferred_element_type=F32), at(kd16), u16)
            s = each(lambda x, g, d: x * jnp.exp(g[ell - 1:ell, :]) + d, s, at(g_col), ds)
            for r, h in enumerate(hs):
                o = o_s[r] + o_u[r]
                o = o * lax.rsqrt(jnp.mean(o * o, axis=-1, keepdims=True) + EPS) * ow_ref[...]
                gt = gate_ref[0, u * ell:(u + 1) * ell, h * hd:(h + 1) * hd]
                o_ref[0, u * ell:(u + 1) * ell, h * hd:(h + 1) * hd] = (o * _silu(gt)).astype(o_ref.dtype)
        for r, h in enumerate(hs):
            s_scr[h] = s[r]

    for h0 in range(0, nh, group):
        run(range(h0, min(h0 + group, nh)))

    @pl.when(cidx == pl.num_programs(1) - 1)
    def _():
        sout_ref[0] = s_scr[...]


def _gdn_mixer(x16, ss, proj, conv_state, conv_w, w_ab, a_log, dt_bias, s0, o_norm_w, ell):
    n, t, d = x16.shape
    nh, hd = a_log.shape[0], o_norm_w.shape[0]
    hw = nh * hd
    width = conv_w.shape[0]
    rows = ell * GDN_BLOCKS_PER_STEP if t % (ell * GDN_BLOCKS_PER_STEP) == 0 else ell
    col = lambda c: pl.BlockSpec((1, rows, hw), lambda i, j: (i, j, c))
    full = lambda shape: pl.BlockSpec(shape, lambda i, j: (0,) * len(shape))
    st_spec = pl.BlockSpec((1, nh, hd, hd), lambda i, j: (i, 0, 0, 0))
    return pl.pallas_call(
        functools.partial(_gdn_kernel, ell=ell, group=_tile(nh, GDN_HEAD_GROUP, 1)),
        grid=(n, t // rows),
        in_specs=[pl.BlockSpec((1, rows, d), lambda i, j: (i, j, 0)),
                  pl.BlockSpec((1, rows, LANES), lambda i, j: (i, j, 0)), col(0), col(1), col(2), col(3),
                  pl.BlockSpec((1, width - 1, 3 * hw), lambda i, j: (i, 0, 0)), full((width, 3 * hw)),
                  full((d, 2 * nh)), full((2 * nh, d)),
                  full((1, nh)), full((nh, 1)), full((1, nh)), full((nh, 1)),
                  st_spec, full((1, hd))],
        out_specs=[pl.BlockSpec((1, rows, hw), lambda i, j: (i, j, 0)), st_spec],
        out_shape=[jax.ShapeDtypeStruct((n, t, hw), BF16),
                   jax.ShapeDtypeStruct((n, nh, hd, hd), F32)],
        scratch_shapes=[pltpu.VMEM((nh, hd, hd), F32), pltpu.VMEM((3 * nh, rows + SUBLANES, hd), F32)],
        compiler_params=_params("parallel", "arbitrary"),
        name="gdn_mixer",
    )(x16, ss, proj, proj, proj, proj, conv_state, conv_w, w_ab, w_ab.T,
      a_log.reshape(1, nh), a_log.reshape(nh, 1), dt_bias.reshape(1, nh), dt_bias.reshape(nh, 1),
      s0, o_norm_w.reshape(1, hd))


def _norm_prep_kernel(x_ref, o16_ref, ss_ref):
    x = x_ref[...]
    o16_ref[...] = x.astype(BF16)
    sq = x * x
    part = sq[:, :LANES]
    for c in range(1, sq.shape[1] // LANES):
        part = part + sq[:, c * LANES:(c + 1) * LANES]
    ss_ref[...] = part


def _norm_prep(x):
    m, d = x.shape
    tm = _tile(m, 256)
    x16, ss = pl.pallas_call(
        _norm_prep_kernel,
        grid=(m // tm,),
        in_specs=[pl.BlockSpec((tm, d), lambda i: (i, 0))],
        out_specs=[pl.BlockSpec((tm, d), lambda i: (i, 0)), pl.BlockSpec((tm, LANES), lambda i: (i, 0))],
        out_shape=[jax.ShapeDtypeStruct((m, d), BF16), jax.ShapeDtypeStruct((m, LANES), F32)],
        compiler_params=_params("parallel"),
        name="norm_prep",
    )(x)
    return x, x16, ss


def _with_casts(result, n_casts):
    if not n_casts:
        return result, []
    main = result[:len(result) - n_casts]
    return (main[0] if len(main) == 1 else main), list(result[len(result) - n_casts:])


def _swa_sconv_layer(stream, n, t, pos0, k_cache, v_cache, conv_state, w_in, sinks, conv_w, w_out,
                     side_casts=()):
    x, x16, ss = stream
    ch = conv_w.shape[1]
    aw = w_out.shape[0] - ch
    kvw = (w_in.shape[1] - aw - 3 * ch) // 2
    hd = aw // sinks.shape[0]
    proj, casts = _with_casts(_matmul(x16, w_in, row_ss=ss, side_casts=side_casts, name="in_proj_a"),
                              len(side_casts))
    proj = proj.reshape(n, t, -1)
    cos_t, sin_t = _rope_tables(t, pos0, hd)
    k_new = _rope_k(proj, cos_t, sin_t, aw, kvw, hd)
    v_new = proj[:, t - min(t, WINDOW):, aw + kvw:aw + 2 * kvw]
    attn = functools.partial(_attention, cos_t=cos_t, sin_t=sin_t, sinks=sinks,
                             aw=aw, kvw=kvw, hd=hd, out_width=aw + ch)
    if k_cache is None:
        k_all, v_win = k_new, v_new
        mix = attn(proj, k_new, 0, proj, aw + kvw, window=WINDOW)
    else:
        k_all = jnp.concatenate([k_cache, k_new], axis=1)
        v_all = jnp.concatenate([v_cache, v_new], axis=1)
        v_win = v_all
        mix = attn(proj, k_all, 0, v_all, 0, window=0)
    col_b = aw + 2 * kvw
    mix, new_state = _sconv(proj, conv_state, conv_w, mix, col_b, col_b + ch, col_b + 2 * ch, aw)
    stream = _matmul(mix.reshape(n * t, aw + ch), w_out, res=x, emit_norm=True, name="out_proj_a")
    return stream, k_all[:, -WINDOW:], v_win[:, -WINDOW:], new_state, casts


def _gdn_layer(stream, n, t, conv_state, s0, w_in, w_ab, conv_w, a_log, dt_bias, o_norm_w, w_out, ell,
               side_casts=()):
    x, x16, ss = stream
    d = x.shape[1]
    hw = a_log.shape[0] * o_norm_w.shape[0]
    proj, casts = _with_casts(_matmul(x16, w_in, row_ss=ss, side_casts=side_casts, name="in_proj_g"),
                              len(side_casts))
    proj = proj.reshape(n, t, 4 * hw)
    o, s_new = _gdn_mixer(x16.reshape(n, t, d), ss.reshape(n, t, LANES), proj, conv_state, conv_w, w_ab,
                          a_log, dt_bias, s0, o_norm_w, ell)
    stream = _matmul(o.reshape(n * t, hw), w_out, res=x, emit_norm=True, name="out_proj_g")
    width = conv_w.shape[0]
    return stream, proj[:, t - (width - 1):, :3 * hw], s_new, casts


def _mlp(stream, w_up, w_down, last, side_casts=()):
    x, x16, ss = stream
    hid, casts = _with_casts(_matmul(x16, w_up, act="relu2", row_ss=ss, side_casts=side_casts,
                                     out_dtype=BF16, name="mlp_up"), len(side_casts))
    if isinstance(w_down, CastJob):
        w_down = casts.pop(0)
    if last:
        return (_matmul(hid, w_down, res=x, name="mlp_down"), None, None), w_down, casts
    return _matmul(hid, w_down, res=x, emit_norm=True, name="mlp_down"), w_down, casts


def kernel(x_prompt, x_sample, cache_swa_k, cache_swa_v, state_sconv, state_dn_conv, state_dn,
           attn_norm, w_in_a, sinks, sconv_w, w_out_a,
           dn_norm, w_in_g, dn_conv_w, A_log, dt_bias, o_norm_w, w_out_g,
           mlp_norm, w_up, w_down, final_norm):
    (nb, tp, d), (nd, ts, _) = x_prompt.shape, x_sample.shape
    depth = mlp_norm.shape[0]
    n_kv, hd_a = cache_swa_k.shape[3], cache_swa_k.shape[4]
    nh_g, hd_g = state_dn.shape[2], state_dn.shape[3]
    sp = _norm_prep(x_prompt.reshape(nb * tp, d))
    ss = _norm_prep(x_sample.reshape(nd * ts, d))
    outs = [[] for _ in range(10)]
    hw_g = nh_g * hd_g
    w_in_g_t = jnp.swapaxes(w_in_g, 1, 2)

    def in_proj_job(li):
        if li % 2 == 0:
            return CastJob(w_in_a, li // 2, attn_norm[li // 2])
        return CastJob(w_in_g_t, li // 2, dn_norm[li // 2], True, 4 * hw_g)

    w_in = _run_cast(in_proj_job(0))
    for li in range(depth):
        j = li // 2
        up_job = [CastJob(w_up, li, mlp_norm[li])]
        next_job = [in_proj_job(li + 1)] if li + 1 < depth else []
        if li % 2 == 0:
            w_out = _to_bf16(w_out_a, j)
            zero_state = jnp.zeros((nb,) + state_sconv.shape[2:], F32)
            sp, kp, vp, cp, (wu,) = _swa_sconv_layer(sp, nb, tp, 0, None, None, zero_state, w_in,
                                                     sinks[j], sconv_w[j], w_out, up_job)
            kc = cache_swa_k[j].reshape(nd, -1, n_kv * hd_a)
            vc = cache_swa_v[j].reshape(nd, -1, n_kv * hd_a)
            ss, ks, vs, cs, _ = _swa_sconv_layer(ss, nd, ts, PAST_LEN, kc, vc, state_sconv[j], w_in,
                                                 sinks[j], sconv_w[j], w_out)
            shape5 = lambda a: a.reshape(a.shape[0], a.shape[1], n_kv, hd_a)
            for lst, val in zip(outs[:6], (shape5(kp), shape5(vp), shape5(ks), shape5(vs), cp, cs)):
                lst.append(val)
        else:
            w_ab = _to_bf16(w_in_g_t, j, 4 * hw_g, gain=dn_norm[j], transposed=True)
            w_out = _to_bf16(w_out_g, j)
            zero_conv = jnp.zeros((nb,) + state_dn_conv.shape[2:], F32)
            zero_s = jnp.zeros((nb, nh_g, hd_g, hd_g), F32)
            sp, dcp, dsp, (wu,) = _gdn_layer(sp, nb, tp, zero_conv, zero_s, w_in, w_ab, dn_conv_w[j],
                                             A_log[j], dt_bias[j], o_norm_w[j], w_out, CHUNK, up_job)
            ss, dcs, dss, _ = _gdn_layer(ss, nd, ts, state_dn_conv[j], state_dn[j], w_in, w_ab, dn_conv_w[j],
                                         A_log[j], dt_bias[j], o_norm_w[j], w_out, ts)
            for lst, val in zip(outs[6:], (dcp, dcs, dsp, dss)):
                lst.append(val)
        last = li == depth - 1
        sp, wd, nxt = _mlp(sp, wu, CastJob(w_down, li), last, [CastJob(w_down, li)] + next_job)
        ss, _, _ = _mlp(ss, wu, wd, last)
        w_in = nxt[0] if nxt else None
    y_prompt = _rmsnorm(sp[0], final_norm, F32).reshape(x_prompt.shape)
    y_sample = _rmsnorm(ss[0], final_norm, F32).reshape(x_sample.shape)
    return (y_prompt, y_sample) + tuple(jnp.stack(o, 0) for o in outs)
```

```python
import functools
from typing import NamedTuple, Optional

import jax
import jax.numpy as jnp
from jax import lax
from jax.experimental import pallas as pl
from jax.experimental.pallas import tpu as pltpu

EPS = 1e-6
CHUNK = 64
WINDOW = 128
PAST_LEN = 4096
ROPE_THETA = 500000.0
NEG_INF = -1e30
BF16 = jnp.bfloat16
F32 = jnp.float32

LANES = 128
SUBLANES = 8
BF16_ROWS = 16
MIB = 1024 * 1024
VMEM_LIMIT = 56 * MIB
MM_ROWS = 1024
MM_COLS = 1024
MM_COLS_EMIT = 512
MM_DEPTH = 4096
MM_DEPTH_LONG = 2048
MM_SMALL_ROWS = 256
EPILOGUE_ROWS = 256
ROW_TILE = 256
CAST_BLOCK_BYTES = 4 * MIB
SIDE_BLOCK_BYTES = 2 * MIB
SCONV_BLOCK_BYTES = 2 * MIB


def _params(*sem):
    return pltpu.CompilerParams(dimension_semantics=sem, vmem_limit_bytes=VMEM_LIMIT)


def _tile(n, pref, align=SUBLANES):
    if n <= pref:
        return n
    t = pref - pref % align
    while t >= align:
        if n % t == 0:
            return t
        t -= align
    return n


def _lane_partial_sq(x):
    sq = x * x
    part = sq[:, :LANES]
    for c in range(1, sq.shape[1] // LANES):
        part = part + sq[:, c * LANES:(c + 1) * LANES]
    return part


def _row_rms_scale(ss, d):
    return lax.rsqrt(jnp.sum(ss, axis=-1, keepdims=True) * (1.0 / d) + EPS)


def _rmsnorm_kernel(x_ref, g_ref, o_ref):
    x = x_ref[...]
    ms = jnp.mean(x * x, axis=-1, keepdims=True)
    o_ref[...] = (x * lax.rsqrt(ms + EPS) * g_ref[...]).astype(o_ref.dtype)


def _rmsnorm(x, g, out_dtype):
    m, d = x.shape
    tm = _tile(m, ROW_TILE)
    return pl.pallas_call(
        _rmsnorm_kernel,
        grid=(m // tm,),
        in_specs=[pl.BlockSpec((tm, d), lambda i: (i, 0)),
                  pl.BlockSpec((1, d), lambda i: (0, 0))],
        out_specs=pl.BlockSpec((tm, d), lambda i: (i, 0)),
        out_shape=jax.ShapeDtypeStruct((m, d), out_dtype),
        compiler_params=_params("parallel"),
        name="rmsnorm",
    )(x, g.reshape(1, d))


def _norm_prep_kernel(x_ref, o16_ref, ss_ref):
    x = x_ref[...]
    o16_ref[...] = x.astype(BF16)
    ss_ref[...] = _lane_partial_sq(x)


def _norm_prep(x):
    m, d = x.shape
    tm = _tile(m, ROW_TILE)
    x16, ss = pl.pallas_call(
        _norm_prep_kernel,
        grid=(m // tm,),
        in_specs=[pl.BlockSpec((tm, d), lambda i: (i, 0))],
        out_specs=[pl.BlockSpec((tm, d), lambda i: (i, 0)), pl.BlockSpec((tm, LANES), lambda i: (i, 0))],
        out_shape=[jax.ShapeDtypeStruct((m, d), BF16), jax.ShapeDtypeStruct((m, LANES), F32)],
        compiler_params=_params("parallel"),
        name="norm_prep",
    )(x)
    return x, x16, ss


class CastJob(NamedTuple):
    w: jax.Array
    layer: int
    gain: Optional[jax.Array] = None
    transposed: bool = False
    n_cols: Optional[int] = None


def _cast_kernel(*refs, has_gain, transposed):
    w = refs[0][0]
    if has_gain:
        w = w * refs[1][...]
    refs[-1][...] = (w.T if transposed else w).astype(refs[-1].dtype)


def _to_bf16(w, layer, col0=0, n_cols=None, gain=None, transposed=False):
    kdim, n = (w.shape[2], w.shape[1]) if transposed else (w.shape[1], w.shape[2])
    n_cols = n - col0 if n_cols is None else n_cols
    tc = _tile(n_cols, 2048, LANES) if n_cols >= LANES else n_cols
    tr = _tile(kdim, max(512, CAST_BLOCK_BYTES // (4 * tc)), LANES)
    assert col0 % tc == 0 and (transposed or tc % LANES == 0)
    if transposed:
        in_specs = [pl.BlockSpec((1, tc, tr), lambda i, j: (layer, col0 // tc + j, i))]
        gain_spec, gain_shape = pl.BlockSpec((1, tr), lambda i, j: (0, i)), (1, kdim)
    else:
        in_specs = [pl.BlockSpec((1, tr, tc), lambda i, j: (layer, i, col0 // tc + j))]
        gain_spec, gain_shape = pl.BlockSpec((tr, 1), lambda i, j: (i, 0)), (kdim, 1)
    args = [w]
    if gain is not None:
        in_specs.append(gain_spec)
        args.append(gain.reshape(gain_shape))
    return pl.pallas_call(
        functools.partial(_cast_kernel, has_gain=gain is not None, transposed=transposed),
        grid=(kdim // tr, n_cols // tc),
        in_specs=in_specs,
        out_specs=pl.BlockSpec((tr, tc), lambda i, j: (i, j)),
        out_shape=jax.ShapeDtypeStruct((kdim, n_cols), BF16),
        compiler_params=_params("parallel", "parallel"),
        name="weight_cast",
    )(*args)


def _run_cast(job):
    return _to_bf16(job.w, job.layer, 0, job.n_cols, gain=job.gain, transposed=job.transposed)


def _side_plan(job, n_steps):
    if job.transposed:
        n_cols, kdim = job.n_cols or job.w.shape[1], job.w.shape[2]
        n_blk, in_blk, out_blk = n_cols // LANES, (1, LANES, kdim), (kdim, LANES)
        if n_cols % LANES or kdim % LANES:
            return None
    else:
        kdim, n_cols = job.w.shape[1], job.n_cols or job.w.shape[2]
        n_blk = 1
        while n_blk * 2 <= n_steps and kdim % (n_blk * 2 * BF16_ROWS) == 0:
            n_blk *= 2
        in_blk, out_blk = (1, kdim // n_blk, n_cols), (kdim // n_blk, n_cols)
        if n_cols != job.w.shape[2]:
            return None
    if n_blk > n_steps or 4 * in_blk[1] * in_blk[2] > SIDE_BLOCK_BYTES:
        return None
    return n_blk, in_blk, out_blk, (kdim, n_cols)


def _mm_kernel(*refs, nk, act, has_res, has_ss, emit_norm, d_norm, side):
    refs = list(refs)
    a_ref, w_ref = refs.pop(0), refs.pop(0)
    r_ref = refs.pop(0) if has_res else None
    ssin_ref = refs.pop(0) if has_ss else None
    side_in = [(refs.pop(0), refs.pop(0) if has_gain else None) for has_gain, _ in side]
    o_ref = refs.pop(0)
    o16_ref, ssout_ref = (refs.pop(0), refs.pop(0)) if emit_norm else (None, None)

    for (sw_ref, g_ref), (_, transposed) in zip(side_in, side):
        so_ref = refs.pop(0)
        wv = sw_ref[0] if g_ref is None else sw_ref[0] * g_ref[...]
        so_ref[...] = (wv.T if transposed else wv).astype(so_ref.dtype)

    def finish(acc_of):
        tm = o_ref.shape[0]
        rc = min(tm, EPILOGUE_ROWS)
        parts = []
        for r0 in range(0, tm, rc):
            rows = slice(r0, r0 + rc)
            acc = acc_of(rows)
            if has_ss:
                acc = acc * _row_rms_scale(ssin_ref[rows, :], d_norm)
            if act == "relu2":
                acc = jnp.square(jnp.maximum(acc, 0.0))
            if has_res:
                acc = r_ref[rows, :] + acc
            o_ref[rows, :] = acc.astype(o_ref.dtype)
            if emit_norm:
                o16_ref[rows, :] = acc.astype(BF16)
                parts.append(_lane_partial_sq(acc))
        if emit_norm:
            part = jnp.concatenate(parts, axis=0) if len(parts) > 1 else parts[0]
            j = pl.program_id(1)

            @pl.when(j == 0)
            def _():
                ssout_ref[...] = part

            @pl.when(j > 0)
            def _():
                ssout_ref[...] += part

    if nk == 1:
        full = jnp.dot(a_ref[...], w_ref[...], preferred_element_type=F32)
        finish(lambda rows: full[rows, :])
    else:
        acc_ref = refs.pop(0)
        k = pl.program_id(2)

        @pl.when(k == 0)
        def _():
            acc_ref[...] = jnp.zeros_like(acc_ref)

        acc_ref[...] += jnp.dot(a_ref[...], w_ref[...], preferred_element_type=F32)

        @pl.when(k == nk - 1)
        def _():
            finish(lambda rows: acc_ref[rows, :])


def _matmul(a, w, *, res=None, act=None, row_ss=None, emit_norm=False, side_casts=(), out_dtype=F32,
            name="matmul"):
    m, kdim = a.shape
    n = w.shape[1]
    tm = _tile(m, MM_ROWS)
    deep = kdim <= MM_DEPTH or tm <= MM_SMALL_ROWS
    tk = _tile(kdim, MM_DEPTH if deep else MM_DEPTH_LONG, LANES)
    nk = kdim // tk
    tn = n if n < LANES else _tile(n, MM_COLS_EMIT if (emit_norm and nk == 1) else MM_COLS, LANES)
    nj = n // tn
    n_steps = (m // tm) * nj * nk
    row_blk = lambda width: pl.BlockSpec((tm, width), lambda i, j, k: (i, 0))
    out_blk = pl.BlockSpec((tm, tn), lambda i, j, k: (i, j))
    in_specs = [pl.BlockSpec((tm, tk), lambda i, j, k: (i, k)),
                pl.BlockSpec((tk, tn), lambda i, j, k: (k, j))]
    args = [a, w]
    if res is not None:
        in_specs.append(out_blk)
        args.append(res)
    if row_ss is not None:
        in_specs.append(row_blk(LANES))
        args.append(row_ss)
    out_specs, out_shape = [out_blk], [jax.ShapeDtypeStruct((m, n), out_dtype)]
    if emit_norm:
        out_specs += [out_blk, row_blk(LANES)]
        out_shape += [jax.ShapeDtypeStruct((m, n), BF16), jax.ShapeDtypeStruct((m, LANES), F32)]
    plans = [_side_plan(job, n_steps) for job in side_casts]
    side = []
    for job, plan in zip(side_casts, plans):
        if plan is None:
            continue
        n_blk, in_b, out_b, shape = plan
        blk = lambda i, j, k, rep=n_steps // n_blk, last=n_blk - 1: jnp.minimum(((i * nj + j) * nk + k) // rep, last)
        layer = job.layer
        in_specs.append(pl.BlockSpec(in_b, lambda i, j, k, blk=blk, layer=layer: (layer, blk(i, j, k), 0)))
        if job.transposed:
            gain_spec, gain_shape = pl.BlockSpec((1, shape[0]), lambda i, j, k: (0, 0)), (1, shape[0])
            out_specs.append(pl.BlockSpec(out_b, lambda i, j, k, blk=blk: (0, blk(i, j, k))))
        else:
            gain_spec = pl.BlockSpec((in_b[1], 1), lambda i, j, k, blk=blk: (blk(i, j, k), 0))
            gain_shape = (shape[0], 1)
            out_specs.append(pl.BlockSpec(out_b, lambda i, j, k, blk=blk: (blk(i, j, k), 0)))
        args.append(job.w)
        if job.gain is not None:
            in_specs.append(gain_spec)
            args.append(job.gain.reshape(gain_shape))
        out_shape.append(jax.ShapeDtypeStruct(shape, BF16))
        side.append((job.gain is not None, job.transposed))
    out = pl.pallas_call(
        functools.partial(_mm_kernel, nk=nk, act=act, has_res=res is not None,
                          has_ss=row_ss is not None, emit_norm=emit_norm, d_norm=kdim, side=tuple(side)),
        grid=(m // tm, nj, nk),
        in_specs=in_specs,
        out_specs=out_specs,
        out_shape=out_shape,
        scratch_shapes=[pltpu.VMEM((tm, tn), F32)] if nk > 1 else [],
        compiler_params=_params("arbitrary", "arbitrary", "arbitrary"),
        name=name,
    )(*args)
    out = list(out)
    n_main = 3 if emit_norm else 1
    hosted = iter(out[n_main:])
    casts = [next(hosted) if plan is not None else _run_cast(job) for job, plan in zip(side_casts, plans)]
    result = out[:n_main] + casts
    return result[0] if len(result) == 1 else tuple(result)


def _with_casts(result, n_casts):
    if not n_casts:
        return result, []
    main = result[:len(result) - n_casts]
    return (main[0] if len(main) == 1 else main), list(result[len(result) - n_casts:])


def _rope_table_kernel(inv_ref, c_ref, s_ref, *, pos0, hd):
    shape = c_ref.shape
    pos = (pos0 + lax.broadcasted_iota(jnp.int32, shape, 0)).astype(F32)
    ang = pos * inv_ref[...]
    d = jnp.bitwise_and(lax.broadcasted_iota(jnp.int32, shape, 1), hd - 1)
    rot = hd // 4
    cos, sin = jnp.cos(ang), jnp.sin(ang)
    c_ref[...] = jnp.where(d < rot, cos, 1.0)
    s_ref[...] = jnp.where(d < rot // 2, -sin, jnp.where(d < rot, sin, 0.0))


def _rope_tables(n_pos, pos0, hd):
    half = hd // 8
    inv = ROPE_THETA ** (-jnp.arange(half, dtype=F32) / half)
    inv_lane = jnp.tile(jnp.concatenate([inv, inv, jnp.zeros((hd - 2 * half,), F32)]), LANES // hd)
    shp = jax.ShapeDtypeStruct((n_pos, LANES), F32)
    return pl.pallas_call(
        functools.partial(_rope_table_kernel, pos0=pos0, hd=hd),
        out_shape=(shp, shp),
        name="rope_tables",
    )(inv_lane.reshape(1, LANES))


def _rope(x, c, s, hd):
    w = x.shape[1]
    reps = w // LANES
    cf = jnp.concatenate([c] * reps, axis=1) if reps > 1 else c
    sf = jnp.concatenate([s] * reps, axis=1) if reps > 1 else s
    d = jnp.bitwise_and(lax.broadcasted_iota(jnp.int32, x.shape, 1), hd - 1)
    half = hd // 8
    fwd = pltpu.roll(x, w - half, 1)
    bwd = pltpu.roll(x, half, 1)
    sw = jnp.where(d < half, fwd, jnp.where(d < 2 * half, bwd, 0.0))
    return x * cf + sw * sf


def _rope_k_kernel(k_ref, c_ref, s_ref, o_ref, *, hd):
    o_ref[0] = _rope(k_ref[0], c_ref[...], s_ref[...], hd)


def _rope_k(proj, cos_t, sin_t, col0, width, hd):
    n, t, _ = proj.shape
    tb = _tile(t, 512)
    return pl.pallas_call(
        functools.partial(_rope_k_kernel, hd=hd),
        grid=(n, t // tb),
        in_specs=[pl.BlockSpec((1, tb, width), lambda i, j: (i, j, col0 // width)),
                  pl.BlockSpec((tb, LANES), lambda i, j: (j, 0)),
                  pl.BlockSpec((tb, LANES), lambda i, j: (j, 0))],
        out_specs=pl.BlockSpec((1, tb, width), lambda i, j: (i, j, 0)),
        out_shape=jax.ShapeDtypeStruct((n, t, width), F32),
        compiler_params=_params("parallel", "parallel"),
        name="rope_k",
    )(proj, cos_t, sin_t)


def _attn_kernel(sink_ref, q_ref, k_ref, v_ref, c_ref, s_ref, o_ref, *,
                 rows, band, window, n_kv, group, hd):
    tb = q_ref.shape[1]
    gw = group * hd
    n_groups = tb // rows
    each = lambda f, *cols: [f(*x) for x in zip(*cols)]
    units = [(ci, h) for ci in range(n_groups) for h in range(n_kv)]
    sinks = [jnp.concatenate([jnp.full((rows, 1), sink_ref[h * group + g], F32) for g in range(group)],
                             axis=0) for h in range(n_kv)]
    ksl, valid = [], []
    for ci in range(n_groups):
        if window:
            lo = (pl.program_id(1) * n_groups + ci) * rows - window
            start = pl.multiple_of(jnp.maximum(lo, 0), rows)
            kpos = start + lax.broadcasted_iota(jnp.int32, (1, band), 1)
            valid.append(jnp.logical_and(kpos >= lo, kpos < lo + band))
            ksl.append(pl.ds(start, band))
        else:
            ksl.append(slice(0, band))
    qh = [_rope(q_ref[0, ci * rows:(ci + 1) * rows, h * gw:(h + 1) * gw],
                c_ref[ci * rows:(ci + 1) * rows, :], s_ref[ci * rows:(ci + 1) * rows, :], hd)
          for ci, h in units]
    qs = each(lambda x: jnp.concatenate([x[:, g * hd:(g + 1) * hd] for g in range(group)],
                                        axis=0).astype(BF16), qh)
    kh = [k_ref[0, ksl[ci], h * hd:(h + 1) * hd].astype(BF16) for ci, h in units]
    vh = [v_ref[0, ksl[ci], h * hd:(h + 1) * hd].astype(BF16) for ci, h in units]
    sc = each(lambda a, b: lax.dot_general(a, b, (((1,), (1,)), ((), ())),
                                           preferred_element_type=F32) * (hd ** -0.5), qs, kh)
    if window:
        sc = [jnp.where(valid[ci], x, NEG_INF) for (ci, _), x in zip(units, sc)]
    sk = [sinks[h] for _, h in units]
    m = each(lambda x, s: jnp.maximum(jnp.max(x, axis=-1, keepdims=True), s), sc, sk)
    p = each(lambda x, mx: jnp.exp(x - mx).astype(BF16), sc, m)
    ones = jnp.ones((band, hd), BF16)
    denom = each(lambda x, s, mx: jnp.dot(x, ones, preferred_element_type=F32) + jnp.exp(s - mx), p, sk, m)
    o = each(lambda x, v, d: jnp.dot(x, v, preferred_element_type=F32) / d, p, vh, denom)
    for (ci, h), x in zip(units, o):
        o_ref[0, ci * rows:(ci + 1) * rows, h * gw:(h + 1) * gw] = jnp.concatenate(
            [x[g * rows:(g + 1) * rows, :] for g in range(group)], axis=1).astype(o_ref.dtype)


def _attention(proj, k_src, k_col, v_src, v_col, cos_t, sin_t, sinks, *, aw, kvw, hd, out_width, window):
    n, t, _ = proj.shape
    tk = k_src.shape[1]
    n_kv = kvw // hd
    group = aw // kvw
    if window:
        rows, band = CHUNK, window + CHUNK
        tb = _tile(t, 4 * CHUNK, CHUNK)
    else:
        rows, band, tb = t, tk, t
    return pl.pallas_call(
        functools.partial(_attn_kernel, rows=rows, band=band, window=window,
                          n_kv=n_kv, group=group, hd=hd),
        grid=(n, t // tb),
        in_specs=[pl.BlockSpec(memory_space=pltpu.SMEM),
                  pl.BlockSpec((1, tb, aw), lambda i, j: (i, j, 0)),
                  pl.BlockSpec((1, tk, kvw), lambda i, j: (i, 0, k_col // kvw)),
                  pl.BlockSpec((1, tk, kvw), lambda i, j: (i, 0, v_col // kvw)),
                  pl.BlockSpec((tb, LANES), lambda i, j: (j, 0)),
                  pl.BlockSpec((tb, LANES), lambda i, j: (j, 0))],
        out_specs=pl.BlockSpec((1, tb, aw), lambda i, j: (i, j, 0)),
        out_shape=jax.ShapeDtypeStruct((n, t, out_width), BF16),
        compiler_params=_params("parallel", "parallel"),
        name="swa_attention",
    )(sinks, proj, k_src, v_src, cos_t, sin_t)


def _sconv_kernel(b_ref, c_ref, x_ref, st_ref, w_ref, mix_ref, z_ref, ns_ref, buf):
    del mix_ref
    t = x_ref.shape[1]
    width = w_ref.shape[0]
    buf[SUBLANES:SUBLANES + t, :] = c_ref[0] * x_ref[0]
    buf[SUBLANES - (width - 1):SUBLANES, :] = st_ref[0]
    lo = SUBLANES - (width - 1)
    acc = buf[lo:lo + t, :] * w_ref[0:1, :]
    for i in range(1, width):
        acc = acc + buf[lo + i:lo + i + t, :] * w_ref[i:i + 1, :]
    z_ref[0] = (b_ref[0] * acc).astype(z_ref.dtype)
    ns_ref[0] = buf[SUBLANES + t - (width - 1):SUBLANES + t, :]


def _sconv(proj, state, w, mix, col_b, col_c, col_x, col_out):
    n, t, _ = proj.shape
    width, ch = w.shape
    cols = (ch, col_b, col_c, col_x, col_out)
    cb = next(c for c in (2048, 1024, 512, 256, LANES)
              if all(v % c == 0 for v in cols) and t * c * 4 <= SCONV_BLOCK_BYTES)
    blk = lambda off: pl.BlockSpec((1, t, cb), lambda i, j: (i, 0, off // cb + j))
    return pl.pallas_call(
        _sconv_kernel,
        grid=(n, ch // cb),
        in_specs=[blk(col_b), blk(col_c), blk(col_x),
                  pl.BlockSpec((1, width - 1, cb), lambda i, j: (i, 0, j)),
                  pl.BlockSpec((width, cb), lambda i, j: (0, j)),
                  pl.BlockSpec(memory_space=pl.ANY)],
        out_specs=[blk(col_out),
                   pl.BlockSpec((1, width - 1, cb), lambda i, j: (i, 0, j))],
        out_shape=[jax.ShapeDtypeStruct(mix.shape, mix.dtype),
                   jax.ShapeDtypeStruct((n, width - 1, ch), F32)],
        scratch_shapes=[pltpu.VMEM((t + SUBLANES, cb), F32)],
        input_output_aliases={5: 0},
        compiler_params=_params("parallel", "parallel"),
        name="sconv",
    )(proj, proj, proj, state, w, mix)


def _softplus(x):
    return jnp.maximum(x, 0.0) + jnp.log1p(jnp.exp(-jnp.abs(x)))


def _sigmoid(x):
    return 0.5 * (jnp.tanh(0.5 * x) + 1.0)


def _silu(x):
    h = 0.5 * x
    return h + h * jnp.tanh(h)


def _bdot(a, b):
    return jnp.dot(a.astype(BF16), b.astype(BF16), preferred_element_type=F32)


def _gdn_kernel(x16_ref, ss_ref, q_ref, k_ref, v_ref, gate_ref, cst_ref, cw_ref, wab_ref, wabt_ref,
                al_ref, alt_ref, dt_ref, dtt_ref, s0_ref, ow_ref, o_ref, sout_ref, s_scr, buf):
    nh = al_ref.shape[1]
    ell, hw = q_ref.shape[1], q_ref.shape[2]
    hd = hw // nh
    width = cw_ref.shape[0]
    lo = SUBLANES - (width - 1)
    cidx = pl.program_id(1)
    heads = range(nh)
    each = lambda f, *cols: [f(*x) for x in zip(*cols)]
    nt = (((1,), (1,)), ((), ()))
    tn = (((0,), (0,)), ((), ()))
    dot_nt = lambda a, b: lax.dot_general(a, b, nt, preferred_element_type=F32)
    dot_nn = lambda a, b: jnp.dot(a, b, preferred_element_type=F32)
    to16 = lambda a: a.astype(BF16)
    l2n = lambda y: y * lax.rsqrt(jnp.sum(y * y, axis=-1, keepdims=True) + EPS)

    def conv_silu(c):
        buf[c, SUBLANES:SUBLANES + ell, :] = (q_ref, k_ref, v_ref)[c // nh][0, :, (c % nh) * hd:(c % nh + 1) * hd]
        acc = buf[c, lo:lo + ell, :] * cw_ref[0:1, c * hd:(c + 1) * hd]
        for t in range(1, width):
            acc = acc + buf[c, lo + t:lo + t + ell, :] * cw_ref[t:t + 1, c * hd:(c + 1) * hd]
        buf[c, lo:SUBLANES, :] = buf[c, SUBLANES + ell - (width - 1):SUBLANES + ell, :]
        return _silu(acc)

    @pl.when(cidx == 0)
    def _():
        s_scr[...] = s0_ref[0]
        for c in range(3 * nh):
            buf[c, lo:SUBLANES, :] = cst_ref[0, :, c * hd:(c + 1) * hd]

    i = lax.broadcasted_iota(jnp.int32, (ell, ell), 0)
    j = lax.broadcasted_iota(jnp.int32, (ell, ell), 1)
    hx = x16_ref[0]
    r_col = _row_rms_scale(ss_ref[0], hx.shape[1])
    r_row = jnp.sum(jnp.where(i == j, jnp.broadcast_to(r_col, (ell, ell)), 0.0), axis=0, keepdims=True)
    ab = dot_nn(hx, wab_ref[...]) * r_col
    abt = dot_nt(wabt_ref[...], hx) * r_row
    cum_cols = jnp.dot((i >= j).astype(F32), -jnp.exp(al_ref[...]) * _softplus(ab[:, :nh] + dt_ref[...]),
                       preferred_element_type=F32, precision=lax.Precision.HIGHEST)
    cum_rows = jnp.dot(-jnp.exp(alt_ref[...]) * _softplus(abt[:nh, :] + dtt_ref[...]), (i <= j).astype(F32),
                       preferred_element_type=F32, precision=lax.Precision.HIGHEST)
    beta_cols = _sigmoid(ab[:, nh:])
    beta_rows = _sigmoid(abt[nh:, :])
    g_row = [cum_rows[h:h + 1, :] for h in heads]
    b_row = [beta_rows[h:h + 1, :] for h in heads]
    g_col = [jnp.broadcast_to(cum_cols[:, h:h + 1], (ell, hd)) for h in heads]
    b_col = [jnp.broadcast_to(beta_cols[:, h:h + 1], (ell, ell)) for h in heads]

    q = [l2n(conv_silu(h)) * (hd ** -0.5) for h in heads]
    k = [l2n(conv_silu(nh + h)) for h in heads]
    v = [conv_silu(2 * nh + h) for h in heads]
    k16 = each(to16, k)
    kk = each(dot_nt, k16, k16)
    qk = each(dot_nt, each(to16, q), k16)
    decay = each(lambda gc, gr: jnp.exp(jnp.where(i >= j, gc[:, :ell] - gr, -jnp.inf)), g_col, g_row)
    a = each(lambda bc, x, d: jnp.where(i > j, bc * x * d, 0.0), b_col, kk, decay)
    eye = (i == j).astype(F32)
    shifts = range(3, ell.bit_length())
    blk = [jnp.right_shift(i, sh) == jnp.right_shift(j, sh) for sh in shifts]
    p = each(lambda x: -jnp.where(blk[0], x, 0.0), a)
    tm = each(lambda x: eye + x, p)
    for _ in range(2):
        p = each(_bdot, p, p)
        tm = each(lambda t, x: t + _bdot(t, x), tm, p)
    for lvl in range(1, len(blk)):
        ring = jnp.logical_and(blk[lvl], jnp.logical_not(blk[lvl - 1]))
        tl = each(lambda t, x: _bdot(t, jnp.where(ring, x, 0.0)), tm, a)
        tm = each(lambda t, x: t - _bdot(x, t), tm, tl)
    u_base = each(lambda t, b, x: _bdot(t * b, x), tm, b_row, v)
    w16 = each(lambda t, b, g, x: to16(_bdot(t * (b * jnp.exp(g)), x)), tm, b_row, g_row, k16)
    qd16 = each(lambda x, g: to16(x * jnp.exp(g)), q, g_col)
    kd16 = each(lambda x, g: to16(x * jnp.exp(g[ell - 1:ell, :] - g)), k, g_col)
    aqk16 = each(lambda x, d: to16(x * d), qk, decay)

    s = [s_scr[h] for h in heads]
    s16 = each(to16, s)
    u16 = each(lambda ub, w, x: to16(ub - dot_nn(w, x)), u_base, w16, s16)
    o_s = each(dot_nn, qd16, s16)
    o_u = each(dot_nn, aqk16, u16)
    ds = each(lambda kd, x: lax.dot_general(kd, x, tn, preferred_element_type=F32), kd16, u16)
    for h in heads:
        s_scr[h] = s[h] * jnp.exp(g_col[h][ell - 1:ell, :]) + ds[h]
        o = o_s[h] + o_u[h]
        o = o * lax.rsqrt(jnp.mean(o * o, axis=-1, keepdims=True) + EPS) * ow_ref[...]
        gt = gate_ref[0, :, h * hd:(h + 1) * hd]
        o_ref[0, :, h * hd:(h + 1) * hd] = (o * _silu(gt)).astype(o_ref.dtype)

    @pl.when(cidx == pl.num_programs(1) - 1)
    def _():
        sout_ref[0] = s_scr[...]


def _gdn_mixer(x16, ss, proj, conv_state, conv_w, w_ab, a_log, dt_bias, s0, o_norm_w, ell):
    n, t, d = x16.shape
    nh, hd = a_log.shape[0], o_norm_w.shape[0]
    hw = nh * hd
    width = conv_w.shape[0]
    col = lambda c: pl.BlockSpec((1, ell, hw), lambda i, j: (i, j, c))
    full = lambda shape: pl.BlockSpec(shape, lambda i, j: (0,) * len(shape))
    st_spec = pl.BlockSpec((1, nh, hd, hd), lambda i, j: (i, 0, 0, 0))
    return pl.pallas_call(
        _gdn_kernel,
        grid=(n, t // ell),
        in_specs=[pl.BlockSpec((1, ell, d), lambda i, j: (i, j, 0)),
                  pl.BlockSpec((1, ell, LANES), lambda i, j: (i, j, 0)), col(0), col(1), col(2), col(3),
                  pl.BlockSpec((1, width - 1, 3 * hw), lambda i, j: (i, 0, 0)), full((width, 3 * hw)),
                  full((d, 2 * nh)), full((2 * nh, d)),
                  full((1, nh)), full((nh, 1)), full((1, nh)), full((nh, 1)),
                  st_spec, full((1, hd))],
        out_specs=[pl.BlockSpec((1, ell, hw), lambda i, j: (i, j, 0)), st_spec],
        out_shape=[jax.ShapeDtypeStruct((n, t, hw), BF16),
                   jax.ShapeDtypeStruct((n, nh, hd, hd), F32)],
        scratch_shapes=[pltpu.VMEM((nh, hd, hd), F32), pltpu.VMEM((3 * nh, ell + SUBLANES, hd), F32)],
        compiler_params=_params("parallel", "arbitrary"),
        name="gdn_mixer",
    )(x16, ss, proj, proj, proj, proj, conv_state, conv_w, w_ab, w_ab.T,
      a_log.reshape(1, nh), a_log.reshape(nh, 1), dt_bias.reshape(1, nh), dt_bias.reshape(nh, 1),
      s0, o_norm_w.reshape(1, hd))


def _swa_sconv_layer(stream, n, t, pos0, k_cache, v_cache, conv_state, w_in, sinks, conv_w, w_out,
                     side_casts=()):
    x, x16, ss = stream
    ch = conv_w.shape[1]
    aw = w_out.shape[0] - ch
    kvw = (w_in.shape[1] - aw - 3 * ch) // 2
    hd = aw // sinks.shape[0]
    proj, casts = _with_casts(_matmul(x16, w_in, row_ss=ss, side_casts=side_casts, name="in_proj_a"),
                              len(side_casts))
    proj = proj.reshape(n, t, -1)
    cos_t, sin_t = _rope_tables(t, pos0, hd)
    k_new = _rope_k(proj, cos_t, sin_t, aw, kvw, hd)
    v_new = proj[:, t - min(t, WINDOW):, aw + kvw:aw + 2 * kvw]
    attn = functools.partial(_attention, cos_t=cos_t, sin_t=sin_t, sinks=sinks,
                             aw=aw, kvw=kvw, hd=hd, out_width=aw + ch)
    if k_cache is None:
        k_all, v_win = k_new, v_new
        mix = attn(proj, k_new, 0, proj, aw + kvw, window=WINDOW)
    else:
        k_all = jnp.concatenate([k_cache, k_new], axis=1)
        v_all = jnp.concatenate([v_cache, v_new], axis=1)
        v_win = v_all
        mix = attn(proj, k_all, 0, v_all, 0, window=0)
    col_b = aw + 2 * kvw
    mix, new_state = _sconv(proj, conv_state, conv_w, mix, col_b, col_b + ch, col_b + 2 * ch, aw)
    stream = _matmul(mix.reshape(n * t, aw + ch), w_out, res=x, emit_norm=True, name="out_proj_a")
    return stream, k_all[:, -WINDOW:], v_win[:, -WINDOW:], new_state, casts


def _gdn_layer(stream, n, t, conv_state, s0, w_in, w_ab, conv_w, a_log, dt_bias, o_norm_w, w_out, ell,
               side_casts=()):
    x, x16, ss = stream
    d = x.shape[1]
    hw = a_log.shape[0] * o_norm_w.shape[0]
    proj, casts = _with_casts(_matmul(x16, w_in, row_ss=ss, side_casts=side_casts, name="in_proj_g"),
                              len(side_casts))
    proj = proj.reshape(n, t, 4 * hw)
    o, s_new = _gdn_mixer(x16.reshape(n, t, d), ss.reshape(n, t, LANES), proj, conv_state, conv_w, w_ab,
                          a_log, dt_bias, s0, o_norm_w, ell)
    stream = _matmul(o.reshape(n * t, hw), w_out, res=x, emit_norm=True, name="out_proj_g")
    width = conv_w.shape[0]
    return stream, proj[:, t - (width - 1):, :3 * hw], s_new, casts


def _mlp(stream, w_up, w_down, last, side_casts=()):
    x, x16, ss = stream
    jobs = ([w_down] if isinstance(w_down, CastJob) else []) + list(side_casts)
    hid, casts = _with_casts(_matmul(x16, w_up, act="relu2", row_ss=ss, side_casts=jobs,
                                     out_dtype=BF16, name="mlp_up"), len(jobs))
    if isinstance(w_down, CastJob):
        w_down = casts.pop(0)
    if last:
        return (_matmul(hid, w_down, res=x, name="mlp_down"), None, None), w_down, casts
    return _matmul(hid, w_down, res=x, emit_norm=True, name="mlp_down"), w_down, casts


def kernel(x_prompt, x_sample, cache_swa_k, cache_swa_v, state_sconv, state_dn_conv, state_dn,
           attn_norm, w_in_a, sinks, sconv_w, w_out_a,
           dn_norm, w_in_g, dn_conv_w, A_log, dt_bias, o_norm_w, w_out_g,
           mlp_norm, w_up, w_down, final_norm):
    (nb, tp, d), (nd, ts, _) = x_prompt.shape, x_sample.shape
    depth = mlp_norm.shape[0]
    n_kv, hd_a = cache_swa_k.shape[3], cache_swa_k.shape[4]
    nh_g, hd_g = state_dn.shape[2], state_dn.shape[3]
    sp = _norm_prep(x_prompt.reshape(nb * tp, d))
    ss = _norm_prep(x_sample.reshape(nd * ts, d))
    outs = [[] for _ in range(10)]
    hw_g = nh_g * hd_g
    w_in_g_t = jnp.swapaxes(w_in_g, 1, 2)

    def in_proj_job(li):
        if li % 2 == 0:
            return CastJob(w_in_a, li // 2, attn_norm[li // 2])
        return CastJob(w_in_g_t, li // 2, dn_norm[li // 2], True, 4 * hw_g)

    w_in = _run_cast(in_proj_job(0))
    for li in range(depth):
        j = li // 2
        up_job = [CastJob(w_up, li, mlp_norm[li])]
        next_job = [in_proj_job(li + 1)] if li + 1 < depth else []
        if li % 2 == 0:
            w_out = _to_bf16(w_out_a, j)
            zero_state = jnp.zeros((nb,) + state_sconv.shape[2:], F32)
            sp, kp, vp, cp, (wu,) = _swa_sconv_layer(sp, nb, tp, 0, None, None, zero_state, w_in,
                                                     sinks[j], sconv_w[j], w_out, up_job)
            kc = cache_swa_k[j].reshape(nd, -1, n_kv * hd_a)
            vc = cache_swa_v[j].reshape(nd, -1, n_kv * hd_a)
            ss, ks, vs, cs, _ = _swa_sconv_layer(ss, nd, ts, PAST_LEN, kc, vc, state_sconv[j], w_in,
                                                 sinks[j], sconv_w[j], w_out)
            shape5 = lambda a: a.reshape(a.shape[0], a.shape[1], n_kv, hd_a)
            for lst, val in zip(outs[:6], (shape5(kp), shape5(vp), shape5(ks), shape5(vs), cp, cs)):
                lst.append(val)
        else:
            w_ab = _to_bf16(w_in_g_t, j, 4 * hw_g, gain=dn_norm[j], transposed=True)
            w_out = _to_bf16(w_out_g, j)
            zero_conv = jnp.zeros((nb,) + state_dn_conv.shape[2:], F32)
            zero_s = jnp.zeros((nb, nh_g, hd_g, hd_g), F32)
            sp, dcp, dsp, (wu,) = _gdn_layer(sp, nb, tp, zero_conv, zero_s, w_in, w_ab, dn_conv_w[j],
                                             A_log[j], dt_bias[j], o_norm_w[j], w_out, CHUNK, up_job)
            ss, dcs, dss, _ = _gdn_layer(ss, nd, ts, state_dn_conv[j], state_dn[j], w_in, w_ab, dn_conv_w[j],
                                         A_log[j], dt_bias[j], o_norm_w[j], w_out, ts)
            for lst, val in zip(outs[6:], (dcp, dcs, dsp, dss)):
                lst.append(val)
        last = li == depth - 1
        sp, wd, nxt = _mlp(sp, wu, CastJob(w_down, li), last, next_job)
        ss, _, _ = _mlp(ss, wu, wd, last)
        w_in = nxt[0] if nxt else None
    y_prompt = _rmsnorm(sp[0], final_norm, F32).reshape(x_prompt.shape)
    y_sample = _rmsnorm(ss[0], final_norm, F32).reshape(x_sample.shape)
    return (y_prompt, y_sample) + tuple(jnp.stack(o, 0) for o in outs)
```

```python
import functools
from typing import NamedTuple, Optional

import jax
import jax.numpy as jnp
from jax import lax
from jax.experimental import pallas as pl
from jax.experimental.pallas import tpu as pltpu

EPS = 1e-6
CHUNK = 64
WINDOW = 128
PAST_LEN = 4096
ROPE_THETA = 500000.0
NEG_INF = -1e30
BF16 = jnp.bfloat16
F32 = jnp.float32

LANES = 128
SUBLANES = 8
BF16_ROWS = 16
MIB = 1024 * 1024
VMEM_LIMIT = 56 * MIB
MM_ROWS = 1024
MM_COLS = 1024
MM_COLS_EMIT = 512
MM_DEPTH = 4096
MM_DEPTH_LONG = 2048
MM_SMALL_ROWS = 256
EPILOGUE_ROWS = 256
ROW_TILE = 256
CAST_BLOCK_BYTES = 4 * MIB
SIDE_BLOCK_BYTES = 2 * MIB
SCONV_BLOCK_BYTES = 2 * MIB


def _params(*sem):
    return pltpu.CompilerParams(dimension_semantics=sem, vmem_limit_bytes=VMEM_LIMIT)


def _tile(n, pref, align=SUBLANES):
    if n <= pref:
        return n
    t = pref - pref % align
    while t >= align:
        if n % t == 0:
            return t
        t -= align
    return n


def _lane_partial_sq(x):
    sq = x * x
    part = sq[:, :LANES]
    for c in range(1, sq.shape[1] // LANES):
        part = part + sq[:, c * LANES:(c + 1) * LANES]
    return part


def _row_rms_scale(ss, d):
    return lax.rsqrt(jnp.sum(ss, axis=-1, keepdims=True) * (1.0 / d) + EPS)


def _rmsnorm_kernel(x_ref, g_ref, o_ref):
    x = x_ref[...]
    ms = jnp.mean(x * x, axis=-1, keepdims=True)
    o_ref[...] = (x * lax.rsqrt(ms + EPS) * g_ref[...]).astype(o_ref.dtype)


def _rmsnorm(x, g, out_dtype):
    m, d = x.shape
    tm = _tile(m, ROW_TILE)
    return pl.pallas_call(
        _rmsnorm_kernel,
        grid=(m // tm,),
        in_specs=[pl.BlockSpec((tm, d), lambda i: (i, 0)),
                  pl.BlockSpec((1, d), lambda i: (0, 0))],
        out_specs=pl.BlockSpec((tm, d), lambda i: (i, 0)),
        out_shape=jax.ShapeDtypeStruct((m, d), out_dtype),
        compiler_params=_params("parallel"),
        name="rmsnorm",
    )(x, g.reshape(1, d))


def _norm_prep_kernel(x_ref, o16_ref, ss_ref):
    x = x_ref[...]
    o16_ref[...] = x.astype(BF16)
    ss_ref[...] = _lane_partial_sq(x)


def _norm_prep(x):
    m, d = x.shape
    tm = _tile(m, ROW_TILE)
    x16, ss = pl.pallas_call(
        _norm_prep_kernel,
        grid=(m // tm,),
        in_specs=[pl.BlockSpec((tm, d), lambda i: (i, 0))],
        out_specs=[pl.BlockSpec((tm, d), lambda i: (i, 0)), pl.BlockSpec((tm, LANES), lambda i: (i, 0))],
        out_shape=[jax.ShapeDtypeStruct((m, d), BF16), jax.ShapeDtypeStruct((m, LANES), F32)],
        compiler_params=_params("parallel"),
        name="norm_prep",
    )(x)
    return x, x16, ss


class CastJob(NamedTuple):
    w: jax.Array
    layer: int
    gain: Optional[jax.Array] = None
    transposed: bool = False
    n_cols: Optional[int] = None


def _cast_kernel(*refs, has_gain, transposed):
    w = refs[0][0]
    if has_gain:
        w = w * refs[1][...]
    refs[-1][...] = (w.T if transposed else w).astype(refs[-1].dtype)


def _to_bf16(w, layer, col0=0, n_cols=None, gain=None, transposed=False):
    kdim, n = (w.shape[2], w.shape[1]) if transposed else (w.shape[1], w.shape[2])
    n_cols = n - col0 if n_cols is None else n_cols
    tc = _tile(n_cols, 2048, LANES) if n_cols >= LANES else n_cols
    tr = _tile(kdim, max(512, CAST_BLOCK_BYTES // (4 * tc)), LANES)
    assert col0 % tc == 0 and (transposed or tc % LANES == 0)
    if transposed:
        in_specs = [pl.BlockSpec((1, tc, tr), lambda i, j: (layer, col0 // tc + j, i))]
        gain_spec, gain_shape = pl.BlockSpec((1, tr), lambda i, j: (0, i)), (1, kdim)
    else:
        in_specs = [pl.BlockSpec((1, tr, tc), lambda i, j: (layer, i, col0 // tc + j))]
        gain_spec, gain_shape = pl.BlockSpec((tr, 1), lambda i, j: (i, 0)), (kdim, 1)
    args = [w]
    if gain is not None:
        in_specs.append(gain_spec)
        args.append(gain.reshape(gain_shape))
    return pl.pallas_call(
        functools.partial(_cast_kernel, has_gain=gain is not None, transposed=transposed),
        grid=(kdim // tr, n_cols // tc),
        in_specs=in_specs,
        out_specs=pl.BlockSpec((tr, tc), lambda i, j: (i, j)),
        out_shape=jax.ShapeDtypeStruct((kdim, n_cols), BF16),
        compiler_params=_params("parallel", "parallel"),
        name="weight_cast",
    )(*args)


def _run_cast(job):
    return _to_bf16(job.w, job.layer, 0, job.n_cols, gain=job.gain, transposed=job.transposed)


def _side_plan(job, n_steps):
    if job.transposed:
        n_cols, kdim = job.n_cols or job.w.shape[1], job.w.shape[2]
        n_blk, in_blk, out_blk = n_cols // LANES, (1, LANES, kdim), (kdim, LANES)
        if n_cols % LANES or kdim % LANES:
            return None
    else:
        kdim, n_cols = job.w.shape[1], job.n_cols or job.w.shape[2]
        n_blk = 1
        while n_blk * 2 <= n_steps and kdim % (n_blk * 2 * BF16_ROWS) == 0:
            n_blk *= 2
        in_blk, out_blk = (1, kdim // n_blk, n_cols), (kdim // n_blk, n_cols)
        if n_cols != job.w.shape[2]:
            return None
    if n_blk > n_steps or 4 * in_blk[1] * in_blk[2] > SIDE_BLOCK_BYTES:
        return None
    return n_blk, in_blk, out_blk, (kdim, n_cols)


def _mm_kernel(*refs, nk, act, has_res, has_ss, emit_norm, d_norm, side):
    refs = list(refs)
    a_ref, w_ref = refs.pop(0), refs.pop(0)
    r_ref = refs.pop(0) if has_res else None
    ssin_ref = refs.pop(0) if has_ss else None
    side_in = [(refs.pop(0), refs.pop(0) if has_gain else None) for has_gain, _ in side]
    o_ref = refs.pop(0)
    o16_ref, ssout_ref = (refs.pop(0), refs.pop(0)) if emit_norm else (None, None)

    for (sw_ref, g_ref), (_, transposed) in zip(side_in, side):
        so_ref = refs.pop(0)
        wv = sw_ref[0] if g_ref is None else sw_ref[0] * g_ref[...]
        so_ref[...] = (wv.T if transposed else wv).astype(so_ref.dtype)

    def finish(acc_of):
        tm = o_ref.shape[0]
        rc = min(tm, EPILOGUE_ROWS)
        parts = []
        for r0 in range(0, tm, rc):
            rows = slice(r0, r0 + rc)
            acc = acc_of(rows)
            if has_ss:
                acc = acc * _row_rms_scale(ssin_ref[rows, :], d_norm)
            if act == "relu2":
                acc = jnp.square(jnp.maximum(acc, 0.0))
            if has_res:
                acc = r_ref[rows, :] + acc
            o_ref[rows, :] = acc.astype(o_ref.dtype)
            if emit_norm:
                o16_ref[rows, :] = acc.astype(BF16)
                parts.append(_lane_partial_sq(acc))
        if emit_norm:
            part = jnp.concatenate(parts, axis=0) if len(parts) > 1 else parts[0]
            j = pl.program_id(1)

            @pl.when(j == 0)
            def _():
                ssout_ref[...] = part

            @pl.when(j > 0)
            def _():
                ssout_ref[...] += part

    if nk == 1:
        full = jnp.dot(a_ref[...], w_ref[...], preferred_element_type=F32)
        finish(lambda rows: full[rows, :])
    else:
        acc_ref = refs.pop(0)
        k = pl.program_id(2)

        @pl.when(k == 0)
        def _():
            acc_ref[...] = jnp.zeros_like(acc_ref)

        acc_ref[...] += jnp.dot(a_ref[...], w_ref[...], preferred_element_type=F32)

        @pl.when(k == nk - 1)
        def _():
            finish(lambda rows: acc_ref[rows, :])


def _matmul(a, w, *, res=None, act=None, row_ss=None, emit_norm=False, side_casts=(), out_dtype=F32,
            name="matmul"):
    m, kdim = a.shape
    n = w.shape[1]
    tm = _tile(m, MM_ROWS)
    deep = kdim <= MM_DEPTH or tm <= MM_SMALL_ROWS
    tk = _tile(kdim, MM_DEPTH if deep else MM_DEPTH_LONG, LANES)
    nk = kdim // tk
    tn = n if n < LANES else _tile(n, MM_COLS_EMIT if (emit_norm and nk == 1) else MM_COLS, LANES)
    nj = n // tn
    n_steps = (m // tm) * nj * nk
    row_blk = lambda width: pl.BlockSpec((tm, width), lambda i, j, k: (i, 0))
    out_blk = pl.BlockSpec((tm, tn), lambda i, j, k: (i, j))
    in_specs = [pl.BlockSpec((tm, tk), lambda i, j, k: (i, k)),
                pl.BlockSpec((tk, tn), lambda i, j, k: (k, j))]
    args = [a, w]
    if res is not None:
        in_specs.append(out_blk)
        args.append(res)
    if row_ss is not None:
        in_specs.append(row_blk(LANES))
        args.append(row_ss)
    out_specs, out_shape = [out_blk], [jax.ShapeDtypeStruct((m, n), out_dtype)]
    if emit_norm:
        out_specs += [out_blk, row_blk(LANES)]
        out_shape += [jax.ShapeDtypeStruct((m, n), BF16), jax.ShapeDtypeStruct((m, LANES), F32)]
    plans = [_side_plan(job, n_steps) for job in side_casts]
    side = []
    for job, plan in zip(side_casts, plans):
        if plan is None:
            continue
        n_blk, in_b, out_b, shape = plan
        blk = lambda i, j, k, rep=n_steps // n_blk, last=n_blk - 1: jnp.minimum(((i * nj + j) * nk + k) // rep, last)
        layer = job.layer
        in_specs.append(pl.BlockSpec(in_b, lambda i, j, k, blk=blk, layer=layer: (layer, blk(i, j, k), 0)))
        if job.transposed:
            gain_spec, gain_shape = pl.BlockSpec((1, shape[0]), lambda i, j, k: (0, 0)), (1, shape[0])
            out_specs.append(pl.BlockSpec(out_b, lambda i, j, k, blk=blk: (0, blk(i, j, k))))
        else:
            gain_spec = pl.BlockSpec((in_b[1], 1), lambda i, j, k, blk=blk: (blk(i, j, k), 0))
            gain_shape = (shape[0], 1)
            out_specs.append(pl.BlockSpec(out_b, lambda i, j, k, blk=blk: (blk(i, j, k), 0)))
        args.append(job.w)
        if job.gain is not None:
            in_specs.append(gain_spec)
            args.append(job.gain.reshape(gain_shape))
        out_shape.append(jax.ShapeDtypeStruct(shape, BF16))
        side.append((job.gain is not None, job.transposed))
    out = pl.pallas_call(
        functools.partial(_mm_kernel, nk=nk, act=act, has_res=res is not None,
                          has_ss=row_ss is not None, emit_norm=emit_norm, d_norm=kdim, side=tuple(side)),
        grid=(m // tm, nj, nk),
        in_specs=in_specs,
        out_specs=out_specs,
        out_shape=out_shape,
        scratch_shapes=[pltpu.VMEM((tm, tn), F32)] if nk > 1 else [],
        compiler_params=_params("arbitrary", "arbitrary", "arbitrary"),
        name=name,
    )(*args)
    out = list(out)
    n_main = 3 if emit_norm else 1
    hosted = iter(out[n_main:])
    casts = [next(hosted) if plan is not None else _run_cast(job) for job, plan in zip(side_casts, plans)]
    result = out[:n_main] + casts
    return result[0] if len(result) == 1 else tuple(result)


def _with_casts(result, n_casts):
    if not n_casts:
        return result, []
    main = result[:len(result) - n_casts]
    return (main[0] if len(main) == 1 else main), list(result[len(result) - n_casts:])


def _rope_table_kernel(inv_ref, c_ref, s_ref, *, pos0, hd):
    shape = c_ref.shape
    pos = (pos0 + lax.broadcasted_iota(jnp.int32, shape, 0)).astype(F32)
    ang = pos * inv_ref[...]
    d = jnp.bitwise_and(lax.broadcasted_iota(jnp.int32, shape, 1), hd - 1)
    rot = hd // 4
    cos, sin = jnp.cos(ang), jnp.sin(ang)
    c_ref[...] = jnp.where(d < rot, cos, 1.0)
    s_ref[...] = jnp.where(d < rot // 2, -sin, jnp.where(d < rot, sin, 0.0))


def _rope_tables(n_pos, pos0, hd):
    half = hd // 8
    inv = ROPE_THETA ** (-jnp.arange(half, dtype=F32) / half)
    inv_lane = jnp.tile(jnp.concatenate([inv, inv, jnp.zeros((hd - 2 * half,), F32)]), LANES // hd)
    shp = jax.ShapeDtypeStruct((n_pos, LANES), F32)
    return pl.pallas_call(
        functools.partial(_rope_table_kernel, pos0=pos0, hd=hd),
        out_shape=(shp, shp),
        name="rope_tables",
    )(inv_lane.reshape(1, LANES))


def _rope(x, c, s, hd):
    w = x.shape[1]
    reps = w // LANES
    cf = jnp.concatenate([c] * reps, axis=1) if reps > 1 else c
    sf = jnp.concatenate([s] * reps, axis=1) if reps > 1 else s
    d = jnp.bitwise_and(lax.broadcasted_iota(jnp.int32, x.shape, 1), hd - 1)
    half = hd // 8
    fwd = pltpu.roll(x, w - half, 1)
    bwd = pltpu.roll(x, half, 1)
    sw = jnp.where(d < half, fwd, jnp.where(d < 2 * half, bwd, 0.0))
    return x * cf + sw * sf


def _rope_k_kernel(k_ref, c_ref, s_ref, o_ref, *, hd):
    o_ref[0] = _rope(k_ref[0].astype(F32), c_ref[...], s_ref[...], hd)


def _rope_k(proj, cos_t, sin_t, col0, width, hd):
    n, t, _ = proj.shape
    tb = _tile(t, 512)
    return pl.pallas_call(
        functools.partial(_rope_k_kernel, hd=hd),
        grid=(n, t // tb),
        in_specs=[pl.BlockSpec((1, tb, width), lambda i, j: (i, j, col0 // width)),
                  pl.BlockSpec((tb, LANES), lambda i, j: (j, 0)),
                  pl.BlockSpec((tb, LANES), lambda i, j: (j, 0))],
        out_specs=pl.BlockSpec((1, tb, width), lambda i, j: (i, j, 0)),
        out_shape=jax.ShapeDtypeStruct((n, t, width), F32),
        compiler_params=_params("parallel", "parallel"),
        name="rope_k",
    )(proj, cos_t, sin_t)


def _attn_kernel(sink_ref, q_ref, k_ref, v_ref, c_ref, s_ref, o_ref, *,
                 rows, band, window, n_kv, group, hd):
    tb = q_ref.shape[1]
    gw = group * hd
    n_groups = tb // rows
    each = lambda f, *cols: [f(*x) for x in zip(*cols)]
    units = [(ci, h) for ci in range(n_groups) for h in range(n_kv)]
    sinks = [jnp.concatenate([jnp.full((rows, 1), sink_ref[h * group + g], F32) for g in range(group)],
                             axis=0) for h in range(n_kv)]
    ksl, valid = [], []
    for ci in range(n_groups):
        if window:
            lo = (pl.program_id(1) * n_groups + ci) * rows - window
            start = pl.multiple_of(jnp.maximum(lo, 0), rows)
            kpos = start + lax.broadcasted_iota(jnp.int32, (1, band), 1)
            valid.append(jnp.logical_and(kpos >= lo, kpos < lo + band))
            ksl.append(pl.ds(start, band))
        else:
            ksl.append(slice(0, band))
    qh = [_rope(q_ref[0, ci * rows:(ci + 1) * rows, h * gw:(h + 1) * gw].astype(F32),
                c_ref[ci * rows:(ci + 1) * rows, :], s_ref[ci * rows:(ci + 1) * rows, :], hd)
          for ci, h in units]
    qs = each(lambda x: jnp.concatenate([x[:, g * hd:(g + 1) * hd] for g in range(group)],
                                        axis=0).astype(BF16), qh)
    kh = [k_ref[0, ksl[ci], h * hd:(h + 1) * hd].astype(BF16) for ci, h in units]
    vh = [v_ref[0, ksl[ci], h * hd:(h + 1) * hd].astype(BF16) for ci, h in units]
    sc = each(lambda a, b: lax.dot_general(a, b, (((1,), (1,)), ((), ())),
                                           preferred_element_type=F32) * (hd ** -0.5), qs, kh)
    if window:
        sc = [jnp.where(valid[ci], x, NEG_INF) for (ci, _), x in zip(units, sc)]
    sk = [sinks[h] for _, h in units]
    m = each(lambda x, s: jnp.maximum(jnp.max(x, axis=-1, keepdims=True), s), sc, sk)
    p = each(lambda x, mx: jnp.exp(x - mx).astype(BF16), sc, m)
    ones = jnp.ones((band, hd), BF16)
    denom = each(lambda x, s, mx: jnp.dot(x, ones, preferred_element_type=F32) + jnp.exp(s - mx), p, sk, m)
    o = each(lambda x, v, d: jnp.dot(x, v, preferred_element_type=F32) / d, p, vh, denom)
    for (ci, h), x in zip(units, o):
        o_ref[0, ci * rows:(ci + 1) * rows, h * gw:(h + 1) * gw] = jnp.concatenate(
            [x[g * rows:(g + 1) * rows, :] for g in range(group)], axis=1).astype(o_ref.dtype)


def _attention(proj, k_src, k_col, v_src, v_col, cos_t, sin_t, sinks, *, aw, kvw, hd, out_width, window):
    n, t, _ = proj.shape
    tk = k_src.shape[1]
    n_kv = kvw // hd
    group = aw // kvw
    if window:
        rows, band = CHUNK, window + CHUNK
        tb = _tile(t, 4 * CHUNK, CHUNK)
    else:
        rows, band, tb = t, tk, t
    return pl.pallas_call(
        functools.partial(_attn_kernel, rows=rows, band=band, window=window,
                          n_kv=n_kv, group=group, hd=hd),
        grid=(n, t // tb),
        in_specs=[pl.BlockSpec(memory_space=pltpu.SMEM),
                  pl.BlockSpec((1, tb, aw), lambda i, j: (i, j, 0)),
                  pl.BlockSpec((1, tk, kvw), lambda i, j: (i, 0, k_col // kvw)),
                  pl.BlockSpec((1, tk, kvw), lambda i, j: (i, 0, v_col // kvw)),
                  pl.BlockSpec((tb, LANES), lambda i, j: (j, 0)),
                  pl.BlockSpec((tb, LANES), lambda i, j: (j, 0))],
        out_specs=pl.BlockSpec((1, tb, aw), lambda i, j: (i, j, 0)),
        out_shape=jax.ShapeDtypeStruct((n, t, out_width), BF16),
        compiler_params=_params("parallel", "parallel"),
        name="swa_attention",
    )(sinks, proj, k_src, v_src, cos_t, sin_t)


def _sconv_kernel(b_ref, c_ref, x_ref, st_ref, w_ref, mix_ref, z_ref, ns_ref, buf):
    del mix_ref
    t = x_ref.shape[1]
    width = w_ref.shape[0]
    buf[SUBLANES:SUBLANES + t, :] = c_ref[0].astype(F32) * x_ref[0].astype(F32)
    buf[SUBLANES - (width - 1):SUBLANES, :] = st_ref[0]
    lo = SUBLANES - (width - 1)
    acc = buf[lo:lo + t, :] * w_ref[0:1, :]
    for i in range(1, width):
        acc = acc + buf[lo + i:lo + i + t, :] * w_ref[i:i + 1, :]
    z_ref[0] = (b_ref[0].astype(F32) * acc).astype(z_ref.dtype)
    ns_ref[0] = buf[SUBLANES + t - (width - 1):SUBLANES + t, :]


def _sconv(proj, state, w, mix, col_b, col_c, col_x, col_out):
    n, t, _ = proj.shape
    width, ch = w.shape
    cols = (ch, col_b, col_c, col_x, col_out)
    cb = next(c for c in (2048, 1024, 512, 256, LANES)
              if all(v % c == 0 for v in cols) and t * c * 4 <= SCONV_BLOCK_BYTES)
    blk = lambda off: pl.BlockSpec((1, t, cb), lambda i, j: (i, 0, off // cb + j))
    return pl.pallas_call(
        _sconv_kernel,
        grid=(n, ch // cb),
        in_specs=[blk(col_b), blk(col_c), blk(col_x),
                  pl.BlockSpec((1, width - 1, cb), lambda i, j: (i, 0, j)),
                  pl.BlockSpec((width, cb), lambda i, j: (0, j)),
                  pl.BlockSpec(memory_space=pl.ANY)],
        out_specs=[blk(col_out),
                   pl.BlockSpec((1, width - 1, cb), lambda i, j: (i, 0, j))],
        out_shape=[jax.ShapeDtypeStruct(mix.shape, mix.dtype),
                   jax.ShapeDtypeStruct((n, width - 1, ch), F32)],
        scratch_shapes=[pltpu.VMEM((t + SUBLANES, cb), F32)],
        input_output_aliases={5: 0},
        compiler_params=_params("parallel", "parallel"),
        name="sconv",
    )(proj, proj, proj, state, w, mix)


def _softplus(x):
    return jnp.maximum(x, 0.0) + jnp.log1p(jnp.exp(-jnp.abs(x)))


def _sigmoid(x):
    return 0.5 * (jnp.tanh(0.5 * x) + 1.0)


def _silu(x):
    h = 0.5 * x
    return h + h * jnp.tanh(h)


def _bdot(a, b):
    return jnp.dot(a.astype(BF16), b.astype(BF16), preferred_element_type=F32)


def _gdn_kernel(x16_ref, ss_ref, q_ref, k_ref, v_ref, gate_ref, cst_ref, cw_ref, wab_ref, wabt_ref,
                al_ref, alt_ref, dt_ref, dtt_ref, s0_ref, ow_ref, o_ref, sout_ref, s_scr, buf):
    nh = al_ref.shape[1]
    ell, hw = q_ref.shape[1], q_ref.shape[2]
    hd = hw // nh
    width = cw_ref.shape[0]
    lo = SUBLANES - (width - 1)
    cidx = pl.program_id(1)
    heads = range(nh)
    each = lambda f, *cols: [f(*x) for x in zip(*cols)]
    nt = (((1,), (1,)), ((), ()))
    tn = (((0,), (0,)), ((), ()))
    dot_nt = lambda a, b: lax.dot_general(a, b, nt, preferred_element_type=F32)
    dot_nn = lambda a, b: jnp.dot(a, b, preferred_element_type=F32)
    to16 = lambda a: a.astype(BF16)
    l2n = lambda y: y * lax.rsqrt(jnp.sum(y * y, axis=-1, keepdims=True) + EPS)

    def conv_silu(c):
        buf[c, SUBLANES:SUBLANES + ell, :] = (q_ref, k_ref, v_ref)[c // nh][0, :, (c % nh) * hd:(c % nh + 1) * hd]
        acc = buf[c, lo:lo + ell, :] * cw_ref[0:1, c * hd:(c + 1) * hd]
        for t in range(1, width):
            acc = acc + buf[c, lo + t:lo + t + ell, :] * cw_ref[t:t + 1, c * hd:(c + 1) * hd]
        buf[c, lo:SUBLANES, :] = buf[c, SUBLANES + ell - (width - 1):SUBLANES + ell, :]
        return _silu(acc)

    @pl.when(cidx == 0)
    def _():
        s_scr[...] = s0_ref[0]
        for c in range(3 * nh):
            buf[c, lo:SUBLANES, :] = cst_ref[0, :, c * hd:(c + 1) * hd]

    i = lax.broadcasted_iota(jnp.int32, (ell, ell), 0)
    j = lax.broadcasted_iota(jnp.int32, (ell, ell), 1)
    hx = x16_ref[0]
    r_col = _row_rms_scale(ss_ref[0], hx.shape[1])
    r_row = jnp.sum(jnp.where(i == j, jnp.broadcast_to(r_col, (ell, ell)), 0.0), axis=0, keepdims=True)
    ab = dot_nn(hx, wab_ref[...]) * r_col
    abt = dot_nt(wabt_ref[...], hx) * r_row
    cum_cols = jnp.dot((i >= j).astype(F32), -jnp.exp(al_ref[...]) * _softplus(ab[:, :nh] + dt_ref[...]),
                       preferred_element_type=F32, precision=lax.Precision.HIGHEST)
    cum_rows = jnp.dot(-jnp.exp(alt_ref[...]) * _softplus(abt[:nh, :] + dtt_ref[...]), (i <= j).astype(F32),
                       preferred_element_type=F32, precision=lax.Precision.HIGHEST)
    beta_cols = _sigmoid(ab[:, nh:])
    beta_rows = _sigmoid(abt[nh:, :])
    g_row = [cum_rows[h:h + 1, :] for h in heads]
    b_row = [beta_rows[h:h + 1, :] for h in heads]
    g_col = [jnp.broadcast_to(cum_cols[:, h:h + 1], (ell, hd)) for h in heads]
    b_col = [jnp.broadcast_to(beta_cols[:, h:h + 1], (ell, ell)) for h in heads]

    q = [l2n(conv_silu(h)) * (hd ** -0.5) for h in heads]
    k = [l2n(conv_silu(nh + h)) for h in heads]
    v = [conv_silu(2 * nh + h) for h in heads]
    k16 = each(to16, k)
    kk = each(dot_nt, k16, k16)
    qk = each(dot_nt, each(to16, q), k16)
    decay = each(lambda gc, gr: jnp.exp(jnp.where(i >= j, gc[:, :ell] - gr, -jnp.inf)), g_col, g_row)
    a = each(lambda bc, x, d: jnp.where(i > j, bc * x * d, 0.0), b_col, kk, decay)
    eye = (i == j).astype(F32)
    shifts = range(3, ell.bit_length())
    blk = [jnp.right_shift(i, sh) == jnp.right_shift(j, sh) for sh in shifts]
    p = each(lambda x: -jnp.where(blk[0], x, 0.0), a)
    tm = each(lambda x: eye + x, p)
    for _ in range(2):
        p = each(_bdot, p, p)
        tm = each(lambda t, x: t + _bdot(t, x), tm, p)
    for lvl in range(1, len(blk)):
        ring = jnp.logical_and(blk[lvl], jnp.logical_not(blk[lvl - 1]))
        tl = each(lambda t, x: _bdot(t, jnp.where(ring, x, 0.0)), tm, a)
        tm = each(lambda t, x: t - _bdot(x, t), tm, tl)
    u_base = each(lambda t, b, x: _bdot(t * b, x), tm, b_row, v)
    w16 = each(lambda t, b, g, x: to16(_bdot(t * (b * jnp.exp(g)), x)), tm, b_row, g_row, k16)
    qd16 = each(lambda x, g: to16(x * jnp.exp(g)), q, g_col)
    kd16 = each(lambda x, g: to16(x * jnp.exp(g[ell - 1:ell, :] - g)), k, g_col)
    aqk16 = each(lambda x, d: to16(x * d), qk, decay)

    s = [s_scr[h] for h in heads]
    s16 = each(to16, s)
    u16 = each(lambda ub, w, x: to16(ub - dot_nn(w, x)), u_base, w16, s16)
    o_s = each(dot_nn, qd16, s16)
    o_u = each(dot_nn, aqk16, u16)
    ds = each(lambda kd, x: lax.dot_general(kd, x, tn, preferred_element_type=F32), kd16, u16)
    for h in heads:
        s_scr[h] = s[h] * jnp.exp(g_col[h][ell - 1:ell, :]) + ds[h]
        o = o_s[h] + o_u[h]
        o = o * lax.rsqrt(jnp.mean(o * o, axis=-1, keepdims=True) + EPS) * ow_ref[...]
        gt = gate_ref[0, :, h * hd:(h + 1) * hd]
        o_ref[0, :, h * hd:(h + 1) * hd] = (o * _silu(gt)).astype(o_ref.dtype)

    @pl.when(cidx == pl.num_programs(1) - 1)
    def _():
        sout_ref[0] = s_scr[...]


def _gdn_mixer(x16, ss, proj, conv_state, conv_w, w_ab, a_log, dt_bias, s0, o_norm_w, ell):
    n, t, d = x16.shape
    nh, hd = a_log.shape[0], o_norm_w.shape[0]
    hw = nh * hd
    width = conv_w.shape[0]
    col = lambda c: pl.BlockSpec((1, ell, hw), lambda i, j: (i, j, c))
    full = lambda shape: pl.BlockSpec(shape, lambda i, j: (0,) * len(shape))
    st_spec = pl.BlockSpec((1, nh, hd, hd), lambda i, j: (i, 0, 0, 0))
    return pl.pallas_call(
        _gdn_kernel,
        grid=(n, t // ell),
        in_specs=[pl.BlockSpec((1, ell, d), lambda i, j: (i, j, 0)),
                  pl.BlockSpec((1, ell, LANES), lambda i, j: (i, j, 0)), col(0), col(1), col(2), col(3),
                  pl.BlockSpec((1, width - 1, 3 * hw), lambda i, j: (i, 0, 0)), full((width, 3 * hw)),
                  full((d, 2 * nh)), full((2 * nh, d)),
                  full((1, nh)), full((nh, 1)), full((1, nh)), full((nh, 1)),
                  st_spec, full((1, hd))],
        out_specs=[pl.BlockSpec((1, ell, hw), lambda i, j: (i, j, 0)), st_spec],
        out_shape=[jax.ShapeDtypeStruct((n, t, hw), BF16),
                   jax.ShapeDtypeStruct((n, nh, hd, hd), F32)],
        scratch_shapes=[pltpu.VMEM((nh, hd, hd), F32), pltpu.VMEM((3 * nh, ell + SUBLANES, hd), F32)],
        compiler_params=_params("parallel", "arbitrary"),
        name="gdn_mixer",
    )(x16, ss, proj, proj, proj, proj, conv_state, conv_w, w_ab, w_ab.T,
      a_log.reshape(1, nh), a_log.reshape(nh, 1), dt_bias.reshape(1, nh), dt_bias.reshape(nh, 1),
      s0, o_norm_w.reshape(1, hd))


def _swa_sconv_layer(stream, n, t, pos0, k_cache, v_cache, conv_state, w_in, sinks, conv_w, w_out,
                     side_casts=()):
    x, x16, ss = stream
    ch = conv_w.shape[1]
    aw = w_out.shape[0] - ch
    kvw = (w_in.shape[1] - aw - 3 * ch) // 2
    hd = aw // sinks.shape[0]
    proj, casts = _with_casts(_matmul(x16, w_in, row_ss=ss, side_casts=side_casts, out_dtype=BF16,
                                      name="in_proj_a"), len(side_casts))
    proj = proj.reshape(n, t, -1)
    cos_t, sin_t = _rope_tables(t, pos0, hd)
    k_new = _rope_k(proj, cos_t, sin_t, aw, kvw, hd)
    v_new = proj[:, t - min(t, WINDOW):, aw + kvw:aw + 2 * kvw].astype(F32)
    attn = functools.partial(_attention, cos_t=cos_t, sin_t=sin_t, sinks=sinks,
                             aw=aw, kvw=kvw, hd=hd, out_width=aw + ch)
    if k_cache is None:
        k_all, v_win = k_new, v_new
        mix = attn(proj, k_new, 0, proj, aw + kvw, window=WINDOW)
    else:
        k_all = jnp.concatenate([k_cache, k_new], axis=1)
        v_all = jnp.concatenate([v_cache, v_new], axis=1)
        v_win = v_all
        mix = attn(proj, k_all, 0, v_all, 0, window=0)
    col_b = aw + 2 * kvw
    mix, new_state = _sconv(proj, conv_state, conv_w, mix, col_b, col_b + ch, col_b + 2 * ch, aw)
    stream = _matmul(mix.reshape(n * t, aw + ch), w_out, res=x, emit_norm=True, name="out_proj_a")
    return stream, k_all[:, -WINDOW:], v_win[:, -WINDOW:], new_state, casts


def _gdn_layer(stream, n, t, conv_state, s0, w_in, w_ab, conv_w, a_log, dt_bias, o_norm_w, w_out, ell,
               side_casts=()):
    x, x16, ss = stream
    d = x.shape[1]
    hw = a_log.shape[0] * o_norm_w.shape[0]
    proj, casts = _with_casts(_matmul(x16, w_in, row_ss=ss, side_casts=side_casts, name="in_proj_g"),
                              len(side_casts))
    proj = proj.reshape(n, t, 4 * hw)
    o, s_new = _gdn_mixer(x16.reshape(n, t, d), ss.reshape(n, t, LANES), proj, conv_state, conv_w, w_ab,
                          a_log, dt_bias, s0, o_norm_w, ell)
    stream = _matmul(o.reshape(n * t, hw), w_out, res=x, emit_norm=True, name="out_proj_g")
    width = conv_w.shape[0]
    return stream, proj[:, t - (width - 1):, :3 * hw], s_new, casts


def _mlp(stream, w_up, w_down, last, side_casts=()):
    x, x16, ss = stream
    jobs = ([w_down] if isinstance(w_down, CastJob) else []) + list(side_casts)
    hid, casts = _with_casts(_matmul(x16, w_up, act="relu2", row_ss=ss, side_casts=jobs,
                                     out_dtype=BF16, name="mlp_up"), len(jobs))
    if isinstance(w_down, CastJob):
        w_down = casts.pop(0)
    if last:
        return (_matmul(hid, w_down, res=x, name="mlp_down"), None, None), w_down, casts
    return _matmul(hid, w_down, res=x, emit_norm=True, name="mlp_down"), w_down, casts


def kernel(x_prompt, x_sample, cache_swa_k, cache_swa_v, state_sconv, state_dn_conv, state_dn,
           attn_norm, w_in_a, sinks, sconv_w, w_out_a,
           dn_norm, w_in_g, dn_conv_w, A_log, dt_bias, o_norm_w, w_out_g,
           mlp_norm, w_up, w_down, final_norm):
    (nb, tp, d), (nd, ts, _) = x_prompt.shape, x_sample.shape
    depth = mlp_norm.shape[0]
    n_kv, hd_a = cache_swa_k.shape[3], cache_swa_k.shape[4]
    nh_g, hd_g = state_dn.shape[2], state_dn.shape[3]
    sp = _norm_prep(x_prompt.reshape(nb * tp, d))
    ss = _norm_prep(x_sample.reshape(nd * ts, d))
    outs = [[] for _ in range(10)]
    hw_g = nh_g * hd_g
    w_in_g_t = jnp.swapaxes(w_in_g, 1, 2)

    def in_proj_job(li):
        if li % 2 == 0:
            return CastJob(w_in_a, li // 2, attn_norm[li // 2])
        return CastJob(w_in_g_t, li // 2, dn_norm[li // 2], True, 4 * hw_g)

    w_in = _run_cast(in_proj_job(0))
    for li in range(depth):
        j = li // 2
        up_job = [CastJob(w_up, li, mlp_norm[li])]
        next_job = [in_proj_job(li + 1)] if li + 1 < depth else []
        if li % 2 == 0:
            w_out = _to_bf16(w_out_a, j)
            zero_state = jnp.zeros((nb,) + state_sconv.shape[2:], F32)
            sp, kp, vp, cp, (wu,) = _swa_sconv_layer(sp, nb, tp, 0, None, None, zero_state, w_in,
                                                     sinks[j], sconv_w[j], w_out, up_job)
            kc = cache_swa_k[j].reshape(nd, -1, n_kv * hd_a)
            vc = cache_swa_v[j].reshape(nd, -1, n_kv * hd_a)
            ss, ks, vs, cs, _ = _swa_sconv_layer(ss, nd, ts, PAST_LEN, kc, vc, state_sconv[j], w_in,
                                                 sinks[j], sconv_w[j], w_out)
            shape5 = lambda a: a.reshape(a.shape[0], a.shape[1], n_kv, hd_a)
            for lst, val in zip(outs[:6], (shape5(kp), shape5(vp), shape5(ks), shape5(vs), cp, cs)):
                lst.append(val)
        else:
            w_ab = _to_bf16(w_in_g_t, j, 4 * hw_g, gain=dn_norm[j], transposed=True)
            w_out = _to_bf16(w_out_g, j)
            zero_conv = jnp.zeros((nb,) + state_dn_conv.shape[2:], F32)
            zero_s = jnp.zeros((nb, nh_g, hd_g, hd_g), F32)
            sp, dcp, dsp, (wu,) = _gdn_layer(sp, nb, tp, zero_conv, zero_s, w_in, w_ab, dn_conv_w[j],
                                             A_log[j], dt_bias[j], o_norm_w[j], w_out, CHUNK, up_job)
            ss, dcs, dss, _ = _gdn_layer(ss, nd, ts, state_dn_conv[j], state_dn[j], w_in, w_ab, dn_conv_w[j],
                                         A_log[j], dt_bias[j], o_norm_w[j], w_out, ts)
            for lst, val in zip(outs[6:], (dcp, dcs, dsp, dss)):
                lst.append(val)
        last = li == depth - 1
        sp, wd, nxt = _mlp(sp, wu, CastJob(w_down, li), last, next_job)
        ss, _, _ = _mlp(ss, wu, wd, last)
        w_in = nxt[0] if nxt else None
    y_prompt = _rmsnorm(sp[0], final_norm, F32).reshape(x_prompt.shape)
    y_sample = _rmsnorm(ss[0], final_norm, F32).reshape(x_sample.shape)
    return (y_prompt, y_sample) + tuple(jnp.stack(o, 0) for o in outs)
```

```python
import functools
from typing import NamedTuple, Optional

import jax
import jax.numpy as jnp
from jax import lax
from jax.experimental import pallas as pl
from jax.experimental.pallas import tpu as pltpu

EPS = 1e-6
CHUNK = 64
WINDOW = 128
PAST_LEN = 4096
ROPE_THETA = 500000.0
NEG_INF = -1e30
BF16 = jnp.bfloat16
F32 = jnp.float32

LANES = 128
SUBLANES = 8
BF16_ROWS = 16
MIB = 1024 * 1024
VMEM_LIMIT = 56 * MIB
MM_ROWS = 1024
MM_COLS = 1024
MM_COLS_EMIT = 512
MM_DEPTH = 4096
MM_DEPTH_LONG = 2048
MM_SMALL_ROWS = 256
EPILOGUE_ROWS = 256
ROW_TILE = 256
CAST_BLOCK_BYTES = 4 * MIB
SIDE_BLOCK_BYTES = 2 * MIB
SCONV_BLOCK_BYTES = 2 * MIB


def _params(*sem):
    return pltpu.CompilerParams(dimension_semantics=sem, vmem_limit_bytes=VMEM_LIMIT)


def _tile(n, pref, align=SUBLANES):
    if n <= pref:
        return n
    t = pref - pref % align
    while t >= align:
        if n % t == 0:
            return t
        t -= align
    return n


def _lane_partial_sq(x):
    sq = x * x
    part = sq[:, :LANES]
    for c in range(1, sq.shape[1] // LANES):
        part = part + sq[:, c * LANES:(c + 1) * LANES]
    return part


def _row_rms_scale(ss, d):
    return lax.rsqrt(jnp.sum(ss, axis=-1, keepdims=True) * (1.0 / d) + EPS)


def _rmsnorm_kernel(x_ref, g_ref, o_ref):
    x = x_ref[...]
    ms = jnp.mean(x * x, axis=-1, keepdims=True)
    o_ref[...] = (x * lax.rsqrt(ms + EPS) * g_ref[...]).astype(o_ref.dtype)


def _rmsnorm(x, g, out_dtype):
    m, d = x.shape
    tm = _tile(m, ROW_TILE)
    return pl.pallas_call(
        _rmsnorm_kernel,
        grid=(m // tm,),
        in_specs=[pl.BlockSpec((tm, d), lambda i: (i, 0)),
                  pl.BlockSpec((1, d), lambda i: (0, 0))],
        out_specs=pl.BlockSpec((tm, d), lambda i: (i, 0)),
        out_shape=jax.ShapeDtypeStruct((m, d), out_dtype),
        compiler_params=_params("parallel"),
        name="rmsnorm",
    )(x, g.reshape(1, d))


def _norm_prep_kernel(x_ref, o16_ref, ss_ref):
    x = x_ref[...]
    o16_ref[...] = x.astype(BF16)
    ss_ref[...] = _lane_partial_sq(x)


def _norm_prep(x):
    m, d = x.shape
    tm = _tile(m, ROW_TILE)
    x16, ss = pl.pallas_call(
        _norm_prep_kernel,
        grid=(m // tm,),
        in_specs=[pl.BlockSpec((tm, d), lambda i: (i, 0))],
        out_specs=[pl.BlockSpec((tm, d), lambda i: (i, 0)), pl.BlockSpec((tm, LANES), lambda i: (i, 0))],
        out_shape=[jax.ShapeDtypeStruct((m, d), BF16), jax.ShapeDtypeStruct((m, LANES), F32)],
        compiler_params=_params("parallel"),
        name="norm_prep",
    )(x)
    return x, x16, ss


class CastJob(NamedTuple):
    w: jax.Array
    layer: int
    gain: Optional[jax.Array] = None
    transposed: bool = False
    n_cols: Optional[int] = None


def _cast_kernel(*refs, has_gain, transposed):
    w = refs[0][0]
    if has_gain:
        w = w * refs[1][...]
    refs[-1][...] = (w.T if transposed else w).astype(refs[-1].dtype)


def _to_bf16(w, layer, col0=0, n_cols=None, gain=None, transposed=False):
    kdim, n = (w.shape[2], w.shape[1]) if transposed else (w.shape[1], w.shape[2])
    n_cols = n - col0 if n_cols is None else n_cols
    tc = _tile(n_cols, 2048, LANES) if n_cols >= LANES else n_cols
    tr = _tile(kdim, max(512, CAST_BLOCK_BYTES // (4 * tc)), LANES)
    assert col0 % tc == 0 and (transposed or tc % LANES == 0)
    if transposed:
        in_specs = [pl.BlockSpec((1, tc, tr), lambda i, j: (layer, col0 // tc + j, i))]
        gain_spec, gain_shape = pl.BlockSpec((1, tr), lambda i, j: (0, i)), (1, kdim)
    else:
        in_specs = [pl.BlockSpec((1, tr, tc), lambda i, j: (layer, i, col0 // tc + j))]
        gain_spec, gain_shape = pl.BlockSpec((tr, 1), lambda i, j: (i, 0)), (kdim, 1)
    args = [w]
    if gain is not None:
        in_specs.append(gain_spec)
        args.append(gain.reshape(gain_shape))
    return pl.pallas_call(
        functools.partial(_cast_kernel, has_gain=gain is not None, transposed=transposed),
        grid=(kdim // tr, n_cols // tc),
        in_specs=in_specs,
        out_specs=pl.BlockSpec((tr, tc), lambda i, j: (i, j)),
        out_shape=jax.ShapeDtypeStruct((kdim, n_cols), BF16),
        compiler_params=_params("parallel", "parallel"),
        name="weight_cast",
    )(*args)


def _run_cast(job):
    return _to_bf16(job.w, job.layer, 0, job.n_cols, gain=job.gain, transposed=job.transposed)


def _side_plan(job, n_steps):
    if job.transposed:
        n_cols, kdim = job.n_cols or job.w.shape[1], job.w.shape[2]
        n_blk, in_blk, out_blk = n_cols // LANES, (1, LANES, kdim), (kdim, LANES)
        if n_cols % LANES or kdim % LANES:
            return None
    else:
        kdim, n_cols = job.w.shape[1], job.n_cols or job.w.shape[2]
        n_blk = 1
        while n_blk * 2 <= n_steps and kdim % (n_blk * 2 * BF16_ROWS) == 0:
            n_blk *= 2
        in_blk, out_blk = (1, kdim // n_blk, n_cols), (kdim // n_blk, n_cols)
        if n_cols != job.w.shape[2]:
            return None
    if n_blk > n_steps or 4 * in_blk[1] * in_blk[2] > SIDE_BLOCK_BYTES:
        return None
    return n_blk, in_blk, out_blk, (kdim, n_cols)


def _mm_kernel(*refs, nk, act, has_res, has_ss, emit_norm, d_norm, side):
    refs = list(refs)
    a_ref, w_ref = refs.pop(0), refs.pop(0)
    r_ref = refs.pop(0) if has_res else None
    ssin_ref = refs.pop(0) if has_ss else None
    side_in = [(refs.pop(0), refs.pop(0) if has_gain else None) for has_gain, _ in side]
    o_ref = refs.pop(0)
    o16_ref, ssout_ref = (refs.pop(0), refs.pop(0)) if emit_norm else (None, None)

    for (sw_ref, g_ref), (_, transposed) in zip(side_in, side):
        so_ref = refs.pop(0)
        wv = sw_ref[0] if g_ref is None else sw_ref[0] * g_ref[...]
        so_ref[...] = (wv.T if transposed else wv).astype(so_ref.dtype)

    def finish(acc_of):
        tm = o_ref.shape[0]
        rc = min(tm, EPILOGUE_ROWS)
        parts = []
        for r0 in range(0, tm, rc):
            rows = slice(r0, r0 + rc)
            acc = acc_of(rows)
            if has_ss:
                acc = acc * _row_rms_scale(ssin_ref[rows, :], d_norm)
            if act == "relu2":
                acc = jnp.square(jnp.maximum(acc, 0.0))
            if has_res:
                acc = r_ref[rows, :] + acc
            o_ref[rows, :] = acc.astype(o_ref.dtype)
            if emit_norm:
                o16_ref[rows, :] = acc.astype(BF16)
                parts.append(_lane_partial_sq(acc))
        if emit_norm:
            part = jnp.concatenate(parts, axis=0) if len(parts) > 1 else parts[0]
            j = pl.program_id(1)

            @pl.when(j == 0)
            def _():
                ssout_ref[...] = part

            @pl.when(j > 0)
            def _():
                ssout_ref[...] += part

    if nk == 1:
        full = jnp.dot(a_ref[...], w_ref[...], preferred_element_type=F32)
        finish(lambda rows: full[rows, :])
    else:
        acc_ref = refs.pop(0)
        k = pl.program_id(2)

        @pl.when(k == 0)
        def _():
            acc_ref[...] = jnp.zeros_like(acc_ref)

        acc_ref[...] += jnp.dot(a_ref[...], w_ref[...], preferred_element_type=F32)

        @pl.when(k == nk - 1)
        def _():
            finish(lambda rows: acc_ref[rows, :])


def _matmul(a, w, *, res=None, act=None, row_ss=None, emit_norm=False, side_casts=(), out_dtype=F32,
            name="matmul"):
    m, kdim = a.shape
    n = w.shape[1]
    tm = _tile(m, MM_ROWS)
    deep = kdim <= MM_DEPTH or tm <= MM_SMALL_ROWS
    tk = _tile(kdim, MM_DEPTH if deep else MM_DEPTH_LONG, LANES)
    nk = kdim // tk
    tn = n if n < LANES else _tile(n, MM_COLS_EMIT if (emit_norm and nk == 1) else MM_COLS, LANES)
    nj = n // tn
    n_steps = (m // tm) * nj * nk
    row_blk = lambda width: pl.BlockSpec((tm, width), lambda i, j, k: (i, 0))
    out_blk = pl.BlockSpec((tm, tn), lambda i, j, k: (i, j))
    in_specs = [pl.BlockSpec((tm, tk), lambda i, j, k: (i, k)),
                pl.BlockSpec((tk, tn), lambda i, j, k: (k, j))]
    args = [a, w]
    if res is not None:
        in_specs.append(out_blk)
        args.append(res)
    if row_ss is not None:
        in_specs.append(row_blk(LANES))
        args.append(row_ss)
    out_specs, out_shape = [out_blk], [jax.ShapeDtypeStruct((m, n), out_dtype)]
    if emit_norm:
        out_specs += [out_blk, row_blk(LANES)]
        out_shape += [jax.ShapeDtypeStruct((m, n), BF16), jax.ShapeDtypeStruct((m, LANES), F32)]
    plans = [_side_plan(job, n_steps) for job in side_casts]
    side = []
    for job, plan in zip(side_casts, plans):
        if plan is None:
            continue
        n_blk, in_b, out_b, shape = plan
        blk = lambda i, j, k, rep=n_steps // n_blk, last=n_blk - 1: jnp.minimum(((i * nj + j) * nk + k) // rep, last)
        layer = job.layer
        in_specs.append(pl.BlockSpec(in_b, lambda i, j, k, blk=blk, layer=layer: (layer, blk(i, j, k), 0)))
        if job.transposed:
            gain_spec, gain_shape = pl.BlockSpec((1, shape[0]), lambda i, j, k: (0, 0)), (1, shape[0])
            out_specs.append(pl.BlockSpec(out_b, lambda i, j, k, blk=blk: (0, blk(i, j, k))))
        else:
            gain_spec = pl.BlockSpec((in_b[1], 1), lambda i, j, k, blk=blk: (blk(i, j, k), 0))
            gain_shape = (shape[0], 1)
            out_specs.append(pl.BlockSpec(out_b, lambda i, j, k, blk=blk: (blk(i, j, k), 0)))
        args.append(job.w)
        if job.gain is not None:
            in_specs.append(gain_spec)
            args.append(job.gain.reshape(gain_shape))
        out_shape.append(jax.ShapeDtypeStruct(shape, BF16))
        side.append((job.gain is not None, job.transposed))
    out = pl.pallas_call(
        functools.partial(_mm_kernel, nk=nk, act=act, has_res=res is not None,
                          has_ss=row_ss is not None, emit_norm=emit_norm, d_norm=kdim, side=tuple(side)),
        grid=(m // tm, nj, nk),
        in_specs=in_specs,
        out_specs=out_specs,
        out_shape=out_shape,
        scratch_shapes=[pltpu.VMEM((tm, tn), F32)] if nk > 1 else [],
        compiler_params=_params("arbitrary", "arbitrary", "arbitrary"),
        name=name,
    )(*args)
    out = list(out)
    n_main = 3 if emit_norm else 1
    hosted = iter(out[n_main:])
    casts = [next(hosted) if plan is not None else _run_cast(job) for job, plan in zip(side_casts, plans)]
    result = out[:n_main] + casts
    return result[0] if len(result) == 1 else tuple(result)


def _with_casts(result, n_casts):
    if not n_casts:
        return result, []
    main = result[:len(result) - n_casts]
    return (main[0] if len(main) == 1 else main), list(result[len(result) - n_casts:])


def _rope_table_kernel(inv_ref, c_ref, s_ref, *, pos0, hd):
    shape = c_ref.shape
    pos = (pos0 + lax.broadcasted_iota(jnp.int32, shape, 0)).astype(F32)
    ang = pos * inv_ref[...]
    d = jnp.bitwise_and(lax.broadcasted_iota(jnp.int32, shape, 1), hd - 1)
    rot = hd // 4
    cos, sin = jnp.cos(ang), jnp.sin(ang)
    c_ref[...] = jnp.where(d < rot, cos, 1.0)
    s_ref[...] = jnp.where(d < rot // 2, -sin, jnp.where(d < rot, sin, 0.0))


def _rope_tables(n_pos, pos0, hd):
    half = hd // 8
    inv = ROPE_THETA ** (-jnp.arange(half, dtype=F32) / half)
    inv_lane = jnp.tile(jnp.concatenate([inv, inv, jnp.zeros((hd - 2 * half,), F32)]), LANES // hd)
    shp = jax.ShapeDtypeStruct((n_pos, LANES), F32)
    return pl.pallas_call(
        functools.partial(_rope_table_kernel, pos0=pos0, hd=hd),
        out_shape=(shp, shp),
        name="rope_tables",
    )(inv_lane.reshape(1, LANES))


def _rope(x, c, s, hd):
    w = x.shape[1]
    reps = w // LANES
    cf = jnp.concatenate([c] * reps, axis=1) if reps > 1 else c
    sf = jnp.concatenate([s] * reps, axis=1) if reps > 1 else s
    d = jnp.bitwise_and(lax.broadcasted_iota(jnp.int32, x.shape, 1), hd - 1)
    half = hd // 8
    fwd = pltpu.roll(x, w - half, 1)
    bwd = pltpu.roll(x, half, 1)
    sw = jnp.where(d < half, fwd, jnp.where(d < 2 * half, bwd, 0.0))
    return x * cf + sw * sf


def _rope_k_kernel(k_ref, c_ref, s_ref, o_ref, *, hd):
    o_ref[0] = _rope(k_ref[0].astype(F32), c_ref[...], s_ref[...], hd)


def _rope_k(proj, cos_t, sin_t, col0, width, hd):
    n, t, _ = proj.shape
    tb = _tile(t, 512)
    return pl.pallas_call(
        functools.partial(_rope_k_kernel, hd=hd),
        grid=(n, t // tb),
        in_specs=[pl.BlockSpec((1, tb, width), lambda i, j: (i, j, col0 // width)),
                  pl.BlockSpec((tb, LANES), lambda i, j: (j, 0)),
                  pl.BlockSpec((tb, LANES), lambda i, j: (j, 0))],
        out_specs=pl.BlockSpec((1, tb, width), lambda i, j: (i, j, 0)),
        out_shape=jax.ShapeDtypeStruct((n, t, width), F32),
        compiler_params=_params("parallel", "parallel"),
        name="rope_k",
    )(proj, cos_t, sin_t)


def _attn_kernel(sink_ref, q_ref, k_ref, v_ref, c_ref, s_ref, o_ref, *,
                 rows, band, window, n_kv, group, hd):
    tb = q_ref.shape[1]
    gw = group * hd
    n_groups = tb // rows
    each = lambda f, *cols: [f(*x) for x in zip(*cols)]
    units = [(ci, h) for ci in range(n_groups) for h in range(n_kv)]
    sinks = [jnp.concatenate([jnp.full((rows, 1), sink_ref[h * group + g], F32) for g in range(group)],
                             axis=0) for h in range(n_kv)]
    ksl, valid = [], []
    for ci in range(n_groups):
        if window:
            lo = (pl.program_id(1) * n_groups + ci) * rows - window
            start = pl.multiple_of(jnp.maximum(lo, 0), rows)
            kpos = start + lax.broadcasted_iota(jnp.int32, (1, band), 1)
            valid.append(jnp.logical_and(kpos >= lo, kpos < lo + band))
            ksl.append(pl.ds(start, band))
        else:
            ksl.append(slice(0, band))
    qh = [_rope(q_ref[0, ci * rows:(ci + 1) * rows, h * gw:(h + 1) * gw].astype(F32),
                c_ref[ci * rows:(ci + 1) * rows, :], s_ref[ci * rows:(ci + 1) * rows, :], hd)
          for ci, h in units]
    qs = each(lambda x: jnp.concatenate([x[:, g * hd:(g + 1) * hd] for g in range(group)],
                                        axis=0).astype(BF16), qh)
    kh = [k_ref[0, ksl[ci], h * hd:(h + 1) * hd].astype(BF16) for ci, h in units]
    vh = [v_ref[0, ksl[ci], h * hd:(h + 1) * hd].astype(BF16) for ci, h in units]
    sc = each(lambda a, b: lax.dot_general(a, b, (((1,), (1,)), ((), ())),
                                           preferred_element_type=F32) * (hd ** -0.5), qs, kh)
    if window:
        sc = [jnp.where(valid[ci], x, NEG_INF) for (ci, _), x in zip(units, sc)]
    sk = [sinks[h] for _, h in units]
    m = each(lambda x, s: jnp.maximum(jnp.max(x, axis=-1, keepdims=True), s), sc, sk)
    p = each(lambda x, mx: jnp.exp(x - mx).astype(BF16), sc, m)
    ones = jnp.ones((band, hd), BF16)
    denom = each(lambda x, s, mx: jnp.dot(x, ones, preferred_element_type=F32) + jnp.exp(s - mx), p, sk, m)
    o = each(lambda x, v, d: jnp.dot(x, v, preferred_element_type=F32) / d, p, vh, denom)
    for (ci, h), x in zip(units, o):
        o_ref[0, ci * rows:(ci + 1) * rows, h * gw:(h + 1) * gw] = jnp.concatenate(
            [x[g * rows:(g + 1) * rows, :] for g in range(group)], axis=1).astype(o_ref.dtype)


def _attention(proj, k_src, k_col, v_src, v_col, cos_t, sin_t, sinks, *, aw, kvw, hd, out_width, window):
    n, t, _ = proj.shape
    tk = k_src.shape[1]
    n_kv = kvw // hd
    group = aw // kvw
    if window:
        rows, band = CHUNK, window + CHUNK
        tb = _tile(t, 4 * CHUNK, CHUNK)
    else:
        rows, band, tb = t, tk, t
    return pl.pallas_call(
        functools.partial(_attn_kernel, rows=rows, band=band, window=window,
                          n_kv=n_kv, group=group, hd=hd),
        grid=(n, t // tb),
        in_specs=[pl.BlockSpec(memory_space=pltpu.SMEM),
                  pl.BlockSpec((1, tb, aw), lambda i, j: (i, j, 0)),
                  pl.BlockSpec((1, tk, kvw), lambda i, j: (i, 0, k_col // kvw)),
                  pl.BlockSpec((1, tk, kvw), lambda i, j: (i, 0, v_col // kvw)),
                  pl.BlockSpec((tb, LANES), lambda i, j: (j, 0)),
                  pl.BlockSpec((tb, LANES), lambda i, j: (j, 0))],
        out_specs=pl.BlockSpec((1, tb, aw), lambda i, j: (i, j, 0)),
        out_shape=jax.ShapeDtypeStruct((n, t, out_width), BF16),
        compiler_params=_params("parallel", "parallel"),
        name="swa_attention",
    )(sinks, proj, k_src, v_src, cos_t, sin_t)


def _sconv_kernel(b_ref, c_ref, x_ref, st_ref, w_ref, mix_ref, z_ref, ns_ref, buf):
    del mix_ref
    t = x_ref.shape[1]
    width = w_ref.shape[0]
    buf[SUBLANES:SUBLANES + t, :] = c_ref[0].astype(F32) * x_ref[0].astype(F32)
    buf[SUBLANES - (width - 1):SUBLANES, :] = st_ref[0]
    lo = SUBLANES - (width - 1)
    acc = buf[lo:lo + t, :] * w_ref[0:1, :]
    for i in range(1, width):
        acc = acc + buf[lo + i:lo + i + t, :] * w_ref[i:i + 1, :]
    z_ref[0] = (b_ref[0].astype(F32) * acc).astype(z_ref.dtype)
    ns_ref[0] = buf[SUBLANES + t - (width - 1):SUBLANES + t, :]


def _sconv(proj, state, w, mix, col_b, col_c, col_x, col_out):
    n, t, _ = proj.shape
    width, ch = w.shape
    cols = (ch, col_b, col_c, col_x, col_out)
    cb = next(c for c in (2048, 1024, 512, 256, LANES)
              if all(v % c == 0 for v in cols) and t * c * 4 <= SCONV_BLOCK_BYTES)
    blk = lambda off: pl.BlockSpec((1, t, cb), lambda i, j: (i, 0, off // cb + j))
    return pl.pallas_call(
        _sconv_kernel,
        grid=(n, ch // cb),
        in_specs=[blk(col_b), blk(col_c), blk(col_x),
                  pl.BlockSpec((1, width - 1, cb), lambda i, j: (i, 0, j)),
                  pl.BlockSpec((width, cb), lambda i, j: (0, j)),
                  pl.BlockSpec(memory_space=pl.ANY)],
        out_specs=[blk(col_out),
                   pl.BlockSpec((1, width - 1, cb), lambda i, j: (i, 0, j))],
        out_shape=[jax.ShapeDtypeStruct(mix.shape, mix.dtype),
                   jax.ShapeDtypeStruct((n, width - 1, ch), F32)],
        scratch_shapes=[pltpu.VMEM((t + SUBLANES, cb), F32)],
        input_output_aliases={5: 0},
        compiler_params=_params("parallel", "parallel"),
        name="sconv",
    )(proj, proj, proj, state, w, mix)


def _softplus(x):
    return jnp.maximum(x, 0.0) + jnp.log1p(jnp.exp(-jnp.abs(x)))


def _sigmoid(x):
    return 0.5 * (jnp.tanh(0.5 * x) + 1.0)


def _silu(x):
    h = 0.5 * x
    return h + h * jnp.tanh(h)


def _bdot(a, b):
    return jnp.dot(a.astype(BF16), b.astype(BF16), preferred_element_type=F32)


def _gdn_kernel(x16_ref, ss_ref, q_ref, k_ref, v_ref, gate_ref, cst_ref, cw_ref, wab_ref, wabt_ref,
                al_ref, alt_ref, dt_ref, dtt_ref, s0_ref, ow_ref, o_ref, sout_ref, s_scr, buf):
    nh = al_ref.shape[1]
    ell, hw = q_ref.shape[1], q_ref.shape[2]
    hd = hw // nh
    width = cw_ref.shape[0]
    lo = SUBLANES - (width - 1)
    cidx = pl.program_id(1)
    heads = range(nh)
    each = lambda f, *cols: [f(*x) for x in zip(*cols)]
    nt = (((1,), (1,)), ((), ()))
    tn = (((0,), (0,)), ((), ()))
    dot_nt = lambda a, b: lax.dot_general(a, b, nt, preferred_element_type=F32)
    dot_nn = lambda a, b: jnp.dot(a, b, preferred_element_type=F32)
    to16 = lambda a: a.astype(BF16)
    l2n = lambda y: y * lax.rsqrt(jnp.sum(y * y, axis=-1, keepdims=True) + EPS)

    def conv_silu(c):
        buf[c, SUBLANES:SUBLANES + ell, :] = (q_ref, k_ref, v_ref)[c // nh][0, :, (c % nh) * hd:(c % nh + 1) * hd]
        acc = buf[c, lo:lo + ell, :] * cw_ref[0:1, c * hd:(c + 1) * hd]
        for t in range(1, width):
            acc = acc + buf[c, lo + t:lo + t + ell, :] * cw_ref[t:t + 1, c * hd:(c + 1) * hd]
        buf[c, lo:SUBLANES, :] = buf[c, SUBLANES + ell - (width - 1):SUBLANES + ell, :]
        return _silu(acc)

    @pl.when(cidx == 0)
    def _():
        s_scr[...] = s0_ref[0]
        for c in range(3 * nh):
            buf[c, lo:SUBLANES, :] = cst_ref[0, :, c * hd:(c + 1) * hd]

    i = lax.broadcasted_iota(jnp.int32, (ell, ell), 0)
    j = lax.broadcasted_iota(jnp.int32, (ell, ell), 1)
    hx = x16_ref[0]
    r_col = _row_rms_scale(ss_ref[0], hx.shape[1])
    r_row = jnp.sum(jnp.where(i == j, jnp.broadcast_to(r_col, (ell, ell)), 0.0), axis=0, keepdims=True)
    ab = dot_nn(hx, wab_ref[...]) * r_col
    abt = dot_nt(wabt_ref[...], hx) * r_row
    cum_cols = jnp.dot((i >= j).astype(F32), -jnp.exp(al_ref[...]) * _softplus(ab[:, :nh] + dt_ref[...]),
                       preferred_element_type=F32, precision=lax.Precision.HIGHEST)
    cum_rows = jnp.dot(-jnp.exp(alt_ref[...]) * _softplus(abt[:nh, :] + dtt_ref[...]), (i <= j).astype(F32),
                       preferred_element_type=F32, precision=lax.Precision.HIGHEST)
    beta_cols = _sigmoid(ab[:, nh:])
    beta_rows = _sigmoid(abt[nh:, :])
    g_row = [cum_rows[h:h + 1, :] for h in heads]
    b_row = [beta_rows[h:h + 1, :] for h in heads]
    g_col = [jnp.broadcast_to(cum_cols[:, h:h + 1], (ell, hd)) for h in heads]
    b_col = [jnp.broadcast_to(beta_cols[:, h:h + 1], (ell, ell)) for h in heads]

    q = [l2n(conv_silu(h)) * (hd ** -0.5) for h in heads]
    k = [l2n(conv_silu(nh + h)) for h in heads]
    v = [conv_silu(2 * nh + h) for h in heads]
    k16 = each(to16, k)
    kk = each(dot_nt, k16, k16)
    qk = each(dot_nt, each(to16, q), k16)
    decay = each(lambda gc, gr: jnp.exp(jnp.where(i >= j, gc[:, :ell] - gr, -jnp.inf)), g_col, g_row)
    a = each(lambda bc, x, d: jnp.where(i > j, bc * x * d, 0.0), b_col, kk, decay)
    eye = (i == j).astype(F32)
    shifts = range(3, ell.bit_length())
    blk = [jnp.right_shift(i, sh) == jnp.right_shift(j, sh) for sh in shifts]
    p = each(lambda x: -jnp.where(blk[0], x, 0.0), a)
    tm = each(lambda x: eye + x, p)
    for _ in range(2):
        p = each(_bdot, p, p)
        tm = each(lambda t, x: t + _bdot(t, x), tm, p)
    for lvl in range(1, len(blk)):
        ring = jnp.logical_and(blk[lvl], jnp.logical_not(blk[lvl - 1]))
        tl = each(lambda t, x: _bdot(t, jnp.where(ring, x, 0.0)), tm, a)
        tm = each(lambda t, x: t - _bdot(x, t), tm, tl)
    u_base = each(lambda t, b, x: _bdot(t * b, x), tm, b_row, v)
    w16 = each(lambda t, b, g, x: to16(_bdot(t * (b * jnp.exp(g)), x)), tm, b_row, g_row, k16)
    qd16 = each(lambda x, g: to16(x * jnp.exp(g)), q, g_col)
    kd16 = each(lambda x, g: to16(x * jnp.exp(g[ell - 1:ell, :] - g)), k, g_col)
    aqk16 = each(lambda x, d: to16(x * d), qk, decay)

    s = [s_scr[h] for h in heads]
    s16 = each(to16, s)
    u16 = each(lambda ub, w, x: to16(ub - dot_nn(w, x)), u_base, w16, s16)
    o_s = each(dot_nn, qd16, s16)
    o_u = each(dot_nn, aqk16, u16)
    ds = each(lambda kd, x: lax.dot_general(kd, x, tn, preferred_element_type=F32), kd16, u16)
    for h in heads:
        s_scr[h] = s[h] * jnp.exp(g_col[h][ell - 1:ell, :]) + ds[h]
        o = o_s[h] + o_u[h]
        o = o * lax.rsqrt(jnp.mean(o * o, axis=-1, keepdims=True) + EPS) * ow_ref[...]
        gt = gate_ref[0, :, h * hd:(h + 1) * hd]
        o_ref[0, :, h * hd:(h + 1) * hd] = (o * _silu(gt)).astype(o_ref.dtype)

    @pl.when(cidx == pl.num_programs(1) - 1)
    def _():
        sout_ref[0] = s_scr[...]


def _gdn_mixer(x16, ss, proj, conv_state, conv_w, w_ab, a_log, dt_bias, s0, o_norm_w, ell):
    n, t, d = x16.shape
    nh, hd = a_log.shape[0], o_norm_w.shape[0]
    hw = nh * hd
    width = conv_w.shape[0]
    col = lambda c: pl.BlockSpec((1, ell, hw), lambda i, j: (i, j, c))
    full = lambda shape: pl.BlockSpec(shape, lambda i, j: (0,) * len(shape))
    st_spec = pl.BlockSpec((1, nh, hd, hd), lambda i, j: (i, 0, 0, 0))
    return pl.pallas_call(
        _gdn_kernel,
        grid=(n, t // ell),
        in_specs=[pl.BlockSpec((1, ell, d), lambda i, j: (i, j, 0)),
                  pl.BlockSpec((1, ell, LANES), lambda i, j: (i, j, 0)), col(0), col(1), col(2), col(3),
                  pl.BlockSpec((1, width - 1, 3 * hw), lambda i, j: (i, 0, 0)), full((width, 3 * hw)),
                  full((d, 2 * nh)), full((2 * nh, d)),
                  full((1, nh)), full((nh, 1)), full((1, nh)), full((nh, 1)),
                  st_spec, full((1, hd))],
        out_specs=[pl.BlockSpec((1, ell, hw), lambda i, j: (i, j, 0)), st_spec],
        out_shape=[jax.ShapeDtypeStruct((n, t, hw), BF16),
                   jax.ShapeDtypeStruct((n, nh, hd, hd), F32)],
        scratch_shapes=[pltpu.VMEM((nh, hd, hd), F32), pltpu.VMEM((3 * nh, ell + SUBLANES, hd), F32)],
        compiler_params=_params("parallel", "arbitrary"),
        name="gdn_mixer",
    )(x16, ss, proj, proj, proj, proj, conv_state, conv_w, w_ab, w_ab.T,
      a_log.reshape(1, nh), a_log.reshape(nh, 1), dt_bias.reshape(1, nh), dt_bias.reshape(nh, 1),
      s0, o_norm_w.reshape(1, hd))


def _swa_sconv_layer(stream, n, t, pos0, k_cache, v_cache, conv_state, w_in, sinks, conv_w, w_out,
                     side_casts=()):
    x, x16, ss = stream
    hosted_out = isinstance(w_out, CastJob)
    jobs = list(side_casts) + ([w_out] if hosted_out else [])
    ch = conv_w.shape[1]
    aw = (w_out.w.shape[1] if hosted_out else w_out.shape[0]) - ch
    kvw = (w_in.shape[1] - aw - 3 * ch) // 2
    hd = aw // sinks.shape[0]
    proj, casts = _with_casts(_matmul(x16, w_in, row_ss=ss, side_casts=jobs, out_dtype=BF16,
                                      name="in_proj_a"), len(jobs))
    if hosted_out:
        w_out = casts.pop()
    proj = proj.reshape(n, t, -1)
    cos_t, sin_t = _rope_tables(t, pos0, hd)
    k_new = _rope_k(proj, cos_t, sin_t, aw, kvw, hd)
    v_new = proj[:, t - min(t, WINDOW):, aw + kvw:aw + 2 * kvw].astype(F32)
    attn = functools.partial(_attention, cos_t=cos_t, sin_t=sin_t, sinks=sinks,
                             aw=aw, kvw=kvw, hd=hd, out_width=aw + ch)
    if k_cache is None:
        k_all, v_win = k_new, v_new
        mix = attn(proj, k_new, 0, proj, aw + kvw, window=WINDOW)
    else:
        k_all = jnp.concatenate([k_cache, k_new], axis=1)
        v_all = jnp.concatenate([v_cache, v_new], axis=1)
        v_win = v_all
        mix = attn(proj, k_all, 0, v_all, 0, window=0)
    col_b = aw + 2 * kvw
    mix, new_state = _sconv(proj, conv_state, conv_w, mix, col_b, col_b + ch, col_b + 2 * ch, aw)
    stream = _matmul(mix.reshape(n * t, aw + ch), w_out, res=x, emit_norm=True, name="out_proj_a")
    return stream, k_all[:, -WINDOW:], v_win[:, -WINDOW:], new_state, casts, w_out


def _gdn_layer(stream, n, t, conv_state, s0, w_in, w_ab, conv_w, a_log, dt_bias, o_norm_w, w_out, ell,
               side_casts=()):
    x, x16, ss = stream
    d = x.shape[1]
    hw = a_log.shape[0] * o_norm_w.shape[0]
    hosted_out = isinstance(w_out, CastJob)
    jobs = list(side_casts) + ([w_out] if hosted_out else [])
    proj, casts = _with_casts(_matmul(x16, w_in, row_ss=ss, side_casts=jobs, name="in_proj_g"), len(jobs))
    if hosted_out:
        w_out = casts.pop()
    proj = proj.reshape(n, t, 4 * hw)
    o, s_new = _gdn_mixer(x16.reshape(n, t, d), ss.reshape(n, t, LANES), proj, conv_state, conv_w, w_ab,
                          a_log, dt_bias, s0, o_norm_w, ell)
    stream = _matmul(o.reshape(n * t, hw), w_out, res=x, emit_norm=True, name="out_proj_g")
    width = conv_w.shape[0]
    return stream, proj[:, t - (width - 1):, :3 * hw], s_new, casts, w_out


def _mlp(stream, w_up, w_down, last, side_casts=()):
    x, x16, ss = stream
    jobs = ([w_down] if isinstance(w_down, CastJob) else []) + list(side_casts)
    hid, casts = _with_casts(_matmul(x16, w_up, act="relu2", row_ss=ss, side_casts=jobs,
                                     out_dtype=BF16, name="mlp_up"), len(jobs))
    if isinstance(w_down, CastJob):
        w_down = casts.pop(0)
    if last:
        return (_matmul(hid, w_down, res=x, name="mlp_down"), None, None), w_down, casts
    return _matmul(hid, w_down, res=x, emit_norm=True, name="mlp_down"), w_down, casts


def kernel(x_prompt, x_sample, cache_swa_k, cache_swa_v, state_sconv, state_dn_conv, state_dn,
           attn_norm, w_in_a, sinks, sconv_w, w_out_a,
           dn_norm, w_in_g, dn_conv_w, A_log, dt_bias, o_norm_w, w_out_g,
           mlp_norm, w_up, w_down, final_norm):
    (nb, tp, d), (nd, ts, _) = x_prompt.shape, x_sample.shape
    depth = mlp_norm.shape[0]
    n_kv, hd_a = cache_swa_k.shape[3], cache_swa_k.shape[4]
    nh_g, hd_g = state_dn.shape[2], state_dn.shape[3]
    sp = _norm_prep(x_prompt.reshape(nb * tp, d))
    ss = _norm_prep(x_sample.reshape(nd * ts, d))
    outs = [[] for _ in range(10)]
    hw_g = nh_g * hd_g
    w_in_g_t = jnp.swapaxes(w_in_g, 1, 2)

    def in_proj_job(li):
        if li % 2 == 0:
            return CastJob(w_in_a, li // 2, attn_norm[li // 2])
        return CastJob(w_in_g_t, li // 2, dn_norm[li // 2], True, 4 * hw_g)

    w_in = _run_cast(in_proj_job(0))
    for li in range(depth):
        j = li // 2
        up_job = [CastJob(w_up, li, mlp_norm[li])]
        next_job = [in_proj_job(li + 1)] if li + 1 < depth else []
        if li % 2 == 0:
            zero_state = jnp.zeros((nb,) + state_sconv.shape[2:], F32)
            sp, kp, vp, cp, (wu,), w_out = _swa_sconv_layer(sp, nb, tp, 0, None, None, zero_state, w_in,
                                                            sinks[j], sconv_w[j], CastJob(w_out_a, j), up_job)
            kc = cache_swa_k[j].reshape(nd, -1, n_kv * hd_a)
            vc = cache_swa_v[j].reshape(nd, -1, n_kv * hd_a)
            ss, ks, vs, cs, _, _ = _swa_sconv_layer(ss, nd, ts, PAST_LEN, kc, vc, state_sconv[j], w_in,
                                                    sinks[j], sconv_w[j], w_out)
            shape5 = lambda a: a.reshape(a.shape[0], a.shape[1], n_kv, hd_a)
            for lst, val in zip(outs[:6], (shape5(kp), shape5(vp), shape5(ks), shape5(vs), cp, cs)):
                lst.append(val)
        else:
            w_ab = _to_bf16(w_in_g_t, j, 4 * hw_g, gain=dn_norm[j], transposed=True)
            zero_conv = jnp.zeros((nb,) + state_dn_conv.shape[2:], F32)
            zero_s = jnp.zeros((nb, nh_g, hd_g, hd_g), F32)
            sp, dcp, dsp, (wu,), w_out = _gdn_layer(sp, nb, tp, zero_conv, zero_s, w_in, w_ab, dn_conv_w[j],
                                                    A_log[j], dt_bias[j], o_norm_w[j], CastJob(w_out_g, j),
                                                    CHUNK, up_job)
            ss, dcs, dss, _, _ = _gdn_layer(ss, nd, ts, state_dn_conv[j], state_dn[j], w_in, w_ab, dn_conv_w[j],
                                            A_log[j], dt_bias[j], o_norm_w[j], w_out, ts)
            for lst, val in zip(outs[6:], (dcp, dcs, dsp, dss)):
                lst.append(val)
        last = li == depth - 1
        sp, wd, nxt = _mlp(sp, wu, CastJob(w_down, li), last, next_job)
        ss, _, _ = _mlp(ss, wu, wd, last)
        w_in = nxt[0] if nxt else None
    y_prompt = _rmsnorm(sp[0], final_norm, F32).reshape(x_prompt.shape)
    y_sample = _rmsnorm(ss[0], final_norm, F32).reshape(x_sample.shape)
    return (y_prompt, y_sample) + tuple(jnp.stack(o, 0) for o in outs)
```

```python
import functools
from typing import NamedTuple, Optional

import jax
import jax.numpy as jnp
from jax import lax
from jax.experimental import pallas as pl
from jax.experimental.pallas import tpu as pltpu

EPS = 1e-6
CHUNK = 64
WINDOW = 128
PAST_LEN = 4096
ROPE_THETA = 500000.0
NEG_INF = -1e30
BF16 = jnp.bfloat16
F32 = jnp.float32

LANES = 128
SUBLANES = 8
BF16_ROWS = 16
MIB = 1024 * 1024
VMEM_LIMIT = 56 * MIB
MM_ROWS = 1024
MM_COLS = 1024
MM_COLS_EMIT = 1024
MM_DEPTH = 4096
MM_DEPTH_LONG = 2048
MM_SMALL_ROWS = 256
EPILOGUE_ROWS = 256
ROW_TILE = 256
CAST_BLOCK_BYTES = 4 * MIB
SIDE_BLOCK_BYTES = 2 * MIB
SCONV_BLOCK_BYTES = 2 * MIB


def _params(*sem):
    return pltpu.CompilerParams(dimension_semantics=sem, vmem_limit_bytes=VMEM_LIMIT)


def _tile(n, pref, align=SUBLANES):
    if n <= pref:
        return n
    t = pref - pref % align
    while t >= align:
        if n % t == 0:
            return t
        t -= align
    return n


def _lane_partial_sq(x):
    sq = x * x
    part = sq[:, :LANES]
    for c in range(1, sq.shape[1] // LANES):
        part = part + sq[:, c * LANES:(c + 1) * LANES]
    return part


def _row_rms_scale(ss, d):
    return lax.rsqrt(jnp.sum(ss, axis=-1, keepdims=True) * (1.0 / d) + EPS)


def _rmsnorm_kernel(x_ref, g_ref, o_ref):
    x = x_ref[...]
    ms = jnp.mean(x * x, axis=-1, keepdims=True)
    o_ref[...] = (x * lax.rsqrt(ms + EPS) * g_ref[...]).astype(o_ref.dtype)


def _rmsnorm(x, g, out_dtype):
    m, d = x.shape
    tm = _tile(m, ROW_TILE)
    return pl.pallas_call(
        _rmsnorm_kernel,
        grid=(m // tm,),
        in_specs=[pl.BlockSpec((tm, d), lambda i: (i, 0)),
                  pl.BlockSpec((1, d), lambda i: (0, 0))],
        out_specs=pl.BlockSpec((tm, d), lambda i: (i, 0)),
        out_shape=jax.ShapeDtypeStruct((m, d), out_dtype),
        compiler_params=_params("parallel"),
        name="rmsnorm",
    )(x, g.reshape(1, d))


def _norm_prep_kernel(x_ref, o16_ref, ss_ref):
    x = x_ref[...]
    o16_ref[...] = x.astype(BF16)
    ss_ref[...] = _lane_partial_sq(x)


def _norm_prep(x):
    m, d = x.shape
    tm = _tile(m, ROW_TILE)
    x16, ss = pl.pallas_call(
        _norm_prep_kernel,
        grid=(m // tm,),
        in_specs=[pl.BlockSpec((tm, d), lambda i: (i, 0))],
        out_specs=[pl.BlockSpec((tm, d), lambda i: (i, 0)), pl.BlockSpec((tm, LANES), lambda i: (i, 0))],
        out_shape=[jax.ShapeDtypeStruct((m, d), BF16), jax.ShapeDtypeStruct((m, LANES), F32)],
        compiler_params=_params("parallel"),
        name="norm_prep",
    )(x)
    return x, x16, ss


class CastJob(NamedTuple):
    w: jax.Array
    layer: int
    gain: Optional[jax.Array] = None
    transposed: bool = False
    n_cols: Optional[int] = None


def _cast_kernel(*refs, has_gain, transposed):
    w = refs[0][0]
    if has_gain:
        w = w * refs[1][...]
    refs[-1][...] = (w.T if transposed else w).astype(refs[-1].dtype)


def _to_bf16(w, layer, col0=0, n_cols=None, gain=None, transposed=False):
    kdim, n = (w.shape[2], w.shape[1]) if transposed else (w.shape[1], w.shape[2])
    n_cols = n - col0 if n_cols is None else n_cols
    tc = _tile(n_cols, 2048, LANES) if n_cols >= LANES else n_cols
    tr = _tile(kdim, max(512, CAST_BLOCK_BYTES // (4 * tc)), LANES)
    assert col0 % tc == 0 and (transposed or tc % LANES == 0)
    if transposed:
        in_specs = [pl.BlockSpec((1, tc, tr), lambda i, j: (layer, col0 // tc + j, i))]
        gain_spec, gain_shape = pl.BlockSpec((1, tr), lambda i, j: (0, i)), (1, kdim)
    else:
        in_specs = [pl.BlockSpec((1, tr, tc), lambda i, j: (layer, i, col0 // tc + j))]
        gain_spec, gain_shape = pl.BlockSpec((tr, 1), lambda i, j: (i, 0)), (kdim, 1)
    args = [w]
    if gain is not None:
        in_specs.append(gain_spec)
        args.append(gain.reshape(gain_shape))
    return pl.pallas_call(
        functools.partial(_cast_kernel, has_gain=gain is not None, transposed=transposed),
        grid=(kdim // tr, n_cols // tc),
        in_specs=in_specs,
        out_specs=pl.BlockSpec((tr, tc), lambda i, j: (i, j)),
        out_shape=jax.ShapeDtypeStruct((kdim, n_cols), BF16),
        compiler_params=_params("parallel", "parallel"),
        name="weight_cast",
    )(*args)


def _run_cast(job):
    return _to_bf16(job.w, job.layer, 0, job.n_cols, gain=job.gain, transposed=job.transposed)


def _side_plan(job, n_steps):
    if job.transposed:
        n_cols, kdim = job.n_cols or job.w.shape[1], job.w.shape[2]
        n_blk, in_blk, out_blk = n_cols // LANES, (1, LANES, kdim), (kdim, LANES)
        if n_cols % LANES or kdim % LANES:
            return None
    else:
        kdim, n_cols = job.w.shape[1], job.n_cols or job.w.shape[2]
        n_blk = 1
        while n_blk * 2 <= n_steps and kdim % (n_blk * 2 * BF16_ROWS) == 0:
            n_blk *= 2
        in_blk, out_blk = (1, kdim // n_blk, n_cols), (kdim // n_blk, n_cols)
        if n_cols != job.w.shape[2]:
            return None
    if n_blk > n_steps or 4 * in_blk[1] * in_blk[2] > SIDE_BLOCK_BYTES:
        return None
    return n_blk, in_blk, out_blk, (kdim, n_cols)


def _mm_kernel(*refs, nk, act, has_res, has_ss, emit_norm, d_norm, side):
    refs = list(refs)
    a_ref, w_ref = refs.pop(0), refs.pop(0)
    r_ref = refs.pop(0) if has_res else None
    ssin_ref = refs.pop(0) if has_ss else None
    side_in = [(refs.pop(0), refs.pop(0) if has_gain else None) for has_gain, _ in side]
    o_ref = refs.pop(0)
    o16_ref, ssout_ref = (refs.pop(0), refs.pop(0)) if emit_norm else (None, None)

    for (sw_ref, g_ref), (_, transposed) in zip(side_in, side):
        so_ref = refs.pop(0)
        wv = sw_ref[0] if g_ref is None else sw_ref[0] * g_ref[...]
        so_ref[...] = (wv.T if transposed else wv).astype(so_ref.dtype)

    def finish(acc_of):
        tm = o_ref.shape[0]
        rc = min(tm, EPILOGUE_ROWS)
        parts = []
        for r0 in range(0, tm, rc):
            rows = slice(r0, r0 + rc)
            acc = acc_of(rows)
            if has_ss:
                acc = acc * _row_rms_scale(ssin_ref[rows, :], d_norm)
            if act == "relu2":
                acc = jnp.square(jnp.maximum(acc, 0.0))
            if has_res:
                acc = r_ref[rows, :] + acc
            o_ref[rows, :] = acc.astype(o_ref.dtype)
            if emit_norm:
                o16_ref[rows, :] = acc.astype(BF16)
                parts.append(_lane_partial_sq(acc))
        if emit_norm:
            part = jnp.concatenate(parts, axis=0) if len(parts) > 1 else parts[0]
            j = pl.program_id(1)

            @pl.when(j == 0)
            def _():
                ssout_ref[...] = part

            @pl.when(j > 0)
            def _():
                ssout_ref[...] += part

    if nk == 1:
        full = jnp.dot(a_ref[...], w_ref[...], preferred_element_type=F32)
        finish(lambda rows: full[rows, :])
    else:
        acc_ref = refs.pop(0)
        k = pl.program_id(2)

        @pl.when(k == 0)
        def _():
            acc_ref[...] = jnp.zeros_like(acc_ref)

        acc_ref[...] += jnp.dot(a_ref[...], w_ref[...], preferred_element_type=F32)

        @pl.when(k == nk - 1)
        def _():
            finish(lambda rows: acc_ref[rows, :])


def _matmul(a, w, *, res=None, act=None, row_ss=None, emit_norm=False, side_casts=(), out_dtype=F32,
            name="matmul"):
    m, kdim = a.shape
    n = w.shape[1]
    tm = _tile(m, MM_ROWS)
    deep = kdim <= MM_DEPTH or tm <= MM_SMALL_ROWS
    tk = _tile(kdim, MM_DEPTH if deep else MM_DEPTH_LONG, LANES)
    nk = kdim // tk
    tn = n if n < LANES else _tile(n, MM_COLS_EMIT if (emit_norm and nk == 1) else MM_COLS, LANES)
    nj = n // tn
    n_steps = (m // tm) * nj * nk
    row_blk = lambda width: pl.BlockSpec((tm, width), lambda i, j, k: (i, 0))
    out_blk = pl.BlockSpec((tm, tn), lambda i, j, k: (i, j))
    a_mode = pl.Buffered(1) if (emit_norm and nk == 1 and m // tm > 1) else None
    in_specs = [pl.BlockSpec((tm, tk), lambda i, j, k: (i, k), pipeline_mode=a_mode),
                pl.BlockSpec((tk, tn), lambda i, j, k: (k, j))]
    args = [a, w]
    if res is not None:
        in_specs.append(out_blk)
        args.append(res)
    if row_ss is not None:
        in_specs.append(row_blk(LANES))
        args.append(row_ss)
    out_specs, out_shape = [out_blk], [jax.ShapeDtypeStruct((m, n), out_dtype)]
    if emit_norm:
        out_specs += [out_blk, row_blk(LANES)]
        out_shape += [jax.ShapeDtypeStruct((m, n), BF16), jax.ShapeDtypeStruct((m, LANES), F32)]
    plans = [_side_plan(job, n_steps) for job in side_casts]
    side = []
    for job, plan in zip(side_casts, plans):
        if plan is None:
            continue
        n_blk, in_b, out_b, shape = plan
        blk = lambda i, j, k, rep=n_steps // n_blk, last=n_blk - 1: jnp.minimum(((i * nj + j) * nk + k) // rep, last)
        layer = job.layer
        in_specs.append(pl.BlockSpec(in_b, lambda i, j, k, blk=blk, layer=layer: (layer, blk(i, j, k), 0)))
        if job.transposed:
            gain_spec, gain_shape = pl.BlockSpec((1, shape[0]), lambda i, j, k: (0, 0)), (1, shape[0])
            out_specs.append(pl.BlockSpec(out_b, lambda i, j, k, blk=blk: (0, blk(i, j, k))))
        else:
            gain_spec = pl.BlockSpec((in_b[1], 1), lambda i, j, k, blk=blk: (blk(i, j, k), 0))
            gain_shape = (shape[0], 1)
            out_specs.append(pl.BlockSpec(out_b, lambda i, j, k, blk=blk: (blk(i, j, k), 0)))
        args.append(job.w)
        if job.gain is not None:
            in_specs.append(gain_spec)
            args.append(job.gain.reshape(gain_shape))
        out_shape.append(jax.ShapeDtypeStruct(shape, BF16))
        side.append((job.gain is not None, job.transposed))
    out = pl.pallas_call(
        functools.partial(_mm_kernel, nk=nk, act=act, has_res=res is not None,
                          has_ss=row_ss is not None, emit_norm=emit_norm, d_norm=kdim, side=tuple(side)),
        grid=(m // tm, nj, nk),
        in_specs=in_specs,
        out_specs=out_specs,
        out_shape=out_shape,
        scratch_shapes=[pltpu.VMEM((tm, tn), F32)] if nk > 1 else [],
        compiler_params=_params("arbitrary", "arbitrary", "arbitrary"),
        name=name,
    )(*args)
    out = list(out)
    n_main = 3 if emit_norm else 1
    hosted = iter(out[n_main:])
    casts = [next(hosted) if plan is not None else _run_cast(job) for job, plan in zip(side_casts, plans)]
    result = out[:n_main] + casts
    return result[0] if len(result) == 1 else tuple(result)


def _with_casts(result, n_casts):
    if not n_casts:
        return result, []
    main = result[:len(result) - n_casts]
    return (main[0] if len(main) == 1 else main), list(result[len(result) - n_casts:])


def _rope_table_kernel(inv_ref, c_ref, s_ref, *, pos0, hd):
    shape = c_ref.shape
    pos = (pos0 + lax.broadcasted_iota(jnp.int32, shape, 0)).astype(F32)
    ang = pos * inv_ref[...]
    d = jnp.bitwise_and(lax.broadcasted_iota(jnp.int32, shape, 1), hd - 1)
    rot = hd // 4
    cos, sin = jnp.cos(ang), jnp.sin(ang)
    c_ref[...] = jnp.where(d < rot, cos, 1.0)
    s_ref[...] = jnp.where(d < rot // 2, -sin, jnp.where(d < rot, sin, 0.0))


def _rope_tables(n_pos, pos0, hd):
    half = hd // 8
    inv = ROPE_THETA ** (-jnp.arange(half, dtype=F32) / half)
    inv_lane = jnp.tile(jnp.concatenate([inv, inv, jnp.zeros((hd - 2 * half,), F32)]), LANES // hd)
    shp = jax.ShapeDtypeStruct((n_pos, LANES), F32)
    return pl.pallas_call(
        functools.partial(_rope_table_kernel, pos0=pos0, hd=hd),
        out_shape=(shp, shp),
        name="rope_tables",
    )(inv_lane.reshape(1, LANES))


def _rope(x, c, s, hd):
    w = x.shape[1]
    reps = w // LANES
    cf = jnp.concatenate([c] * reps, axis=1) if reps > 1 else c
    sf = jnp.concatenate([s] * reps, axis=1) if reps > 1 else s
    d = jnp.bitwise_and(lax.broadcasted_iota(jnp.int32, x.shape, 1), hd - 1)
    half = hd // 8
    fwd = pltpu.roll(x, w - half, 1)
    bwd = pltpu.roll(x, half, 1)
    sw = jnp.where(d < half, fwd, jnp.where(d < 2 * half, bwd, 0.0))
    return x * cf + sw * sf


def _rope_k_kernel(k_ref, c_ref, s_ref, o_ref, *, hd):
    o_ref[0] = _rope(k_ref[0].astype(F32), c_ref[...], s_ref[...], hd)


def _rope_k(proj, cos_t, sin_t, col0, width, hd):
    n, t, _ = proj.shape
    tb = _tile(t, 512)
    return pl.pallas_call(
        functools.partial(_rope_k_kernel, hd=hd),
        grid=(n, t // tb),
        in_specs=[pl.BlockSpec((1, tb, width), lambda i, j: (i, j, col0 // width)),
                  pl.BlockSpec((tb, LANES), lambda i, j: (j, 0)),
                  pl.BlockSpec((tb, LANES), lambda i, j: (j, 0))],
        out_specs=pl.BlockSpec((1, tb, width), lambda i, j: (i, j, 0)),
        out_shape=jax.ShapeDtypeStruct((n, t, width), F32),
        compiler_params=_params("parallel", "parallel"),
        name="rope_k",
    )(proj, cos_t, sin_t)


def _attn_kernel(sink_ref, q_ref, k_ref, v_ref, c_ref, s_ref, o_ref, *,
                 rows, band, window, n_kv, group, hd):
    tb = q_ref.shape[1]
    gw = group * hd
    n_groups = tb // rows
    each = lambda f, *cols: [f(*x) for x in zip(*cols)]
    units = [(ci, h) for ci in range(n_groups) for h in range(n_kv)]
    sinks = [jnp.concatenate([jnp.full((rows, 1), sink_ref[h * group + g], F32) for g in range(group)],
                             axis=0) for h in range(n_kv)]
    ksl, valid = [], []
    for ci in range(n_groups):
        if window:
            lo = (pl.program_id(1) * n_groups + ci) * rows - window
            start = pl.multiple_of(jnp.maximum(lo, 0), rows)
            kpos = start + lax.broadcasted_iota(jnp.int32, (1, band), 1)
            valid.append(jnp.logical_and(kpos >= lo, kpos < lo + band))
            ksl.append(pl.ds(start, band))
        else:
            ksl.append(slice(0, band))
    qh = [_rope(q_ref[0, ci * rows:(ci + 1) * rows, h * gw:(h + 1) * gw].astype(F32),
                c_ref[ci * rows:(ci + 1) * rows, :], s_ref[ci * rows:(ci + 1) * rows, :], hd)
          for ci, h in units]
    qs = each(lambda x: jnp.concatenate([x[:, g * hd:(g + 1) * hd] for g in range(group)],
                                        axis=0).astype(BF16), qh)
    kh = [k_ref[0, ksl[ci], h * hd:(h + 1) * hd].astype(BF16) for ci, h in units]
    vh = [v_ref[0, ksl[ci], h * hd:(h + 1) * hd].astype(BF16) for ci, h in units]
    sc = each(lambda a, b: lax.dot_general(a, b, (((1,), (1,)), ((), ())),
                                           preferred_element_type=F32) * (hd ** -0.5), qs, kh)
    if window:
        sc = [jnp.where(valid[ci], x, NEG_INF) for (ci, _), x in zip(units, sc)]
    sk = [sinks[h] for _, h in units]
    m = each(lambda x, s: jnp.maximum(jnp.max(x, axis=-1, keepdims=True), s), sc, sk)
    p = each(lambda x, mx: jnp.exp(x - mx).astype(BF16), sc, m)
    ones = jnp.ones((band, hd), BF16)
    denom = each(lambda x, s, mx: jnp.dot(x, ones, preferred_element_type=F32) + jnp.exp(s - mx), p, sk, m)
    o = each(lambda x, v, d: jnp.dot(x, v, preferred_element_type=F32) / d, p, vh, denom)
    for (ci, h), x in zip(units, o):
        o_ref[0, ci * rows:(ci + 1) * rows, h * gw:(h + 1) * gw] = jnp.concatenate(
            [x[g * rows:(g + 1) * rows, :] for g in range(group)], axis=1).astype(o_ref.dtype)


def _attention(proj, k_src, k_col, v_src, v_col, cos_t, sin_t, sinks, *, aw, kvw, hd, out_width, window):
    n, t, _ = proj.shape
    tk = k_src.shape[1]
    n_kv = kvw // hd
    group = aw // kvw
    if window:
        rows, band = CHUNK, window + CHUNK
        tb = _tile(t, 4 * CHUNK, CHUNK)
    else:
        rows, band, tb = t, tk, t
    return pl.pallas_call(
        functools.partial(_attn_kernel, rows=rows, band=band, window=window,
                          n_kv=n_kv, group=group, hd=hd),
        grid=(n, t // tb),
        in_specs=[pl.BlockSpec(memory_space=pltpu.SMEM),
                  pl.BlockSpec((1, tb, aw), lambda i, j: (i, j, 0)),
                  pl.BlockSpec((1, tk, kvw), lambda i, j: (i, 0, k_col // kvw)),
                  pl.BlockSpec((1, tk, kvw), lambda i, j: (i, 0, v_col // kvw)),
                  pl.BlockSpec((tb, LANES), lambda i, j: (j, 0)),
                  pl.BlockSpec((tb, LANES), lambda i, j: (j, 0))],
        out_specs=pl.BlockSpec((1, tb, aw), lambda i, j: (i, j, 0)),
        out_shape=jax.ShapeDtypeStruct((n, t, out_width), BF16),
        compiler_params=_params("parallel", "parallel"),
        name="swa_attention",
    )(sinks, proj, k_src, v_src, cos_t, sin_t)


def _sconv_kernel(b_ref, c_ref, x_ref, st_ref, w_ref, mix_ref, z_ref, ns_ref, buf):
    del mix_ref
    t = x_ref.shape[1]
    width = w_ref.shape[0]
    buf[SUBLANES:SUBLANES + t, :] = c_ref[0].astype(F32) * x_ref[0].astype(F32)
    buf[SUBLANES - (width - 1):SUBLANES, :] = st_ref[0]
    lo = SUBLANES - (width - 1)
    acc = buf[lo:lo + t, :] * w_ref[0:1, :]
    for i in range(1, width):
        acc = acc + buf[lo + i:lo + i + t, :] * w_ref[i:i + 1, :]
    z_ref[0] = (b_ref[0].astype(F32) * acc).astype(z_ref.dtype)
    ns_ref[0] = buf[SUBLANES + t - (width - 1):SUBLANES + t, :]


def _sconv(proj, state, w, mix, col_b, col_c, col_x, col_out):
    n, t, _ = proj.shape
    width, ch = w.shape
    cols = (ch, col_b, col_c, col_x, col_out)
    cb = next(c for c in (2048, 1024, 512, 256, LANES)
              if all(v % c == 0 for v in cols) and t * c * 4 <= SCONV_BLOCK_BYTES)
    blk = lambda off: pl.BlockSpec((1, t, cb), lambda i, j: (i, 0, off // cb + j))
    return pl.pallas_call(
        _sconv_kernel,
        grid=(n, ch // cb),
        in_specs=[blk(col_b), blk(col_c), blk(col_x),
                  pl.BlockSpec((1, width - 1, cb), lambda i, j: (i, 0, j)),
                  pl.BlockSpec((width, cb), lambda i, j: (0, j)),
                  pl.BlockSpec(memory_space=pl.ANY)],
        out_specs=[blk(col_out),
                   pl.BlockSpec((1, width - 1, cb), lambda i, j: (i, 0, j))],
        out_shape=[jax.ShapeDtypeStruct(mix.shape, mix.dtype),
                   jax.ShapeDtypeStruct((n, width - 1, ch), F32)],
        scratch_shapes=[pltpu.VMEM((t + SUBLANES, cb), F32)],
        input_output_aliases={5: 0},
        compiler_params=_params("parallel", "parallel"),
        name="sconv",
    )(proj, proj, proj, state, w, mix)


def _softplus(x):
    return jnp.maximum(x, 0.0) + jnp.log1p(jnp.exp(-jnp.abs(x)))


def _sigmoid(x):
    return 0.5 * (jnp.tanh(0.5 * x) + 1.0)


def _silu(x):
    h = 0.5 * x
    return h + h * jnp.tanh(h)


def _bdot(a, b):
    return jnp.dot(a.astype(BF16), b.astype(BF16), preferred_element_type=F32)


def _gdn_kernel(x16_ref, ss_ref, q_ref, k_ref, v_ref, gate_ref, cst_ref, cw_ref, wab_ref, wabt_ref,
                al_ref, alt_ref, dt_ref, dtt_ref, s0_ref, ow_ref, o_ref, sout_ref, s_scr, buf):
    nh = al_ref.shape[1]
    ell, hw = q_ref.shape[1], q_ref.shape[2]
    hd = hw // nh
    width = cw_ref.shape[0]
    lo = SUBLANES - (width - 1)
    cidx = pl.program_id(1)
    heads = range(nh)
    each = lambda f, *cols: [f(*x) for x in zip(*cols)]
    nt = (((1,), (1,)), ((), ()))
    tn = (((0,), (0,)), ((), ()))
    dot_nt = lambda a, b: lax.dot_general(a, b, nt, preferred_element_type=F32)
    dot_nn = lambda a, b: jnp.dot(a, b, preferred_element_type=F32)
    to16 = lambda a: a.astype(BF16)
    l2n = lambda y: y * lax.rsqrt(jnp.sum(y * y, axis=-1, keepdims=True) + EPS)

    def conv_silu(c):
        buf[c, SUBLANES:SUBLANES + ell, :] = (q_ref, k_ref, v_ref)[c // nh][0, :, (c % nh) * hd:(c % nh + 1) * hd]
        acc = buf[c, lo:lo + ell, :] * cw_ref[0:1, c * hd:(c + 1) * hd]
        for t in range(1, width):
            acc = acc + buf[c, lo + t:lo + t + ell, :] * cw_ref[t:t + 1, c * hd:(c + 1) * hd]
        buf[c, lo:SUBLANES, :] = buf[c, SUBLANES + ell - (width - 1):SUBLANES + ell, :]
        return _silu(acc)

    @pl.when(cidx == 0)
    def _():
        s_scr[...] = s0_ref[0]
        for c in range(3 * nh):
            buf[c, lo:SUBLANES, :] = cst_ref[0, :, c * hd:(c + 1) * hd]

    i = lax.broadcasted_iota(jnp.int32, (ell, ell), 0)
    j = lax.broadcasted_iota(jnp.int32, (ell, ell), 1)
    hx = x16_ref[0]
    r_col = _row_rms_scale(ss_ref[0], hx.shape[1])
    r_row = jnp.sum(jnp.where(i == j, jnp.broadcast_to(r_col, (ell, ell)), 0.0), axis=0, keepdims=True)
    ab = dot_nn(hx, wab_ref[...]) * r_col
    abt = dot_nt(wabt_ref[...], hx) * r_row
    cum_cols = jnp.dot((i >= j).astype(F32), -jnp.exp(al_ref[...]) * _softplus(ab[:, :nh] + dt_ref[...]),
                       preferred_element_type=F32, precision=lax.Precision.HIGHEST)
    cum_rows = jnp.dot(-jnp.exp(alt_ref[...]) * _softplus(abt[:nh, :] + dtt_ref[...]), (i <= j).astype(F32),
                       preferred_element_type=F32, precision=lax.Precision.HIGHEST)
    beta_cols = _sigmoid(ab[:, nh:])
    beta_rows = _sigmoid(abt[nh:, :])
    g_row = [cum_rows[h:h + 1, :] for h in heads]
    b_row = [beta_rows[h:h + 1, :] for h in heads]
    g_col = [jnp.broadcast_to(cum_cols[:, h:h + 1], (ell, hd)) for h in heads]
    b_col = [jnp.broadcast_to(beta_cols[:, h:h + 1], (ell, ell)) for h in heads]

    q = [l2n(conv_silu(h)) * (hd ** -0.5) for h in heads]
    k = [l2n(conv_silu(nh + h)) for h in heads]
    v = [conv_silu(2 * nh + h) for h in heads]
    k16 = each(to16, k)
    kk = each(dot_nt, k16, k16)
    qk = each(dot_nt, each(to16, q), k16)
    decay = each(lambda gc, gr: jnp.exp(jnp.where(i >= j, gc[:, :ell] - gr, -jnp.inf)), g_col, g_row)
    a = each(lambda bc, x, d: jnp.where(i > j, bc * x * d, 0.0), b_col, kk, decay)
    eye = (i == j).astype(F32)
    shifts = range(3, ell.bit_length())
    blk = [jnp.right_shift(i, sh) == jnp.right_shift(j, sh) for sh in shifts]
    p = each(lambda x: -jnp.where(blk[0], x, 0.0), a)
    tm = each(lambda x: eye + x, p)
    for _ in range(2):
        p = each(_bdot, p, p)
        tm = each(lambda t, x: t + _bdot(t, x), tm, p)
    for lvl in range(1, len(blk)):
        ring = jnp.logical_and(blk[lvl], jnp.logical_not(blk[lvl - 1]))
        tl = each(lambda t, x: _bdot(t, jnp.where(ring, x, 0.0)), tm, a)
        tm = each(lambda t, x: t - _bdot(x, t), tm, tl)
    u_base = each(lambda t, b, x: _bdot(t * b, x), tm, b_row, v)
    w16 = each(lambda t, b, g, x: to16(_bdot(t * (b * jnp.exp(g)), x)), tm, b_row, g_row, k16)
    qd16 = each(lambda x, g: to16(x * jnp.exp(g)), q, g_col)
    kd16 = each(lambda x, g: to16(x * jnp.exp(g[ell - 1:ell, :] - g)), k, g_col)
    aqk16 = each(lambda x, d: to16(x * d), qk, decay)

    s = [s_scr[h] for h in heads]
    s16 = each(to16, s)
    u16 = each(lambda ub, w, x: to16(ub - dot_nn(w, x)), u_base, w16, s16)
    o_s = each(dot_nn, qd16, s16)
    o_u = each(dot_nn, aqk16, u16)
    ds = each(lambda kd, x: lax.dot_general(kd, x, tn, preferred_element_type=F32), kd16, u16)
    for h in heads:
        s_scr[h] = s[h] * jnp.exp(g_col[h][ell - 1:ell, :]) + ds[h]
        o = o_s[h] + o_u[h]
        o = o * lax.rsqrt(jnp.mean(o * o, axis=-1, keepdims=True) + EPS) * ow_ref[...]
        gt = gate_ref[0, :, h * hd:(h + 1) * hd]
        o_ref[0, :, h * hd:(h + 1) * hd] = (o * _silu(gt)).astype(o_ref.dtype)

    @pl.when(cidx == pl.num_programs(1) - 1)
    def _():
        sout_ref[0] = s_scr[...]


def _gdn_mixer(x16, ss, proj, conv_state, conv_w, w_ab, a_log, dt_bias, s0, o_norm_w, ell):
    n, t, d = x16.shape
    nh, hd = a_log.shape[0], o_norm_w.shape[0]
    hw = nh * hd
    width = conv_w.shape[0]
    col = lambda c: pl.BlockSpec((1, ell, hw), lambda i, j: (i, j, c))
    full = lambda shape: pl.BlockSpec(shape, lambda i, j: (0,) * len(shape))
    st_spec = pl.BlockSpec((1, nh, hd, hd), lambda i, j: (i, 0, 0, 0))
    return pl.pallas_call(
        _gdn_kernel,
        grid=(n, t // ell),
        in_specs=[pl.BlockSpec((1, ell, d), lambda i, j: (i, j, 0)),
                  pl.BlockSpec((1, ell, LANES), lambda i, j: (i, j, 0)), col(0), col(1), col(2), col(3),
                  pl.BlockSpec((1, width - 1, 3 * hw), lambda i, j: (i, 0, 0)), full((width, 3 * hw)),
                  full((d, 2 * nh)), full((2 * nh, d)),
                  full((1, nh)), full((nh, 1)), full((1, nh)), full((nh, 1)),
                  st_spec, full((1, hd))],
        out_specs=[pl.BlockSpec((1, ell, hw), lambda i, j: (i, j, 0)), st_spec],
        out_shape=[jax.ShapeDtypeStruct((n, t, hw), BF16),
                   jax.ShapeDtypeStruct((n, nh, hd, hd), F32)],
        scratch_shapes=[pltpu.VMEM((nh, hd, hd), F32), pltpu.VMEM((3 * nh, ell + SUBLANES, hd), F32)],
        compiler_params=_params("parallel", "arbitrary"),
        name="gdn_mixer",
    )(x16, ss, proj, proj, proj, proj, conv_state, conv_w, w_ab, w_ab.T,
      a_log.reshape(1, nh), a_log.reshape(nh, 1), dt_bias.reshape(1, nh), dt_bias.reshape(nh, 1),
      s0, o_norm_w.reshape(1, hd))


def _swa_sconv_layer(stream, n, t, pos0, k_cache, v_cache, conv_state, w_in, sinks, conv_w, w_out,
                     side_casts=()):
    x, x16, ss = stream
    hosted_out = isinstance(w_out, CastJob)
    jobs = list(side_casts) + ([w_out] if hosted_out else [])
    ch = conv_w.shape[1]
    aw = (w_out.w.shape[1] if hosted_out else w_out.shape[0]) - ch
    kvw = (w_in.shape[1] - aw - 3 * ch) // 2
    hd = aw // sinks.shape[0]
    proj, casts = _with_casts(_matmul(x16, w_in, row_ss=ss, side_casts=jobs, out_dtype=BF16,
                                      name="in_proj_a"), len(jobs))
    if hosted_out:
        w_out = casts.pop()
    proj = proj.reshape(n, t, -1)
    cos_t, sin_t = _rope_tables(t, pos0, hd)
    k_new = _rope_k(proj, cos_t, sin_t, aw, kvw, hd)
    v_new = proj[:, t - min(t, WINDOW):, aw + kvw:aw + 2 * kvw].astype(F32)
    attn = functools.partial(_attention, cos_t=cos_t, sin_t=sin_t, sinks=sinks,
                             aw=aw, kvw=kvw, hd=hd, out_width=aw + ch)
    if k_cache is None:
        k_all, v_win = k_new, v_new
        mix = attn(proj, k_new, 0, proj, aw + kvw, window=WINDOW)
    else:
        k_all = jnp.concatenate([k_cache, k_new], axis=1)
        v_all = jnp.concatenate([v_cache, v_new], axis=1)
        v_win = v_all
        mix = attn(proj, k_all, 0, v_all, 0, window=0)
    col_b = aw + 2 * kvw
    mix, new_state = _sconv(proj, conv_state, conv_w, mix, col_b, col_b + ch, col_b + 2 * ch, aw)
    stream = _matmul(mix.reshape(n * t, aw + ch), w_out, res=x, emit_norm=True, name="out_proj_a")
    return stream, k_all[:, -WINDOW:], v_win[:, -WINDOW:], new_state, casts, w_out


def _gdn_layer(stream, n, t, conv_state, s0, w_in, w_ab, conv_w, a_log, dt_bias, o_norm_w, w_out, ell,
               side_casts=()):
    x, x16, ss = stream
    d = x.shape[1]
    hw = a_log.shape[0] * o_norm_w.shape[0]
    hosted_out = isinstance(w_out, CastJob)
    jobs = list(side_casts) + ([w_out] if hosted_out else [])
    proj, casts = _with_casts(_matmul(x16, w_in, row_ss=ss, side_casts=jobs, name="in_proj_g"), len(jobs))
    if hosted_out:
        w_out = casts.pop()
    proj = proj.reshape(n, t, 4 * hw)
    o, s_new = _gdn_mixer(x16.reshape(n, t, d), ss.reshape(n, t, LANES), proj, conv_state, conv_w, w_ab,
                          a_log, dt_bias, s0, o_norm_w, ell)
    stream = _matmul(o.reshape(n * t, hw), w_out, res=x, emit_norm=True, name="out_proj_g")
    width = conv_w.shape[0]
    return stream, proj[:, t - (width - 1):, :3 * hw], s_new, casts, w_out


def _mlp(stream, w_up, w_down, last, side_casts=()):
    x, x16, ss = stream
    jobs = ([w_down] if isinstance(w_down, CastJob) else []) + list(side_casts)
    hid, casts = _with_casts(_matmul(x16, w_up, act="relu2", row_ss=ss, side_casts=jobs,
                                     out_dtype=BF16, name="mlp_up"), len(jobs))
    if isinstance(w_down, CastJob):
        w_down = casts.pop(0)
    if last:
        return (_matmul(hid, w_down, res=x, name="mlp_down"), None, None), w_down, casts
    return _matmul(hid, w_down, res=x, emit_norm=True, name="mlp_down"), w_down, casts


def kernel(x_prompt, x_sample, cache_swa_k, cache_swa_v, state_sconv, state_dn_conv, state_dn,
           attn_norm, w_in_a, sinks, sconv_w, w_out_a,
           dn_norm, w_in_g, dn_conv_w, A_log, dt_bias, o_norm_w, w_out_g,
           mlp_norm, w_up, w_down, final_norm):
    (nb, tp, d), (nd, ts, _) = x_prompt.shape, x_sample.shape
    depth = mlp_norm.shape[0]
    n_kv, hd_a = cache_swa_k.shape[3], cache_swa_k.shape[4]
    nh_g, hd_g = state_dn.shape[2], state_dn.shape[3]
    sp = _norm_prep(x_prompt.reshape(nb * tp, d))
    ss = _norm_prep(x_sample.reshape(nd * ts, d))
    outs = [[] for _ in range(10)]
    hw_g = nh_g * hd_g
    w_in_g_t = jnp.swapaxes(w_in_g, 1, 2)

    def in_proj_job(li):
        if li % 2 == 0:
            return CastJob(w_in_a, li // 2, attn_norm[li // 2])
        return CastJob(w_in_g_t, li // 2, dn_norm[li // 2], True, 4 * hw_g)

    w_in = _run_cast(in_proj_job(0))
    for li in range(depth):
        j = li // 2
        up_job = [CastJob(w_up, li, mlp_norm[li])]
        next_job = [in_proj_job(li + 1)] if li + 1 < depth else []
        if li % 2 == 0:
            zero_state = jnp.zeros((nb,) + state_sconv.shape[2:], F32)
            sp, kp, vp, cp, (wu,), w_out = _swa_sconv_layer(sp, nb, tp, 0, None, None, zero_state, w_in,
                                                            sinks[j], sconv_w[j], CastJob(w_out_a, j), up_job)
            kc = cache_swa_k[j].reshape(nd, -1, n_kv * hd_a)
            vc = cache_swa_v[j].reshape(nd, -1, n_kv * hd_a)
            ss, ks, vs, cs, _, _ = _swa_sconv_layer(ss, nd, ts, PAST_LEN, kc, vc, state_sconv[j], w_in,
                                                    sinks[j], sconv_w[j], w_out)
            shape5 = lambda a: a.reshape(a.shape[0], a.shape[1], n_kv, hd_a)
            for lst, val in zip(outs[:6], (shape5(kp), shape5(vp), shape5(ks), shape5(vs), cp, cs)):
                lst.append(val)
        else:
            w_ab = _to_bf16(w_in_g_t, j, 4 * hw_g, gain=dn_norm[j], transposed=True)
            zero_conv = jnp.zeros((nb,) + state_dn_conv.shape[2:], F32)
            zero_s = jnp.zeros((nb, nh_g, hd_g, hd_g), F32)
            sp, dcp, dsp, (wu,), w_out = _gdn_layer(sp, nb, tp, zero_conv, zero_s, w_in, w_ab, dn_conv_w[j],
                                                    A_log[j], dt_bias[j], o_norm_w[j], CastJob(w_out_g, j),
                                                    CHUNK, up_job)
            ss, dcs, dss, _, _ = _gdn_layer(ss, nd, ts, state_dn_conv[j], state_dn[j], w_in, w_ab, dn_conv_w[j],
                                            A_log[j], dt_bias[j], o_norm_w[j], w_out, ts)
            for lst, val in zip(outs[6:], (dcp, dcs, dsp, dss)):
                lst.append(val)
        last = li == depth - 1
        sp, wd, nxt = _mlp(sp, wu, CastJob(w_down, li), last, next_job)
        ss, _, _ = _mlp(ss, wu, wd, last)
        w_in = nxt[0] if nxt else None
    y_prompt = _rmsnorm(sp[0], final_norm, F32).reshape(x_prompt.shape)
    y_sample = _rmsnorm(ss[0], final_norm, F32).reshape(x_sample.shape)
    return (y_prompt, y_sample) + tuple(jnp.stack(o, 0) for o in outs)
```
